```python
import math
import jax, jax.numpy as jnp
from jax import lax
import numpy as np

D_MODEL = 2048
BATCH = 4
SEQ = 4096
DEPTH = 2

HEAD_DIM = 64
A_Q_HEADS = 16
A_KV_HEADS = 2
A_GROUP = A_Q_HEADS // A_KV_HEADS
A_WINDOW = 128
B_HEADS = 16
B_BRANCHES = ((128, 1), (512, 4), (2048, 16))
A_WIDTH = A_Q_HEADS * HEAD_DIM
A_KV_WIDTH = A_KV_HEADS * HEAD_DIM
B_WIDTH = B_HEADS * HEAD_DIM
MIX_WIDTH = A_WIDTH + B_WIDTH
IN_WIDTH = A_WIDTH + 2 * A_KV_WIDTH + 3 * B_WIDTH
IN_SPLITS = [A_WIDTH, A_WIDTH + A_KV_WIDTH, A_WIDTH + 2 * A_KV_WIDTH,
             A_WIDTH + 2 * A_KV_WIDTH + B_WIDTH, A_WIDTH + 2 * A_KV_WIDTH + 2 * B_WIDTH]
BLK = 128

MEM_LEN = 256
MEM_HEADS = 4
MEM_HEAD_DIM = D_MODEL // MEM_HEADS

N_EXPERTS = 64
TOP_K = 8
N_GROUPS = 8
TOPK_GROUPS = 4
D_EXPERT = 512
D_SHARED = 512
ROUTE_SCALE = 2.5
MOE_BLOCK = 128

ALPHA = (2.0 * DEPTH) ** 0.25
BETA = (8.0 * DEPTH) ** -0.25
LN_EPS = 1e-5
RMS_EPS = 1e-6

kernel_name = "hybrid_swa_sink_dilated_moe_deepnorm"


def layer_norm(x, g, b):
    xf = x.astype(jnp.float32)
    mu = jnp.mean(xf, axis=-1, keepdims=True)
    xc = xf - mu
    var = jnp.mean(xc * xc, axis=-1, keepdims=True)
    return (xc * lax.rsqrt(var + LN_EPS) * g + b).astype(x.dtype)


def rms_norm(x, g):
    xf = x.astype(jnp.float32)
    return xf * lax.rsqrt(jnp.mean(xf * xf, axis=-1, keepdims=True) + RMS_EPS) * g


def alibi_slopes(n):
    return 2.0 ** (-8.0 * jnp.arange(1, n + 1, dtype=jnp.float32) / n)


def banded_attention(q, k, v, max_offset, step, slopes, sinks):
    b, hk, g, L, hd = q.shape
    Lp = -(-L // BLK) * BLK
    pad = Lp - L
    nb = Lp // BLK
    q = jnp.pad(q, ((0, 0), (0, 0), (0, 0), (0, pad), (0, 0)))
    k = jnp.pad(k, ((0, 0), (0, 0), (BLK, pad), (0, 0)))
    v = jnp.pad(v, ((0, 0), (0, 0), (BLK, pad), (0, 0)))
    qb = q.reshape(b, hk, g, nb, BLK, hd)
    kb = jnp.concatenate([k[:, :, :Lp].reshape(b, hk, nb, BLK, hd),
                          k[:, :, BLK:].reshape(b, hk, nb, BLK, hd)], axis=3)
    vb = jnp.concatenate([v[:, :, :Lp].reshape(b, hk, nb, BLK, hd),
                          v[:, :, BLK:].reshape(b, hk, nb, BLK, hd)], axis=3)
    i = np.arange(BLK)[:, None]
    j = np.arange(2 * BLK)[None, :]
    offset = i + BLK - j
    band = (offset >= 0) & (offset <= max_offset)
    first = (np.arange(nb) == 0)[:, None, None]
    valid = jnp.asarray(band[None] & ~(first & (j < BLK)[None]))
    bias = -(slopes[:, :, None, None, None] * (step * jnp.asarray(offset, dtype=jnp.float32)))
    logits = jnp.einsum('bkgnqd,bknsd->bkgnqs', qb, kb,
                        preferred_element_type=jnp.float32) * (1.0 / math.sqrt(hd)) + bias
    logits = jnp.where(valid, logits, -jnp.inf)
    if sinks is not None:
        sink_col = jnp.broadcast_to(sinks.astype(jnp.float32)[None, :, :, None, None, None],
                                    (b, hk, g, nb, BLK, 1))
        logits = jnp.concatenate([logits, sink_col], axis=-1)
    m = jnp.max(logits, axis=-1, keepdims=True)
    e = jnp.exp(logits - m)
    s = jnp.sum(e, axis=-1, keepdims=True)
    probs = (e / s)[..., :2 * BLK]
    lse = (m + jnp.log(s))[..., 0]
    out = jnp.einsum('bkgnqs,bknsd->bkgnqd', probs, vb.astype(jnp.float32))
    out = out.reshape(b, hk, g, Lp, hd)[:, :, :, :L]
    lse = lse.reshape(b, hk, g, Lp)[:, :, :, :L]
    return out, lse


def dilated_branch(q, k, v, window, dilation, slopes):
    b, h, L, hd = q.shape
    Ld = L // dilation

    def to_res(t):
        return t.reshape(b, h, Ld, dilation, hd).transpose(0, 3, 1, 2, 4).reshape(b * dilation, h, Ld, hd)

    o, lse = banded_attention(to_res(q)[:, :, None], to_res(k), to_res(v),
                              window // dilation, dilation, slopes[:, None], None)
    o = o[:, :, 0].reshape(b, dilation, h, Ld, hd).transpose(0, 2, 3, 1, 4).reshape(b, h, L, hd)
    lse = lse[:, :, 0].reshape(b, dilation, h, Ld).transpose(0, 2, 3, 1).reshape(b, h, L)
    return o, lse


def mixing_sublayer(h, w_in, a_sinks, g_a, g_b, w_out):
    b, L, _ = h.shape
    proj = h @ w_in
    qa, ka, va, qb, kb, vb = jnp.split(proj, IN_SPLITS, axis=-1)
    qa = qa.reshape(b, L, A_KV_HEADS, A_GROUP, HEAD_DIM).transpose(0, 2, 3, 1, 4)
    ka = ka.reshape(b, L, A_KV_HEADS, HEAD_DIM).transpose(0, 2, 1, 3)
    va = va.reshape(b, L, A_KV_HEADS, HEAD_DIM).transpose(0, 2, 1, 3)
    oa, _ = banded_attention(qa, ka, va, A_WINDOW - 1, 1,
                             alibi_slopes(A_Q_HEADS).reshape(A_KV_HEADS, A_GROUP),
                             a_sinks.reshape(A_KV_HEADS, A_GROUP))
    oa = oa.transpose(0, 3, 1, 2, 4).reshape(b, L, A_WIDTH)
    qb = qb.reshape(b, L, B_HEADS, HEAD_DIM).transpose(0, 2, 1, 3)
    kb = kb.reshape(b, L, B_HEADS, HEAD_DIM).transpose(0, 2, 1, 3)
    vb = vb.reshape(b, L, B_HEADS, HEAD_DIM).transpose(0, 2, 1, 3)
    slopes_b = alibi_slopes(B_HEADS)
    outs, lses = [], []
    for window, dilation in B_BRANCHES:
        o, lse = dilated_branch(qb, kb, vb, window, dilation, slopes_b)
        outs.append(o)
        lses.append(lse)
    wts = jax.nn.softmax(jnp.stack(lses, axis=0), axis=0)
    ob = jnp.sum(wts[..., None] * jnp.stack(outs, axis=0), axis=0)
    ob = ob.transpose(0, 2, 1, 3).reshape(b, L, B_WIDTH)
    mixed = jnp.concatenate([rms_norm(oa, g_a), rms_norm(ob, g_b)], axis=-1).astype(h.dtype)
    return mixed @ w_out


def memory_sublayer(h, mem, wq, wk, wv, wo):
    b, L, d = h.shape
    m = mem.shape[1]
    q = (h @ wq).reshape(b, L, MEM_HEADS, MEM_HEAD_DIM)
    k = (mem @ wk).reshape(b, m, MEM_HEADS, MEM_HEAD_DIM)
    v = (mem @ wv).reshape(b, m, MEM_HEADS, MEM_HEAD_DIM)
    logits = jnp.einsum('bqhd,bkhd->bhqk', q, k,
                        preferred_element_type=jnp.float32) * (1.0 / math.sqrt(MEM_HEAD_DIM))
    p = jax.nn.softmax(logits, axis=-1)
    o = jnp.einsum('bhqk,bkhd->bqhd', p, v.astype(jnp.float32)).reshape(b, L, d)
    return o.astype(h.dtype) @ wo


def moe_sublayer(h, w_router, router_bias, w_gate, w_up, w_down, ws_gate, ws_up, ws_down):
    b, L, d = h.shape
    xt = h.reshape(-1, d)
    n = xt.shape[0]
    scores = jax.nn.sigmoid((xt @ w_router).astype(jnp.float32))
    sel = scores + router_bias.astype(jnp.float32)
    grp = sel.reshape(n, N_GROUPS, N_EXPERTS // N_GROUPS)
    grp_score = jnp.sum(lax.top_k(grp, 2)[0], axis=-1)
    _, top_groups = lax.top_k(grp_score, TOPK_GROUPS)
    group_mask = jnp.sum(jax.nn.one_hot(top_groups, N_GROUPS), axis=1) > 0
    expert_mask = jnp.repeat(group_mask, N_EXPERTS // N_GROUPS, axis=1)
    _, idx = lax.top_k(jnp.where(expert_mask, sel, -jnp.inf), TOP_K)
    gate = jnp.take_along_axis(scores, idx, axis=1)
    gate = gate / jnp.sum(gate, axis=-1, keepdims=True) * ROUTE_SCALE
    na = n * TOP_K
    flat_e = idx.reshape(-1)
    flat_tok = jnp.repeat(jnp.arange(n, dtype=jnp.int32), TOP_K)
    order = jnp.argsort(flat_e)
    se, stok, sgate = flat_e[order], flat_tok[order], gate.reshape(-1)[order]
    counts = jnp.bincount(flat_e, length=N_EXPERTS).astype(jnp.int32)
    starts = jnp.cumsum(counts) - counts
    padded = (counts + MOE_BLOCK - 1) // MOE_BLOCK * MOE_BLOCK
    padded_ends = jnp.cumsum(padded)
    padded_starts = padded_ends - padded
    pos = padded_starts[se] + (jnp.arange(na, dtype=jnp.int32) - starts[se])
    n_blocks = -(-na // MOE_BLOCK) + N_EXPERTS
    rows = n_blocks * MOE_BLOCK
    row_token = jnp.zeros((rows,), jnp.int32).at[pos].set(stok)
    row_gate = jnp.zeros((rows,), jnp.float32).at[pos].set(sgate)
    block_expert = jnp.minimum(
        jnp.searchsorted(padded_ends, jnp.arange(n_blocks, dtype=jnp.int32) * MOE_BLOCK, side='right'),
        N_EXPERTS - 1).astype(jnp.int32)

    def expert_block(blk):
        tok, g, e = blk
        xb = xt[tok]
        hid = jax.nn.silu(xb @ w_gate[e]) * (xb @ w_up[e])
        return (hid @ w_down[e]) * g[:, None].astype(xt.dtype)

    y_rows = lax.map(expert_block, (row_token.reshape(n_blocks, MOE_BLOCK),
                                    row_gate.reshape(n_blocks, MOE_BLOCK), block_expert))
    routed = jax.ops.segment_sum(y_rows.reshape(rows, d), row_token, num_segments=n)
    shared = (jax.nn.silu(xt @ ws_gate) * (xt @ ws_up)) @ ws_down
    return (routed + shared).reshape(b, L, d)


def setup_inputs(seed: int = 0) -> dict:
    key = jax.random.key(seed)
    ks = jax.random.split(key, 24)
    f32 = jnp.float32

    def nrm(k, shape, scale):
        return jax.random.normal(k, shape, f32) * scale

    def gain(k, shape):
        return 1.0 + 0.02 * jax.random.normal(k, shape, f32)

    return {
        "x": jax.random.normal(ks[0], (BATCH, SEQ, D_MODEL), f32),
        "mem": jax.random.normal(ks[1], (BATCH, MEM_LEN, D_MODEL), f32),
        "w_in": nrm(ks[2], (DEPTH, D_MODEL, IN_WIDTH), D_MODEL ** -0.5),
        "a_sinks": nrm(ks[3], (DEPTH, A_Q_HEADS), 1.0),
        "g_a": gain(ks[4], (DEPTH, A_WIDTH)),
        "g_b": gain(ks[5], (DEPTH, B_WIDTH)),
        "w_out": nrm(ks[6], (DEPTH, MIX_WIDTH, D_MODEL), BETA * MIX_WIDTH ** -0.5),
        "ln1_g": gain(ks[7], (DEPTH, D_MODEL)),
        "ln1_b": nrm(ks[8], (DEPTH, D_MODEL), 0.02),
        "wq_m": nrm(ks[9], (DEPTH, D_MODEL, D_MODEL), D_MODEL ** -0.5),
        "wk_m": nrm(ks[10], (DEPTH, D_MODEL, D_MODEL), D_MODEL ** -0.5),
        "wv_m": nrm(ks[11], (DEPTH, D_MODEL, D_MODEL), D_MODEL ** -0.5),
        "wo_m": nrm(ks[12], (DEPTH, D_MODEL, D_MODEL), BETA * D_MODEL ** -0.5),
        "ln2_g": gain(ks[13], (DEPTH, D_MODEL)),
        "ln2_b": nrm(ks[14], (DEPTH, D_MODEL), 0.02),
        "w_router": nrm(ks[15], (DEPTH, D_MODEL, N_EXPERTS), D_MODEL ** -0.5),
        "router_bias": nrm(ks[16], (DEPTH, N_EXPERTS), 0.01),
        "w_gate": nrm(ks[17], (DEPTH, N_EXPERTS, D_MODEL, D_EXPERT), D_MODEL ** -0.5),
        "w_up": nrm(ks[18], (DEPTH, N_EXPERTS, D_MODEL, D_EXPERT), D_MODEL ** -0.5),
        "w_down": nrm(ks[19], (DEPTH, N_EXPERTS, D_EXPERT, D_MODEL), BETA * D_EXPERT ** -0.5),
        "ws_gate": nrm(ks[20], (DEPTH, D_MODEL, D_SHARED), D_MODEL ** -0.5),
        "ws_up": nrm(ks[21], (DEPTH, D_MODEL, D_SHARED), D_MODEL ** -0.5),
        "ws_down": nrm(ks[22], (DEPTH, D_SHARED, D_MODEL), BETA * D_SHARED ** -0.5),
        "ln3_g": gain(ks[23], (DEPTH, D_MODEL)),
        "ln3_b": nrm(jax.random.fold_in(ks[23], 1), (DEPTH, D_MODEL), 0.02),
    }


def reference(x, mem, w_in, a_sinks, g_a, g_b, w_out, ln1_g, ln1_b,
              wq_m, wk_m, wv_m, wo_m, ln2_g, ln2_b,
              w_router, router_bias, w_gate, w_up, w_down, ws_gate, ws_up, ws_down,
              ln3_g, ln3_b):
    h = x
    for l in range(DEPTH):
        h = layer_norm(ALPHA * h + mixing_sublayer(h, w_in[l], a_sinks[l], g_a[l], g_b[l], w_out[l]),
                       ln1_g[l], ln1_b[l])
        h = layer_norm(ALPHA * h + memory_sublayer(h, mem, wq_m[l], wk_m[l], wv_m[l], wo_m[l]),
                       ln2_g[l], ln2_b[l])
        h = layer_norm(ALPHA * h + moe_sublayer(h, w_router[l], router_bias[l], w_gate[l], w_up[l],
                                                w_down[l], ws_gate[l], ws_up[l], ws_down[l]),
                       ln3_g[l], ln3_b[l])
    return h
```

```python
import functools
import math

import jax
import jax.numpy as jnp
from jax import lax
from jax.experimental import pallas as pl
from jax.experimental.pallas import tpu as pltpu

F32 = jnp.float32
BF16 = jnp.bfloat16

HEAD_DIM = 64
A_Q_HEADS = 16
A_KV_HEADS = 2
A_WINDOW = 128
B_HEADS = 16
B_BRANCH_DILATIONS = (1, 4, 16)
A_WIDTH = A_Q_HEADS * HEAD_DIM
A_KV_WIDTH = A_KV_HEADS * HEAD_DIM
B_WIDTH = B_HEADS * HEAD_DIM
A_PROJ_WIDTH = A_WIDTH + 2 * A_KV_WIDTH
BLK = 128

MEM_HEADS = 4

N_EXPERTS = 64
TOP_K = 8
N_GROUPS = 8
GROUP_SIZE = N_EXPERTS // N_GROUPS
TOPK_GROUPS = 4
ROUTE_SCALE = 2.5

DEPTH = 2
ALPHA = (2.0 * DEPTH) ** 0.25
LN_EPS = 1e-5
RMS_EPS = 1e-6

LANES = 128
NEG = -1e30

MOE_ROWS = 256
COMBINE_ROWS = 128
VMEM_LIMIT = 56 * 1024 * 1024


def _params(sem, vmem=VMEM_LIMIT):
    return pltpu.CompilerParams(dimension_semantics=sem, vmem_limit_bytes=vmem)


def _dot(a, b):
    return jnp.dot(a, b, preferred_element_type=F32)


def _dot_nt(a, b):
    return lax.dot_general(a, b, (((1,), (1,)), ((), ())), preferred_element_type=F32)


def _layer_norm(y, g, b):
    mu = jnp.mean(y, axis=-1, keepdims=True)
    yc = y - mu
    var = jnp.mean(yc * yc, axis=-1, keepdims=True)
    return yc * lax.rsqrt(var + LN_EPS) * g + b


def _mm_kernel(x_ref, w_ref, o_ref):
    o_ref[...] = _dot(x_ref[...], w_ref[...]).astype(o_ref.dtype)


def _matmul(x, w, tm, tn, out_dtype):
    m, k = x.shape
    n = w.shape[1]
    return pl.pallas_call(
        _mm_kernel,
        grid=(n // tn, m // tm),
        in_specs=[pl.BlockSpec((tm, k), lambda j, i: (i, 0)),
                  pl.BlockSpec((k, tn), lambda j, i: (0, j))],
        out_specs=pl.BlockSpec((tm, tn), lambda j, i: (i, j)),
        out_shape=jax.ShapeDtypeStruct((m, n), out_dtype),
        compiler_params=_params(("arbitrary", "arbitrary")),
        name="matmul",
    )(x, w)


def _band_tiles():
    row = lax.broadcasted_iota(jnp.int32, (BLK, BLK), 0)
    col = lax.broadcasted_iota(jnp.int32, (BLK, BLK), 1)
    dist_cur = (row - col).astype(F32)
    dist_prev = dist_cur + float(BLK)
    return row, col, dist_cur, dist_prev


def _roll_half(t):
    return pltpu.roll(t.astype(F32), LANES // 2, axis=1).astype(BF16)


def _attn_a_kernel(slopes_ref, sinks_ref, q_ref, kc_ref, kp_ref, vc_ref, vp_ref, g_ref,
                   o_ref, o_scr):
    n = pl.program_id(1)
    row, col, dist_cur, dist_prev = _band_tiles()
    lo = col < HEAD_DIM
    valid_cur = row >= col
    valid_prev = jnp.logical_and(col > row, n > 0)

    k_cur = (kc_ref[0], _roll_half(kc_ref[0]))
    k_prev = (kp_ref[0], _roll_half(kp_ref[0]))
    v_cur = (vc_ref[0], _roll_half(vc_ref[0]))
    v_prev = (vp_ref[0], _roll_half(vp_ref[0]))

    ss = jnp.zeros((BLK, 1), F32)
    pairs_per_kv = A_Q_HEADS // A_KV_HEADS // 2
    for t in range(A_Q_HEADS // 2):
        kv = t // pairs_per_kv
        qt = q_ref[0, :, t * LANES:(t + 1) * LANES] * jnp.asarray(0.125, BF16)
        halves = []
        for half in range(2):
            head = 2 * t + half
            ver = kv ^ half
            qh = jnp.where(lo if half == 0 else jnp.logical_not(lo), qt, jnp.zeros_like(qt))
            slope = slopes_ref[head]
            l_cur = jnp.where(valid_cur, _dot_nt(qh, k_cur[ver]) - slope * dist_cur, NEG)
            l_prev = jnp.where(valid_prev, _dot_nt(qh, k_prev[ver]) - slope * dist_prev, NEG)
            sink = sinks_ref[head]
            m = jnp.maximum(jnp.maximum(jnp.max(l_cur, axis=1, keepdims=True),
                                        jnp.max(l_prev, axis=1, keepdims=True)), sink)
            e_cur = jnp.exp(l_cur - m)
            e_prev = jnp.exp(l_prev - m)
            s = (jnp.sum(e_cur, axis=1, keepdims=True) + jnp.sum(e_prev, axis=1, keepdims=True)
                 + jnp.exp(sink - m))
            acc = _dot(e_cur.astype(BF16), v_cur[ver]) + _dot(e_prev.astype(BF16), v_prev[ver])
            halves.append(acc / s)
        o_tile = jnp.where(lo, halves[0], halves[1])
        ss = ss + jnp.sum(o_tile * o_tile, axis=1, keepdims=True)
        o_scr[:, t * LANES:(t + 1) * LANES] = o_tile
    scale = lax.rsqrt(ss * (1.0 / A_WIDTH) + RMS_EPS)
    o_ref[0] = (o_scr[...] * scale * g_ref[...]).astype(o_ref.dtype)


def _attn_a(proj_a, slopes, sinks, g_a):
    b, length, _ = proj_a.shape
    kcol = A_WIDTH // LANES
    vcol = kcol + 1
    grid_spec = pltpu.PrefetchScalarGridSpec(
        num_scalar_prefetch=2,
        grid=(b, length // BLK),
        in_specs=[
            pl.BlockSpec((1, BLK, A_WIDTH), lambda i, n, *_: (i, n, 0)),
            pl.BlockSpec((1, BLK, LANES), lambda i, n, *_: (i, n, kcol)),
            pl.BlockSpec((1, BLK, LANES), lambda i, n, *_: (i, jnp.maximum(n - 1, 0), kcol)),
            pl.BlockSpec((1, BLK, LANES), lambda i, n, *_: (i, n, vcol)),
            pl.BlockSpec((1, BLK, LANES), lambda i, n, *_: (i, jnp.maximum(n - 1, 0), vcol)),
            pl.BlockSpec((1, A_WIDTH), lambda i, n, *_: (0, 0)),
        ],
        out_specs=pl.BlockSpec((1, BLK, A_WIDTH), lambda i, n, *_: (i, n, 0)),
        scratch_shapes=[pltpu.VMEM((BLK, A_WIDTH), F32)],
    )
    return pl.pallas_call(
        _attn_a_kernel,
        grid_spec=grid_spec,
        out_shape=jax.ShapeDtypeStruct((b, length, A_WIDTH), BF16),
        compiler_params=_params(("arbitrary", "arbitrary")),
        name="attn_a",
    )(slopes, sinks, proj_a, proj_a, proj_a, proj_a, proj_a, g_a.reshape(1, A_WIDTH))


def _attn_b_kernel(slopes_ref, q_ref, k_ref, v_ref, o_ref, acc_scr, m_scr, s_scr):
    hp = pl.program_id(1)
    length = q_ref.shape[1]
    row, col, dist_cur, dist_prev = _band_tiles()
    lo = col < HEAD_DIM
    valid_cur = row >= col
    band_prev = col >= row
    slope = (slopes_ref[2 * hp], slopes_ref[2 * hp + 1])

    for branch, dil in enumerate(B_BRANCH_DILATIONS):
        blocks_per_class = length // (dil * BLK)

        def rows(start, dil=dil):
            if dil == 1:
                return pl.ds(start, BLK)
            return pl.ds(start, BLK, stride=dil)

        def body(it, carry, branch=branch, dil=dil, blocks_per_class=blocks_per_class, rows=rows):
            res = it // blocks_per_class
            blk = it % blocks_per_class
            start_cur = res + dil * BLK * blk
            start_prev = res + dil * BLK * jnp.maximum(blk - 1, 0)
            valid_prev = jnp.logical_and(band_prev, blk > 0)
            q2 = (q_ref[0, rows(start_cur), :] * 0.125).astype(BF16)
            k_cur = k_ref[0, rows(start_cur), :].astype(BF16)
            k_prev = k_ref[0, rows(start_prev), :].astype(BF16)
            v_cur = v_ref[0, rows(start_cur), :].astype(BF16)
            v_prev = v_ref[0, rows(start_prev), :].astype(BF16)
            stats = []
            for half in range(2):
                qh = jnp.where(lo if half == 0 else jnp.logical_not(lo), q2, jnp.zeros_like(q2))
                step = slope[half] * float(dil)
                l_cur = jnp.where(valid_cur, _dot_nt(qh, k_cur) - step * dist_cur, NEG)
                l_prev = jnp.where(valid_prev, _dot_nt(qh, k_prev) - step * dist_prev, NEG)
                m = jnp.maximum(jnp.max(l_cur, axis=1, keepdims=True),
                                jnp.max(l_prev, axis=1, keepdims=True))
                e_cur = jnp.exp(l_cur - m)
                e_prev = jnp.exp(l_prev - m)
                s = jnp.sum(e_cur, axis=1, keepdims=True) + jnp.sum(e_prev, axis=1, keepdims=True)
                acc = _dot(e_cur.astype(BF16), v_cur) + _dot(e_prev.astype(BF16), v_prev)
                stats.append((m, s, acc))
            m_t = jnp.where(lo, stats[0][0], stats[1][0])
            s_t = jnp.where(lo, stats[0][1], stats[1][1])
            acc_t = jnp.where(lo, stats[0][2], stats[1][2])
            dst = rows(start_cur)
            if branch == 0:
                m_scr[dst, :] = m_t
                s_scr[dst, :] = s_t
                acc_scr[dst, :] = acc_t
            else:
                m_old = m_scr[dst, :]
                m_new = jnp.maximum(m_old, m_t)
                w_old = jnp.exp(m_old - m_new)
                w_blk = jnp.exp(m_t - m_new)
                m_scr[dst, :] = m_new
                s_scr[dst, :] = s_scr[dst, :] * w_old + s_t * w_blk
                acc_scr[dst, :] = acc_scr[dst, :] * w_old + acc_t * w_blk
            return carry

        lax.fori_loop(0, length // BLK, body, 0)

    o_ref[0] = acc_scr[...] / s_scr[...]


def _attn_b(proj_b, slopes):
    b, length, _ = proj_b.shape
    pairs = B_WIDTH // LANES
    grid_spec = pltpu.PrefetchScalarGridSpec(
        num_scalar_prefetch=1,
        grid=(b, pairs),
        in_specs=[
            pl.BlockSpec((1, length, LANES), lambda i, p, *_: (i, 0, p)),
            pl.BlockSpec((1, length, LANES), lambda i, p, *_: (i, 0, pairs + p)),
            pl.BlockSpec((1, length, LANES), lambda i, p, *_: (i, 0, 2 * pairs + p)),
        ],
        out_specs=pl.BlockSpec((1, length, LANES), lambda i, p, *_: (i, 0, p)),
        scratch_shapes=[pltpu.VMEM((length, LANES), F32)] * 3,
    )
    return pl.pallas_call(
        _attn_b_kernel,
        grid_spec=grid_spec,
        out_shape=jax.ShapeDtypeStruct((b, length, B_WIDTH), F32),
        compiler_params=_params(("arbitrary", "arbitrary")),
        name="attn_b",
    )(slopes, proj_b, proj_b, proj_b)


def _mix_out_kernel(xa_ref, xb_ref, gb_ref, wa_ref, wb_ref, h_ref, g_ref, b_ref,
                    o_ref, obf_ref):
    xb = xb_ref[...]
    scale = lax.rsqrt(jnp.mean(xb * xb, axis=-1, keepdims=True) + RMS_EPS)
    xb = (xb * scale * gb_ref[...]).astype(BF16)
    y = _dot(xa_ref[...], wa_ref[...]) + _dot(xb, wb_ref[...])
    out = _layer_norm(ALPHA * h_ref[...] + y, g_ref[...], b_ref[...])
    o_ref[...] = out
    obf_ref[...] = out.astype(BF16)


def _proj_out_kernel(x_ref, w_ref, h_ref, g_ref, b_ref, o_ref, obf_ref):
    y = _dot(x_ref[...], w_ref[...])
    out = _layer_norm(ALPHA * h_ref[...] + y, g_ref[...], b_ref[...])
    o_ref[...] = out
    obf_ref[...] = out.astype(BF16)


def _const_spec(shape):
    return pl.BlockSpec(shape, lambda i: (0,) * len(shape), pipeline_mode=pl.Buffered(1))


def _mix_out(xa, xb, g_b, w_a, w_b, h, ln_g, ln_b, tm=512):
    n, d = h.shape
    ka = xa.shape[1]
    kb = xb.shape[1]
    return pl.pallas_call(
        _mix_out_kernel,
        grid=(n // tm,),
        in_specs=[pl.BlockSpec((tm, ka), lambda i: (i, 0)),
                  pl.BlockSpec((tm, kb), lambda i: (i, 0)),
                  _const_spec((1, kb)),
                  _const_spec((ka, d)),
                  _const_spec((kb, d)),
                  pl.BlockSpec((tm, d), lambda i: (i, 0)),
                  _const_spec((1, d)),
                  _const_spec((1, d))],
        out_specs=[pl.BlockSpec((tm, d), lambda i: (i, 0)),
                   pl.BlockSpec((tm, d), lambda i: (i, 0))],
        out_shape=[jax.ShapeDtypeStruct((n, d), F32), jax.ShapeDtypeStruct((n, d), BF16)],
        compiler_params=_params(("arbitrary",)),
        name="mix_out",
    )(xa, xb, g_b.reshape(1, kb), w_a, w_b, h, ln_g.reshape(1, d), ln_b.reshape(1, d))


def _proj_out(x, w, h, ln_g, ln_b, tm=512):
    n, d = h.shape
    k = x.shape[1]
    return pl.pallas_call(
        _proj_out_kernel,
        grid=(n // tm,),
        in_specs=[pl.BlockSpec((tm, k), lambda i: (i, 0)),
                  _const_spec((k, d)),
                  pl.BlockSpec((tm, d), lambda i: (i, 0)),
                  _const_spec((1, d)),
                  _const_spec((1, d))],
        out_specs=[pl.BlockSpec((tm, d), lambda i: (i, 0)),
                   pl.BlockSpec((tm, d), lambda i: (i, 0))],
        out_shape=[jax.ShapeDtypeStruct((n, d), F32), jax.ShapeDtypeStruct((n, d), BF16)],
        compiler_params=_params(("arbitrary",)),
        name="proj_out",
    )(x, w, h, ln_g.reshape(1, d), ln_b.reshape(1, d))


def _mem_attn_kernel(x_ref, wq_ref, k_ref, v_ref, o_ref):
    d = x_ref.shape[2]
    hd = d // MEM_HEADS
    q = _dot(x_ref[0], wq_ref[...]) * (1.0 / math.sqrt(hd))
    for head in range(MEM_HEADS):
        cols = slice(head * hd, (head + 1) * hd)
        logits = _dot_nt(q[:, cols].astype(BF16), k_ref[0, :, cols])
        m = jnp.max(logits, axis=1, keepdims=True)
        e = jnp.exp(logits - m)
        s = jnp.sum(e, axis=1, keepdims=True)
        o = _dot(e.astype(BF16), v_ref[0, :, cols]) / s
        o_ref[0, :, cols] = o.astype(o_ref.dtype)


def _mem_attn(x, wq, km, vm, tm=512):
    b, length, d = x.shape
    mlen = km.shape[1]
    return pl.pallas_call(
        _mem_attn_kernel,
        grid=(b, length // tm),
        in_specs=[pl.BlockSpec((1, tm, d), lambda i, j: (i, j, 0)),
                  pl.BlockSpec((d, d), lambda i, j: (0, 0), pipeline_mode=pl.Buffered(1)),
                  pl.BlockSpec((1, mlen, d), lambda i, j: (i, 0, 0)),
                  pl.BlockSpec((1, mlen, d), lambda i, j: (i, 0, 0))],
        out_specs=pl.BlockSpec((1, tm, d), lambda i, j: (i, j, 0)),
        out_shape=jax.ShapeDtypeStruct((b, length, d), BF16),
        compiler_params=_params(("arbitrary", "arbitrary")),
        name="mem_attn",
    )(x, wq, km, vm)


def _split_bf16(a):
    hi = a.astype(BF16)
    lo = (a - hi.astype(F32)).astype(BF16)
    return hi, lo


def _router_kernel(x_ref, wt_ref, bias_ref, idx_ref, gate_ref, mask_ref):
    tm = x_ref.shape[0]
    xh, xl = _split_bf16(x_ref[...])
    wh, wl = _split_bf16(wt_ref[...])
    logits = _dot_nt(wh, xh) + (_dot_nt(wh, xl) + _dot_nt(wl, xh))
    scores = jax.nn.sigmoid(logits)
    sel = scores + bias_ref[...]

    member = lax.broadcasted_iota(jnp.int32, (GROUP_SIZE, tm), 0)
    group_scores = []
    for g in range(N_GROUPS):
        v = sel[g * GROUP_SIZE:(g + 1) * GROUP_SIZE, :]
        m1 = jnp.max(v, axis=0, keepdims=True)
        first = jnp.min(jnp.where(v == m1, member, GROUP_SIZE), axis=0, keepdims=True)
        m2 = jnp.max(jnp.where(member == first, -jnp.inf, v), axis=0, keepdims=True)
        group_scores.append(m1 + m2)
    masked = []
    for g in range(N_GROUPS):
        beaten_by = jnp.zeros((1, tm), jnp.int32)
        for o in range(N_GROUPS):
            if o == g:
                continue
            wins = group_scores[o] > group_scores[g]
            if o < g:
                wins = jnp.logical_or(wins, group_scores[o] == group_scores[g])
            beaten_by = beaten_by + wins.astype(jnp.int32)
        keep = beaten_by < TOPK_GROUPS
        masked.append(jnp.where(keep, sel[g * GROUP_SIZE:(g + 1) * GROUP_SIZE, :], -jnp.inf))
    cand = jnp.concatenate(masked, axis=0)

    expert = lax.broadcasted_iota(jnp.int32, (N_EXPERTS, tm), 0)
    beaten_by = jnp.zeros((N_EXPERTS, tm), jnp.int32)
    for o in range(N_EXPERTS):
        other = cand[o:o + 1, :]
        wins = jnp.logical_or(other > cand, jnp.logical_and(other == cand, expert > o))
        beaten_by = beaten_by + wins.astype(jnp.int32)
    chosen = beaten_by < TOP_K
    chosen_i = chosen.astype(jnp.int32)
    mask_ref[...] = chosen_i

    picked = jnp.where(chosen, scores, 0.0)
    gate = picked / jnp.sum(picked, axis=0, keepdims=True) * ROUTE_SCALE

    slot = jnp.zeros((N_EXPERTS, tm), jnp.int32)
    running = jnp.zeros((1, tm), jnp.int32)
    slots = []
    for g in range(N_GROUPS):
        part = chosen_i[g * GROUP_SIZE:(g + 1) * GROUP_SIZE, :]
        rows_ = []
        for r in range(GROUP_SIZE):
            rows_.append(running)
            running = running + part[r:r + 1, :]
        slots.append(jnp.concatenate(rows_, axis=0))
    slot = jnp.concatenate(slots, axis=0)
    idx_rows = []
    gate_rows = []
    for k in range(TOP_K):
        hit = jnp.logical_and(chosen, slot == k)
        idx_rows.append(jnp.sum(jnp.where(hit, expert, 0), axis=0, keepdims=True))
        gate_rows.append(jnp.sum(jnp.where(hit, gate, 0.0), axis=0, keepdims=True))
    idx_ref[...] = jnp.concatenate(idx_rows, axis=0)
    gate_ref[...] = jnp.concatenate(gate_rows, axis=0)


def _router(x, w_router_t, bias, tm=512):
    n, d = x.shape
    return pl.pallas_call(
        _router_kernel,
        grid=(n // tm,),
        in_specs=[pl.BlockSpec((tm, d), lambda i: (i, 0)),
                  pl.BlockSpec((N_EXPERTS, d), lambda i: (0, 0)),
                  pl.BlockSpec((N_EXPERTS, 1), lambda i: (0, 0))],
        out_specs=[pl.BlockSpec((TOP_K, tm), lambda i: (0, i)),
                   pl.BlockSpec((TOP_K, tm), lambda i: (0, i)),
                   pl.BlockSpec((N_EXPERTS, tm), lambda i: (0, i))],
        out_shape=[jax.ShapeDtypeStruct((TOP_K, n), jnp.int32),
                   jax.ShapeDtypeStruct((TOP_K, n), F32),
                   jax.ShapeDtypeStruct((N_EXPERTS, n), jnp.int32)],
        compiler_params=_params(("arbitrary",)),
        name="router",
    )(x, w_router_t, bias.reshape(N_EXPERTS, 1))


def _gather_rows(tok_smem, src_hbm, dst, sem, count):
    def issue(r, carry):
        pltpu.make_async_copy(src_hbm.at[pl.ds(tok_smem[0, r], 1)],
                              dst.at[pl.ds(r, 1)], sem).start()
        return carry
    lax.fori_loop(0, count, issue, 0, unroll=8)


def _experts_kernel(be_ref, used_ref, tok_hbm, gate_ref, x_hbm, wg_ref, wu_ref, wd_ref,
                    y_ref, xbuf, tok_smem, wg_bf, wu_bf, wd_bf, row_sem, tok_sem):
    i = pl.program_id(0)
    used = used_ref[0]
    rows = xbuf.shape[1]
    slot = i % 2

    def tokens_copy(block, s):
        return pltpu.make_async_copy(tok_hbm.at[block], tok_smem.at[s], tok_sem.at[s])

    def start_gather(s):
        _gather_rows(tok_smem.at[s], x_hbm, xbuf.at[s], row_sem.at[s], rows)

    @pl.when(jnp.logical_and(i == 0, used > 0))
    def _():
        tokens_copy(0, 0).start()
        tokens_copy(0, 0).wait()
        start_gather(0)

        @pl.when(used > 1)
        def _():
            tokens_copy(1, 1).start()

    @pl.when(i < used)
    def _():
        @pl.when(i + 1 < used)
        def _():
            tokens_copy(i + 1, 1 - slot).wait()
            start_gather(1 - slot)

            @pl.when(i + 2 < used)
            def _():
                tokens_copy(i + 2, slot).start()

        changed = jnp.logical_or(i == 0, be_ref[jnp.maximum(i - 1, 0)] != be_ref[i])

        @pl.when(changed)
        def _():
            wg_bf[...] = wg_ref[0].astype(BF16)
            wu_bf[...] = wu_ref[0].astype(BF16)
            wd_bf[...] = wd_ref[0].astype(BF16)

        pltpu.make_async_copy(x_hbm.at[pl.ds(0, rows)], xbuf.at[slot], row_sem.at[slot]).wait()
        x = xbuf[slot].astype(BF16)
        hid = jax.nn.silu(_dot(x, wg_bf[...])) * _dot(x, wu_bf[...])
        y_ref[...] = _dot(hid.astype(BF16), wd_bf[...]) * gate_ref[...]

    @pl.when(i >= used)
    def _():
        y_ref[...] = jnp.zeros_like(y_ref)


def _experts(block_expert, used, row_token, row_gate, x, w_gate, w_up, w_down):
    nb = block_expert.shape[0]
    rows = MOE_ROWS
    d = x.shape[1]
    de = w_gate.shape[2]
    grid_spec = pltpu.PrefetchScalarGridSpec(
        num_scalar_prefetch=2,
        grid=(nb,),
        in_specs=[
            pl.BlockSpec(memory_space=pl.ANY),
            pl.BlockSpec((rows, 1), lambda i, be, u: (i, 0)),
            pl.BlockSpec(memory_space=pl.ANY),
            pl.BlockSpec((1, d, de), lambda i, be, u: (be[i], 0, 0)),
            pl.BlockSpec((1, d, de), lambda i, be, u: (be[i], 0, 0)),
            pl.BlockSpec((1, de, d), lambda i, be, u: (be[i], 0, 0)),
        ],
        out_specs=pl.BlockSpec((rows, d), lambda i, be, u: (i, 0)),
        scratch_shapes=[
            pltpu.VMEM((2, rows, d), F32),
            pltpu.SMEM((2, 1, rows), jnp.int32),
            pltpu.VMEM((d, de), BF16),
            pltpu.VMEM((d, de), BF16),
            pltpu.VMEM((de, d), BF16),
            pltpu.SemaphoreType.DMA((2,)),
            pltpu.SemaphoreType.DMA((2,)),
        ],
    )
    return pl.pallas_call(
        _experts_kernel,
        grid_spec=grid_spec,
        out_shape=jax.ShapeDtypeStruct((nb * rows, d), F32),
        compiler_params=_params(("arbitrary",)),
        name="experts",
    )(block_expert, used, row_token.reshape(nb, 1, rows), row_gate.reshape(nb * rows, 1), x,
      w_gate, w_up, w_down)


def _combine_kernel(pos_hbm, y_hbm, h_ref, wsg_ref, wsu_ref, wsd_ref, g_ref, b_ref,
                    o_ref, obf_ref, ybuf, pos_smem, row_sem, pos_sem):
    i = pl.program_id(0)
    steps = pl.num_programs(0)
    tm = h_ref.shape[0]
    slot = i % 2

    def start_gather(block, s):
        cp = pltpu.make_async_copy(pos_hbm.at[block], pos_smem.at[s], pos_sem.at[s])
        cp.start()
        cp.wait()
        for k in range(TOP_K):
            def issue(r, carry, k=k):
                pltpu.make_async_copy(y_hbm.at[pl.ds(pos_smem[s, k, r], 1)],
                                      ybuf.at[s, k, pl.ds(r, 1)], row_sem.at[s]).start()
                return carry
            lax.fori_loop(0, tm, issue, 0, unroll=8)

    @pl.when(i == 0)
    def _():
        start_gather(0, 0)

    @pl.when(i + 1 < steps)
    def _():
        start_gather(i + 1, 1 - slot)

    h = h_ref[...]
    x = h.astype(BF16)
    hid = jax.nn.silu(_dot(x, wsg_ref[...])) * _dot(x, wsu_ref[...])
    y = ALPHA * h + _dot(hid.astype(BF16), wsd_ref[...])

    for k in range(TOP_K):
        pltpu.make_async_copy(y_hbm.at[pl.ds(0, tm)], ybuf.at[slot, k], row_sem.at[slot]).wait()
    for k in range(TOP_K):
        y = y + ybuf[slot, k]
    out = _layer_norm(y, g_ref[...], b_ref[...])
    o_ref[...] = out
    obf_ref[...] = out.astype(BF16)


def _combine(pos, y_rows, h, ws_gate, ws_up, ws_down, ln_g, ln_b):
    n, d = h.shape
    tm = COMBINE_ROWS
    ds = ws_gate.shape[1]
    return pl.pallas_call(
        _combine_kernel,
        grid=(n // tm,),
        in_specs=[pl.BlockSpec(memory_space=pl.ANY),
                  pl.BlockSpec(memory_space=pl.ANY),
                  pl.BlockSpec((tm, d), lambda i: (i, 0)),
                  _const_spec((d, ds)),
                  _const_spec((d, ds)),
                  _const_spec((ds, d)),
                  _const_spec((1, d)),
                  _const_spec((1, d))],
        out_specs=[pl.BlockSpec((tm, d), lambda i: (i, 0)),
                   pl.BlockSpec((tm, d), lambda i: (i, 0))],
        out_shape=[jax.ShapeDtypeStruct((n, d), F32), jax.ShapeDtypeStruct((n, d), BF16)],
        scratch_shapes=[pltpu.VMEM((2, TOP_K, tm, d), F32),
                        pltpu.SMEM((2, TOP_K, tm), jnp.int32),
                        pltpu.SemaphoreType.DMA((2,)),
                        pltpu.SemaphoreType.DMA((2,))],
        compiler_params=_params(("arbitrary",)),
        name="combine",
    )(pos.reshape(TOP_K, n // tm, tm).transpose(1, 0, 2), y_rows, h,
      ws_gate, ws_up, ws_down, ln_g.reshape(1, d), ln_b.reshape(1, d))


def _routing_plan(idx_t, gate_t, mask_t):
    n = idx_t.shape[1]
    rows = MOE_ROWS
    nb = (n * TOP_K) // rows + N_EXPERTS
    counts = jnp.sum(mask_t, axis=1)
    padded = (counts + rows - 1) // rows * rows
    padded_ends = jnp.cumsum(padded)
    padded_starts = padded_ends - padded
    rank = jnp.cumsum(mask_t, axis=1) - mask_t
    pos = padded_starts[idx_t] + jnp.take_along_axis(rank, idx_t, axis=0)
    tok = jnp.broadcast_to(jnp.arange(n, dtype=jnp.int32)[None, :], pos.shape)
    flat = pos.reshape(-1)
    row_token = jnp.zeros((nb * rows,), jnp.int32).at[flat].set(
        tok.reshape(-1), unique_indices=True)
    row_gate = jnp.zeros((nb * rows,), F32).at[flat].set(
        gate_t.reshape(-1), unique_indices=True)
    used = (padded_ends[-1] // rows).astype(jnp.int32)
    block_start = jnp.arange(nb, dtype=jnp.int32) * rows
    block_expert = jnp.minimum(jnp.searchsorted(padded_ends, block_start, side="right"),
                               N_EXPERTS - 1).astype(jnp.int32)
    last = block_expert[jnp.maximum(used - 1, 0)]
    block_expert = jnp.where(jnp.arange(nb) < used, block_expert, last)
    return pos.astype(jnp.int32), row_token, row_gate, block_expert, used.reshape(1)


def _alibi_slopes(n):
    return 2.0 ** (-8.0 * jnp.arange(1, n + 1, dtype=F32) / n)


def kernel(x, mem, w_in, a_sinks, g_a, g_b, w_out, ln1_g, ln1_b, wq_m, wk_m, wv_m, wo_m,
           ln2_g, ln2_b, w_router, router_bias, w_gate, w_up, w_down, ws_gate, ws_up, ws_down,
           ln3_g, ln3_b):
    b, length, d = x.shape
    n = b * length
    mlen = mem.shape[1]
    depth = w_in.shape[0]
    h = x.reshape(n, d)
    h_bf = h.astype(BF16)
    mem_bf = mem.reshape(b * mlen, d).astype(BF16)
    slopes_a = _alibi_slopes(A_Q_HEADS)
    slopes_b = _alibi_slopes(B_HEADS)

    for l in range(depth):
        w_in_bf = w_in[l].astype(BF16)
        proj_a = _matmul(h_bf, w_in_bf[:, :A_PROJ_WIDTH], 512, A_PROJ_WIDTH, BF16)
        proj_b = _matmul(h_bf, w_in_bf[:, A_PROJ_WIDTH:], 512, B_WIDTH, F32)
        mixed_a = _attn_a(proj_a.reshape(b, length, A_PROJ_WIDTH), slopes_a, a_sinks[l], g_a[l])
        out_b = _attn_b(proj_b.reshape(b, length, 3 * B_WIDTH), slopes_b)
        w_out_bf = w_out[l].astype(BF16)
        h, h_bf = _mix_out(mixed_a.reshape(n, A_WIDTH), out_b.reshape(n, B_WIDTH), g_b[l],
                           w_out_bf[:A_WIDTH], w_out_bf[A_WIDTH:], h, ln1_g[l], ln1_b[l])

        km = _matmul(mem_bf, wk_m[l].astype(BF16), b * mlen, d // 2, BF16)
        vm = _matmul(mem_bf, wv_m[l].astype(BF16), b * mlen, d // 2, BF16)
        o = _mem_attn(h_bf.reshape(b, length, d), wq_m[l].astype(BF16),
                      km.reshape(b, mlen, d), vm.reshape(b, mlen, d))
        h, h_bf = _proj_out(o.reshape(n, d), wo_m[l].astype(BF16), h, ln2_g[l], ln2_b[l])

        idx_t, gate_t, mask_t = _router(h, w_router[l].T, router_bias[l])
        pos, row_token, row_gate, block_expert, used = _routing_plan(idx_t, gate_t, mask_t)
        y_rows = _experts(block_expert, used, row_token, row_gate, h,
                          w_gate[l], w_up[l], w_down[l])
        h, h_bf = _combine(pos, y_rows, h, ws_gate[l].astype(BF16), ws_up[l].astype(BF16),
                           ws_down[l].astype(BF16), ln3_g[l], ln3_b[l])
    return h.reshape(b, length, d)
```

```python
import math

import jax
import jax.numpy as jnp
from jax import lax
from jax.experimental import pallas as pl
from jax.experimental.pallas import tpu as pltpu

F32 = jnp.float32
BF16 = jnp.bfloat16

HEAD_DIM = 64
A_Q_HEADS = 16
A_KV_HEADS = 2
A_WINDOW = 128
B_HEADS = 16
B_BRANCH_DILATIONS = (1, 4, 16)
A_WIDTH = A_Q_HEADS * HEAD_DIM
A_KV_WIDTH = A_KV_HEADS * HEAD_DIM
B_WIDTH = B_HEADS * HEAD_DIM
A_PROJ_WIDTH = A_WIDTH + 2 * A_KV_WIDTH
BLK = 128

MEM_HEADS = 4

N_EXPERTS = 64
TOP_K = 8
N_GROUPS = 8
GROUP_SIZE = N_EXPERTS // N_GROUPS
TOPK_GROUPS = 4
ROUTE_SCALE = 2.5

DEPTH = 2
ALPHA = (2.0 * DEPTH) ** 0.25
LN_EPS = 1e-5
RMS_EPS = 1e-6

LANES = 128
NEG = -1e30

MOE_ROWS = 256
COMBINE_ROWS = 128
VMEM_LIMIT = 56 * 1024 * 1024


def _params(sem, vmem=VMEM_LIMIT):
    return pltpu.CompilerParams(dimension_semantics=sem, vmem_limit_bytes=vmem)


def _dot(a, b):
    return jnp.dot(a, b, preferred_element_type=F32)


def _dot_nt(a, b):
    return lax.dot_general(a, b, (((1,), (1,)), ((), ())), preferred_element_type=F32)


def _layer_norm(y, g, b):
    mu = jnp.mean(y, axis=-1, keepdims=True)
    yc = y - mu
    var = jnp.mean(yc * yc, axis=-1, keepdims=True)
    return yc * lax.rsqrt(var + LN_EPS) * g + b


def _layer_spec(shape, layer, index=None, single=True):
    index = index or (0,) * len(shape)
    mode = dict(pipeline_mode=pl.Buffered(1)) if single else {}
    return pl.BlockSpec((None,) + tuple(shape), lambda *_: (layer,) + tuple(index), **mode)


def _mm_kernel(x_ref, w_ref, o_ref):
    o_ref[...] = _dot(x_ref[...], w_ref[...]).astype(o_ref.dtype)


def _matmul(x, w, layer, tm, tn, out_dtype):
    m, k = x.shape
    n = w.shape[2]
    return pl.pallas_call(
        _mm_kernel,
        grid=(n // tn, m // tm),
        in_specs=[pl.BlockSpec((tm, k), lambda j, i: (i, 0)),
                  pl.BlockSpec((None, k, tn), lambda j, i: (layer, 0, j))],
        out_specs=pl.BlockSpec((tm, tn), lambda j, i: (i, j)),
        out_shape=jax.ShapeDtypeStruct((m, n), out_dtype),
        compiler_params=_params(("arbitrary", "arbitrary")),
        name="matmul",
    )(x, w)


def _band_bias(stacked_heads, max_offset, slope_of_stack):
    rows = stacked_heads * BLK
    row = lax.broadcasted_iota(jnp.int32, (rows, 2 * BLK), 0)
    col = lax.broadcasted_iota(jnp.int32, (rows, 2 * BLK), 1)
    offset = (row & (BLK - 1)) + BLK - col
    slope = jnp.full((rows, 2 * BLK), slope_of_stack[stacked_heads - 1], F32)
    for a in range(stacked_heads - 2, -1, -1):
        slope = jnp.where(row < (a + 1) * BLK, slope_of_stack[a], slope)
    valid = jnp.logical_and(offset >= 0, offset <= max_offset)
    bias = jnp.where(valid, -slope * offset.astype(F32), NEG)
    return bias, jnp.where(col < BLK, NEG, bias)


def _roll_half(t):
    return pltpu.roll(t.astype(F32), LANES // 2, axis=1).astype(BF16)


A_STACK = 4


def _attn_a_kernel(slopes_ref, sinks_ref, q_ref, kc_ref, kp_ref, vc_ref, vp_ref, g_ref,
                   o_ref, o_scr, bias_scr):
    n = pl.program_id(1)
    tiles_per_kv = A_Q_HEADS // A_KV_HEADS // 2

    def head_of(kv, half, a):
        return 2 * (kv * tiles_per_kv + a) + half

    @pl.when(jnp.logical_and(pl.program_id(0) == 0, n == 0))
    def _():
        for kv in range(A_KV_HEADS):
            for half in range(2):
                slopes = [slopes_ref[head_of(kv, half, a)] for a in range(A_STACK)]
                bias, bias_first = _band_bias(A_STACK, A_WINDOW - 1, slopes)
                bias_scr[0, 2 * kv + half] = bias
                bias_scr[1, 2 * kv + half] = bias_first

    first = (n == 0).astype(jnp.int32)
    lane = lax.broadcasted_iota(jnp.int32, (BLK, LANES), 1)
    lo = lane < HEAD_DIM
    srow = lax.broadcasted_iota(jnp.int32, (A_STACK * BLK, 1), 0)

    k_ver = (jnp.concatenate([kp_ref[0], kc_ref[0]], axis=0),
             jnp.concatenate([_roll_half(kp_ref[0]), _roll_half(kc_ref[0])], axis=0))
    v_ver = (jnp.concatenate([vp_ref[0], vc_ref[0]], axis=0),
             jnp.concatenate([_roll_half(vp_ref[0]), _roll_half(vc_ref[0])], axis=0))

    ss = jnp.zeros((BLK, 1), F32)
    for kv in range(A_KV_HEADS):
        outs = []
        for half in range(2):
            keep = lo if half == 0 else jnp.logical_not(lo)
            parts = []
            for a in range(A_STACK):
                t = kv * tiles_per_kv + a
                qt = q_ref[0, :, t * LANES:(t + 1) * LANES] * jnp.asarray(0.125, BF16)
                parts.append(jnp.where(keep, qt, jnp.zeros_like(qt)))
            qs = jnp.concatenate(parts, axis=0)
            ver = kv ^ half
            logits = _dot_nt(qs, k_ver[ver]) + bias_scr[first, 2 * kv + half]
            sink = jnp.full((A_STACK * BLK, 1), sinks_ref[head_of(kv, half, A_STACK - 1)], F32)
            for a in range(A_STACK - 2, -1, -1):
                sink = jnp.where(srow < (a + 1) * BLK, sinks_ref[head_of(kv, half, a)], sink)
            m = jnp.maximum(jnp.max(logits, axis=1, keepdims=True), sink)
            e = jnp.exp(logits - m)
            s = jnp.sum(e, axis=1, keepdims=True) + jnp.exp(sink - m)
            acc = _dot(e.astype(BF16), v_ver[ver])
            outs.append(acc * (1.0 / s))
        for a in range(A_STACK):
            t = kv * tiles_per_kv + a
            o_tile = jnp.where(lo, outs[0][a * BLK:(a + 1) * BLK], outs[1][a * BLK:(a + 1) * BLK])
            ss = ss + jnp.sum(o_tile * o_tile, axis=1, keepdims=True)
            o_scr[:, t * LANES:(t + 1) * LANES] = o_tile
    scale = lax.rsqrt(ss * (1.0 / A_WIDTH) + RMS_EPS)
    o_ref[0] = (o_scr[...] * scale * g_ref[...]).astype(o_ref.dtype)


def _attn_a(proj_a, slopes, sinks, g_a):
    b, length, _ = proj_a.shape
    kcol = A_WIDTH // LANES
    vcol = kcol + 1
    grid_spec = pltpu.PrefetchScalarGridSpec(
        num_scalar_prefetch=2,
        grid=(b, length // BLK),
        in_specs=[
            pl.BlockSpec((1, BLK, A_WIDTH), lambda i, n, *_: (i, n, 0)),
            pl.BlockSpec((1, BLK, LANES), lambda i, n, *_: (i, n, kcol)),
            pl.BlockSpec((1, BLK, LANES), lambda i, n, *_: (i, jnp.maximum(n - 1, 0), kcol)),
            pl.BlockSpec((1, BLK, LANES), lambda i, n, *_: (i, n, vcol)),
            pl.BlockSpec((1, BLK, LANES), lambda i, n, *_: (i, jnp.maximum(n - 1, 0), vcol)),
            pl.BlockSpec((1, A_WIDTH), lambda i, n, *_: (0, 0)),
        ],
        out_specs=pl.BlockSpec((1, BLK, A_WIDTH), lambda i, n, *_: (i, n, 0)),
        scratch_shapes=[pltpu.VMEM((BLK, A_WIDTH), F32),
                        pltpu.VMEM((2, 2 * A_KV_HEADS, A_STACK * BLK, 2 * BLK), F32)],
    )
    return pl.pallas_call(
        _attn_a_kernel,
        grid_spec=grid_spec,
        out_shape=jax.ShapeDtypeStruct((b, length, A_WIDTH), BF16),
        compiler_params=_params(("arbitrary", "arbitrary")),
        name="attn_a",
    )(slopes, sinks, proj_a, proj_a, proj_a, proj_a, proj_a, g_a.reshape(1, A_WIDTH))


def _attn_b_kernel(slopes_ref, q_ref, k_ref, v_ref, o_ref, acc_scr, m_scr, s_scr, bias_scr):
    hp = pl.program_id(1)
    length = q_ref.shape[1]
    lane = lax.broadcasted_iota(jnp.int32, (BLK, LANES), 1)
    lo = lane < HEAD_DIM
    qrow = lax.broadcasted_iota(jnp.int32, (2 * BLK, LANES), 0)
    qlane = lax.broadcasted_iota(jnp.int32, (2 * BLK, LANES), 1)
    qmask = (qrow >= BLK) == (qlane >= HEAD_DIM)

    for branch, dil in enumerate(B_BRANCH_DILATIONS):
        slopes = [slopes_ref[2 * hp] * float(dil), slopes_ref[2 * hp + 1] * float(dil)]
        bias, bias_first = _band_bias(2, BLK, slopes)
        bias_scr[branch, 0] = bias
        bias_scr[branch, 1] = bias_first

    for branch, dil in enumerate(B_BRANCH_DILATIONS):
        blocks_per_class = length // (dil * BLK)

        def rows(start, dil=dil):
            if dil == 1:
                return pl.ds(start, BLK)
            return pl.ds(start, BLK, stride=dil)

        def body(it, carry, branch=branch, dil=dil, blocks_per_class=blocks_per_class, rows=rows):
            res = it // blocks_per_class
            blk = it % blocks_per_class
            cur = rows(res + dil * BLK * blk)
            prev = rows(res + dil * BLK * jnp.maximum(blk - 1, 0))
            first = jnp.where(blk == 0, 1, 0)
            q2 = q_ref[0, cur, :] * 0.125
            qs = jnp.concatenate([q2, q2], axis=0)
            qs = jnp.where(qmask, qs, 0.0).astype(BF16)
            k2 = jnp.concatenate([k_ref[0, prev, :], k_ref[0, cur, :]], axis=0).astype(BF16)
            v2 = jnp.concatenate([v_ref[0, prev, :], v_ref[0, cur, :]], axis=0).astype(BF16)
            logits = _dot_nt(qs, k2) + bias_scr[branch, first]
            m = jnp.max(logits, axis=1, keepdims=True)
            e = jnp.exp(logits - m)
            s = jnp.sum(e, axis=1, keepdims=True)
            acc = _dot(e.astype(BF16), v2)
            m_t = jnp.where(lo, m[:BLK], m[BLK:])
            s_t = jnp.where(lo, s[:BLK], s[BLK:])
            acc_t = jnp.where(lo, acc[:BLK], acc[BLK:])
            if branch == 0:
                m_scr[cur, :] = m_t
                s_scr[cur, :] = s_t
                acc_scr[cur, :] = acc_t
            else:
                m_old = m_scr[cur, :]
                s_old = s_scr[cur, :]
                acc_old = acc_scr[cur, :]
                m_new = jnp.maximum(m_old, m_t)
                w_old = jnp.exp(m_old - m_new)
                w_blk = jnp.exp(m_t - m_new)
                m_scr[cur, :] = m_new
                s_scr[cur, :] = s_old * w_old + s_t * w_blk
                acc_scr[cur, :] = acc_old * w_old + acc_t * w_blk
            return carry

        lax.fori_loop(0, length // BLK, body, 0, unroll=2)

    o_ref[0] = acc_scr[...] / s_scr[...]


def _attn_b(proj_b, slopes):
    b, length, _ = proj_b.shape
    pairs = B_WIDTH // LANES
    grid_spec = pltpu.PrefetchScalarGridSpec(
        num_scalar_prefetch=1,
        grid=(b, pairs),
        in_specs=[
            pl.BlockSpec((1, length, LANES), lambda i, p, *_: (i, 0, p)),
            pl.BlockSpec((1, length, LANES), lambda i, p, *_: (i, 0, pairs + p)),
            pl.BlockSpec((1, length, LANES), lambda i, p, *_: (i, 0, 2 * pairs + p)),
        ],
        out_specs=pl.BlockSpec((1, length, LANES), lambda i, p, *_: (i, 0, p)),
        scratch_shapes=[pltpu.VMEM((length, LANES), F32)] * 3
        + [pltpu.VMEM((len(B_BRANCH_DILATIONS), 2, 2 * BLK, 2 * BLK), F32)],
    )
    return pl.pallas_call(
        _attn_b_kernel,
        grid_spec=grid_spec,
        out_shape=jax.ShapeDtypeStruct((b, length, B_WIDTH), F32),
        compiler_params=_params(("arbitrary", "arbitrary")),
        name="attn_b",
    )(slopes, proj_b, proj_b, proj_b)


def _mix_out_kernel(xa_ref, xb_ref, gb_ref, wa_ref, wb_ref, h_ref, g_ref, b_ref,
                    o_ref, obf_ref):
    xb = xb_ref[...]
    scale = lax.rsqrt(jnp.mean(xb * xb, axis=-1, keepdims=True) + RMS_EPS)
    xb = (xb * scale * gb_ref[...]).astype(BF16)
    y = _dot(xa_ref[...], wa_ref[...]) + _dot(xb, wb_ref[...])
    out = _layer_norm(ALPHA * h_ref[...] + y, g_ref[...], b_ref[...])
    o_ref[...] = out
    obf_ref[...] = out.astype(BF16)


def _proj_out_kernel(x_ref, w_ref, h_ref, g_ref, b_ref, o_ref, obf_ref):
    y = _dot(x_ref[...], w_ref[...])
    out = _layer_norm(ALPHA * h_ref[...] + y, g_ref[...], b_ref[...])
    o_ref[...] = out
    obf_ref[...] = out.astype(BF16)


def _mix_out(xa, xb, g_b, w_out, h, ln_g, ln_b, layer, tm=512):
    n, d = h.shape
    ka = xa.shape[1]
    kb = xb.shape[1]
    return pl.pallas_call(
        _mix_out_kernel,
        grid=(n // tm,),
        in_specs=[pl.BlockSpec((tm, ka), lambda i: (i, 0)),
                  pl.BlockSpec((tm, kb), lambda i: (i, 0)),
                  _layer_spec((1, kb), layer),
                  _layer_spec((ka, d), layer, (0, 0)),
                  _layer_spec((kb, d), layer, (ka // kb, 0)),
                  pl.BlockSpec((tm, d), lambda i: (i, 0)),
                  _layer_spec((1, d), layer),
                  _layer_spec((1, d), layer)],
        out_specs=[pl.BlockSpec((tm, d), lambda i: (i, 0)),
                   pl.BlockSpec((tm, d), lambda i: (i, 0))],
        out_shape=[jax.ShapeDtypeStruct((n, d), F32), jax.ShapeDtypeStruct((n, d), BF16)],
        compiler_params=_params(("arbitrary",)),
        name="mix_out",
    )(xa, xb, g_b, w_out, w_out, h, ln_g, ln_b)


def _proj_out(x, w, h, ln_g, ln_b, layer, tm=512):
    n, d = h.shape
    k = x.shape[1]
    return pl.pallas_call(
        _proj_out_kernel,
        grid=(n // tm,),
        in_specs=[pl.BlockSpec((tm, k), lambda i: (i, 0)),
                  _layer_spec((k, d), layer),
                  pl.BlockSpec((tm, d), lambda i: (i, 0)),
                  _layer_spec((1, d), layer),
                  _layer_spec((1, d), layer)],
        out_specs=[pl.BlockSpec((tm, d), lambda i: (i, 0)),
                   pl.BlockSpec((tm, d), lambda i: (i, 0))],
        out_shape=[jax.ShapeDtypeStruct((n, d), F32), jax.ShapeDtypeStruct((n, d), BF16)],
        compiler_params=_params(("arbitrary",)),
        name="proj_out",
    )(x, w, h, ln_g, ln_b)


def _mem_attn_kernel(x_ref, wq_ref, k_ref, v_ref, o_ref):
    d = x_ref.shape[2]
    hd = d // MEM_HEADS
    q = _dot(x_ref[0], wq_ref[...]) * (1.0 / math.sqrt(hd))
    for head in range(MEM_HEADS):
        cols = slice(head * hd, (head + 1) * hd)
        logits = _dot_nt(q[:, cols].astype(BF16), k_ref[0, :, cols])
        m = jnp.max(logits, axis=1, keepdims=True)
        e = jnp.exp(logits - m)
        s = jnp.sum(e, axis=1, keepdims=True)
        o = _dot(e.astype(BF16), v_ref[0, :, cols]) / s
        o_ref[0, :, cols] = o.astype(o_ref.dtype)


def _mem_attn(x, wq, km, vm, layer, tm=512):
    b, length, d = x.shape
    mlen = km.shape[1]
    return pl.pallas_call(
        _mem_attn_kernel,
        grid=(b, length // tm),
        in_specs=[pl.BlockSpec((1, tm, d), lambda i, j: (i, j, 0)),
                  _layer_spec((d, d), layer),
                  pl.BlockSpec((1, mlen, d), lambda i, j: (i, 0, 0)),
                  pl.BlockSpec((1, mlen, d), lambda i, j: (i, 0, 0))],
        out_specs=pl.BlockSpec((1, tm, d), lambda i, j: (i, j, 0)),
        out_shape=jax.ShapeDtypeStruct((b, length, d), BF16),
        compiler_params=_params(("arbitrary", "arbitrary")),
        name="mem_attn",
    )(x, wq, km, vm)


def _split_bf16(a):
    hi = a.astype(BF16)
    lo = (a - hi.astype(F32)).astype(BF16)
    return hi, lo


def _router_kernel(x_ref, wt_ref, bias_ref, idx_ref, gate_ref, mask_ref, cnt_ref):
    tm = x_ref.shape[0]
    xh, xl = _split_bf16(x_ref[...])
    wh, wl = _split_bf16(wt_ref[...])
    logits = _dot_nt(wh, xh) + (_dot_nt(wh, xl) + _dot_nt(wl, xh))
    scores = jax.nn.sigmoid(logits)
    sel = scores + bias_ref[...]

    member = lax.broadcasted_iota(jnp.int32, (GROUP_SIZE, tm), 0)
    group_scores = []
    for g in range(N_GROUPS):
        v = sel[g * GROUP_SIZE:(g + 1) * GROUP_SIZE, :]
        m1 = jnp.max(v, axis=0, keepdims=True)
        first = jnp.min(jnp.where(v == m1, member, GROUP_SIZE), axis=0, keepdims=True)
        m2 = jnp.max(jnp.where(member == first, -jnp.inf, v), axis=0, keepdims=True)
        group_scores.append(m1 + m2)
    masked = []
    for g in range(N_GROUPS):
        beaten_by = jnp.zeros((1, tm), jnp.int32)
        for o in range(N_GROUPS):
            if o == g:
                continue
            wins = group_scores[o] > group_scores[g]
            if o < g:
                wins = jnp.logical_or(wins, group_scores[o] == group_scores[g])
            beaten_by = beaten_by + wins.astype(jnp.int32)
        keep = beaten_by < TOPK_GROUPS
        masked.append(jnp.where(keep, sel[g * GROUP_SIZE:(g + 1) * GROUP_SIZE, :], -jnp.inf))
    cand = jnp.concatenate(masked, axis=0)

    expert = lax.broadcasted_iota(jnp.int32, (N_EXPERTS, tm), 0)
    beaten_by = jnp.zeros((N_EXPERTS, tm), jnp.int32)
    for o in range(N_EXPERTS):
        other = cand[o:o + 1, :]
        wins = jnp.logical_or(other > cand, jnp.logical_and(other == cand, expert > o))
        beaten_by = beaten_by + wins.astype(jnp.int32)
    chosen = beaten_by < TOP_K
    chosen_i = chosen.astype(jnp.int32)
    mask_ref[...] = chosen_i
    cnt_ref[...] = jnp.broadcast_to(jnp.sum(chosen_i, axis=1, keepdims=True), cnt_ref.shape)

    picked = jnp.where(chosen, scores, 0.0)
    gate = picked / jnp.sum(picked, axis=0, keepdims=True) * ROUTE_SCALE

    running = jnp.zeros((1, tm), jnp.int32)
    slots = []
    for g in range(N_GROUPS):
        part = chosen_i[g * GROUP_SIZE:(g + 1) * GROUP_SIZE, :]
        rows_ = []
        for r in range(GROUP_SIZE):
            rows_.append(running)
            running = running + part[r:r + 1, :]
        slots.append(jnp.concatenate(rows_, axis=0))
    slot = jnp.concatenate(slots, axis=0)
    idx_rows = []
    gate_rows = []
    for k in range(TOP_K):
        hit = jnp.logical_and(chosen, slot == k)
        idx_rows.append(jnp.sum(jnp.where(hit, expert, 0), axis=0, keepdims=True))
        gate_rows.append(jnp.sum(jnp.where(hit, gate, 0.0), axis=0, keepdims=True))
    idx_ref[...] = jnp.concatenate(idx_rows, axis=0)
    gate_ref[...] = jnp.concatenate(gate_rows, axis=0)


def _router(x, w_router_t, bias, tm=512):
    n, d = x.shape
    return pl.pallas_call(
        _router_kernel,
        grid=(n // tm,),
        in_specs=[pl.BlockSpec((tm, d), lambda i: (i, 0)),
                  pl.BlockSpec((N_EXPERTS, d), lambda i: (0, 0)),
                  pl.BlockSpec((N_EXPERTS, 1), lambda i: (0, 0))],
        out_specs=[pl.BlockSpec((TOP_K, tm), lambda i: (0, i)),
                   pl.BlockSpec((TOP_K, tm), lambda i: (0, i)),
                   pl.BlockSpec((N_EXPERTS, tm), lambda i: (0, i)),
                   pl.BlockSpec((None, N_EXPERTS, LANES), lambda i: (i, 0, 0))],
        out_shape=[jax.ShapeDtypeStruct((TOP_K, n), jnp.int32),
                   jax.ShapeDtypeStruct((TOP_K, n), F32),
                   jax.ShapeDtypeStruct((N_EXPERTS, n), jnp.int32),
                   jax.ShapeDtypeStruct((n // tm, N_EXPERTS, LANES), jnp.int32)],
        compiler_params=_params(("arbitrary",)),
        name="router",
    )(x, w_router_t, bias.reshape(N_EXPERTS, 1))


def _plan_kernel(mask_ref, idx_ref, off_ref, pos_ref):
    tm = mask_ref.shape[1]
    before = (lax.broadcasted_iota(jnp.int32, (tm, tm), 0)
              < lax.broadcasted_iota(jnp.int32, (tm, tm), 1)).astype(BF16)
    rank = _dot(mask_ref[...].astype(BF16), before).astype(jnp.int32)
    row_of = off_ref[...] + rank
    expert = lax.broadcasted_iota(jnp.int32, (N_EXPERTS, tm), 0)
    rows_ = []
    for k in range(TOP_K):
        hit = expert == idx_ref[k:k + 1, :]
        rows_.append(jnp.sum(jnp.where(hit, row_of, 0), axis=0, keepdims=True))
    pos_ref[...] = jnp.concatenate(rows_, axis=0)


def _plan(mask_t, idx_t, tile_offsets, tm):
    n = mask_t.shape[1]
    return pl.pallas_call(
        _plan_kernel,
        grid=(n // tm,),
        in_specs=[pl.BlockSpec((N_EXPERTS, tm), lambda i: (0, i)),
                  pl.BlockSpec((TOP_K, tm), lambda i: (0, i)),
                  pl.BlockSpec((None, N_EXPERTS, 1), lambda i: (i, 0, 0))],
        out_specs=pl.BlockSpec((TOP_K, tm), lambda i: (0, i)),
        out_shape=jax.ShapeDtypeStruct((TOP_K, n), jnp.int32),
        compiler_params=_params(("arbitrary",)),
        name="plan",
    )(mask_t, idx_t, tile_offsets)


def _block_layout(counts_per_tile, n):
    rows = MOE_ROWS
    nb = (n * TOP_K) // rows + N_EXPERTS
    counts = jnp.sum(counts_per_tile, axis=0)
    padded = (counts + rows - 1) // rows * rows
    padded_ends = jnp.cumsum(padded)
    padded_starts = padded_ends - padded
    tile_offsets = padded_starts[None, :] + jnp.cumsum(counts_per_tile, axis=0) - counts_per_tile
    used = (padded_ends[-1] // rows).astype(jnp.int32)
    block_start = jnp.arange(nb, dtype=jnp.int32) * rows
    block_expert = jnp.minimum(jnp.searchsorted(padded_ends, block_start, side="right"),
                               N_EXPERTS - 1).astype(jnp.int32)
    last = block_expert[jnp.maximum(used - 1, 0)]
    block_expert = jnp.where(jnp.arange(nb) < used, block_expert, last)
    pad_from = (padded_starts + counts).astype(jnp.int32)
    return (tile_offsets.astype(jnp.int32)[:, :, None], block_expert, used.reshape(1),
            pad_from, padded_ends.astype(jnp.int32))


DISPATCH_ROWS = 128


def _dispatch_kernel(pad_from_ref, pad_to_ref, used_ref, pos_hbm, x_hbm, xs_hbm, pos_smem,
                     zeros, row_sem, pos_sem, pad_sem):
    i = pl.program_id(0)
    steps = pl.num_programs(0)
    tm = pos_smem.shape[2]
    slot = i % 2

    def wait_rows(s):
        for _ in range(TOP_K):
            pltpu.make_async_copy(x_hbm.at[pl.ds(0, tm)], xs_hbm.at[pl.ds(0, tm)],
                                  row_sem.at[s]).wait()

    @pl.when(i == 0)
    def _():
        zeros[...] = jnp.zeros_like(zeros)
        block = zeros.shape[0]

        def tail_copy(blk):
            start = pl.multiple_of(blk * block, block)
            return pltpu.make_async_copy(zeros, xs_hbm.at[pl.ds(start, block)], pad_sem)

        def tail_start(blk, c):
            tail_copy(blk).start()
            return c

        def tail_wait(blk, c):
            tail_copy(blk).wait()
            return c

        lax.fori_loop(used_ref[0], xs_hbm.shape[0] // block, tail_start, 0)
        lax.fori_loop(used_ref[0], xs_hbm.shape[0] // block, tail_wait, 0)

        def per_expert(e, carry):
            def pad_copy(r):
                return pltpu.make_async_copy(zeros.at[pl.ds(0, 1)], xs_hbm.at[pl.ds(r, 1)],
                                             pad_sem)

            def start(r, c):
                pad_copy(r).start()
                return c

            def wait(r, c):
                pad_copy(r).wait()
                return c

            lax.fori_loop(pad_from_ref[e], pad_to_ref[e], start, 0)
            lax.fori_loop(pad_from_ref[e], pad_to_ref[e], wait, 0)
            return carry

        lax.fori_loop(0, N_EXPERTS, per_expert, 0)

    @pl.when(i >= 2)
    def _():
        wait_rows(slot)

    cp = pltpu.make_async_copy(pos_hbm.at[i], pos_smem.at[slot], pos_sem)
    cp.start()
    cp.wait()
    base = i * tm
    for k in range(TOP_K):
        def issue(r, carry, k=k):
            pltpu.make_async_copy(x_hbm.at[pl.ds(base + r, 1)],
                                  xs_hbm.at[pl.ds(pos_smem[slot, k, r], 1)],
                                  row_sem.at[slot]).start()
            return carry
        lax.fori_loop(0, tm, issue, 0, unroll=8)

    @pl.when(i == steps - 1)
    def _():
        wait_rows(slot)

        @pl.when(steps > 1)
        def _():
            wait_rows(1 - slot)


def _dispatch(pos, x, pad_from, pad_to, used, total_rows):
    n, d = x.shape
    tm = DISPATCH_ROWS
    grid_spec = pltpu.PrefetchScalarGridSpec(
        num_scalar_prefetch=3,
        grid=(n // tm,),
        in_specs=[pl.BlockSpec(memory_space=pl.ANY),
                  pl.BlockSpec(memory_space=pl.ANY)],
        out_specs=pl.BlockSpec(memory_space=pl.ANY),
        scratch_shapes=[pltpu.SMEM((2, TOP_K, tm), jnp.int32),
                        pltpu.VMEM((MOE_ROWS, d), x.dtype),
                        pltpu.SemaphoreType.DMA((2,)),
                        pltpu.SemaphoreType.DMA,
                        pltpu.SemaphoreType.DMA],
    )
    return pl.pallas_call(
        _dispatch_kernel,
        grid_spec=grid_spec,
        out_shape=jax.ShapeDtypeStruct((total_rows, d), x.dtype),
        compiler_params=_params(("arbitrary",)),
        name="dispatch",
    )(pad_from, pad_to, used, pos.reshape(TOP_K, n // tm, tm).transpose(1, 0, 2), x)


def _experts_kernel(be_ref, used_ref, x_ref, wg_ref, wu_ref, wd_ref, y_ref,
                    wg_bf, wu_bf, wd_bf):
    i = pl.program_id(0)
    used = used_ref[0]

    @pl.when(i < used)
    def _():
        changed = jnp.logical_or(i == 0, be_ref[jnp.maximum(i - 1, 0)] != be_ref[i])

        @pl.when(changed)
        def _():
            wg_bf[...] = wg_ref[...].astype(BF16)
            wu_bf[...] = wu_ref[...].astype(BF16)
            wd_bf[...] = wd_ref[...].astype(BF16)

        x = x_ref[...].astype(BF16)
        hid = jax.nn.silu(_dot(x, wg_bf[...])) * _dot(x, wu_bf[...])
        y_ref[...] = _dot(hid.astype(BF16), wd_bf[...])

    @pl.when(i >= used)
    def _():
        y_ref[...] = jnp.zeros_like(y_ref)


def _experts(block_expert, used, xs, w_gate, w_up, w_down, layer):
    nb = block_expert.shape[0]
    rows = MOE_ROWS
    d = xs.shape[1]
    de = w_gate.shape[3]
    grid_spec = pltpu.PrefetchScalarGridSpec(
        num_scalar_prefetch=2,
        grid=(nb,),
        in_specs=[
            pl.BlockSpec((rows, d), lambda i, be, u: (jnp.minimum(i, u[0] - 1), 0)),
            pl.BlockSpec((None, None, d, de), lambda i, be, u: (layer, be[i], 0, 0)),
            pl.BlockSpec((None, None, d, de), lambda i, be, u: (layer, be[i], 0, 0)),
            pl.BlockSpec((None, None, de, d), lambda i, be, u: (layer, be[i], 0, 0)),
        ],
        out_specs=pl.BlockSpec((rows, d), lambda i, be, u: (i, 0)),
        scratch_shapes=[
            pltpu.VMEM((d, de), BF16),
            pltpu.VMEM((d, de), BF16),
            pltpu.VMEM((de, d), BF16),
        ],
    )
    return pl.pallas_call(
        _experts_kernel,
        grid_spec=grid_spec,
        out_shape=jax.ShapeDtypeStruct((nb * rows, d), F32),
        compiler_params=_params(("arbitrary",)),
        name="experts",
    )(block_expert, used, xs, w_gate, w_up, w_down)


def _combine_kernel(pos_hbm, y_hbm, gate_ref, h_ref, wsg_ref, wsu_ref, wsd_ref, g_ref, b_ref,
                    o_ref, obf_ref, ybuf, pos_smem, row_sem, pos_sem):
    i = pl.program_id(0)
    steps = pl.num_programs(0)
    tm = h_ref.shape[0]
    slot = i % 2

    def start_gather(block, s):
        cp = pltpu.make_async_copy(pos_hbm.at[block], pos_smem.at[s], pos_sem.at[s])
        cp.start()
        cp.wait()
        for k in range(TOP_K):
            def issue(r, carry, k=k):
                pltpu.make_async_copy(y_hbm.at[pl.ds(pos_smem[s, k, r], 1)],
                                      ybuf.at[s, k, pl.ds(r, 1)], row_sem.at[s]).start()
                return carry
            lax.fori_loop(0, tm, issue, 0, unroll=8)

    @pl.when(i == 0)
    def _():
        start_gather(0, 0)

    @pl.when(i + 1 < steps)
    def _():
        start_gather(i + 1, 1 - slot)

    h = h_ref[...]
    x = h.astype(BF16)
    hid = jax.nn.silu(_dot(x, wsg_ref[...])) * _dot(x, wsu_ref[...])
    y = ALPHA * h + _dot(hid.astype(BF16), wsd_ref[...])

    for k in range(TOP_K):
        pltpu.make_async_copy(y_hbm.at[pl.ds(0, tm)], ybuf.at[slot, k], row_sem.at[slot]).wait()
    gates = gate_ref[...]
    for k in range(TOP_K):
        y = y + ybuf[slot, k] * gates[:, k:k + 1]
    out = _layer_norm(y, g_ref[...], b_ref[...])
    o_ref[...] = out
    obf_ref[...] = out.astype(BF16)


def _combine(pos, y_rows, gates, h, ws_gate, ws_up, ws_down, ln_g, ln_b, layer):
    n, d = h.shape
    tm = COMBINE_ROWS
    ds = ws_gate.shape[2]
    return pl.pallas_call(
        _combine_kernel,
        grid=(n // tm,),
        in_specs=[pl.BlockSpec(memory_space=pl.ANY),
                  pl.BlockSpec(memory_space=pl.ANY),
                  pl.BlockSpec((tm, TOP_K), lambda i: (i, 0)),
                  pl.BlockSpec((tm, d), lambda i: (i, 0)),
                  _layer_spec((d, ds), layer),
                  _layer_spec((d, ds), layer),
                  _layer_spec((ds, d), layer),
                  _layer_spec((1, d), layer),
                  _layer_spec((1, d), layer)],
        out_specs=[pl.BlockSpec((tm, d), lambda i: (i, 0)),
                   pl.BlockSpec((tm, d), lambda i: (i, 0))],
        out_shape=[jax.ShapeDtypeStruct((n, d), F32), jax.ShapeDtypeStruct((n, d), BF16)],
        scratch_shapes=[pltpu.VMEM((2, TOP_K, tm, d), F32),
                        pltpu.SMEM((2, TOP_K, tm), jnp.int32),
                        pltpu.SemaphoreType.DMA((2,)),
                        pltpu.SemaphoreType.DMA((2,))],
        compiler_params=_params(("arbitrary",)),
        name="combine",
    )(pos.reshape(TOP_K, n // tm, tm).transpose(1, 0, 2), y_rows, gates, h,
      ws_gate, ws_up, ws_down, ln_g, ln_b)


def _alibi_slopes(n):
    return 2.0 ** (-8.0 * jnp.arange(1, n + 1, dtype=F32) / n)


def kernel(x, mem, w_in, a_sinks, g_a, g_b, w_out, ln1_g, ln1_b, wq_m, wk_m, wv_m, wo_m,
           ln2_g, ln2_b, w_router, router_bias, w_gate, w_up, w_down, ws_gate, ws_up, ws_down,
           ln3_g, ln3_b):
    b, length, d = x.shape
    n = b * length
    mlen = mem.shape[1]
    depth = w_in.shape[0]
    h = x.reshape(n, d)
    h_bf = h.astype(BF16)
    mem_bf = mem.reshape(b * mlen, d).astype(BF16)
    slopes_a = _alibi_slopes(A_Q_HEADS)
    slopes_b = _alibi_slopes(B_HEADS)

    w_in_bf = w_in.astype(BF16)
    w_in_a = w_in_bf[:, :, :A_PROJ_WIDTH]
    w_in_b = w_in_bf[:, :, A_PROJ_WIDTH:]
    w_out_bf = w_out.astype(BF16)
    wq_bf, wk_bf, wv_bf, wo_bf = (w.astype(BF16) for w in (wq_m, wk_m, wv_m, wo_m))
    wsg_bf, wsu_bf, wsd_bf = (w.astype(BF16) for w in (ws_gate, ws_up, ws_down))
    row = lambda p: p.reshape(depth, 1, p.shape[1])
    g_b3, ln1_g3, ln1_b3, ln2_g3, ln2_b3, ln3_g3, ln3_b3 = (
        row(p) for p in (g_b, ln1_g, ln1_b, ln2_g, ln2_b, ln3_g, ln3_b))

    for l in range(depth):
        proj_a = _matmul(h_bf, w_in_a, l, 512, A_PROJ_WIDTH, BF16)
        proj_b = _matmul(h_bf, w_in_b, l, 512, B_WIDTH, F32)
        mixed_a = _attn_a(proj_a.reshape(b, length, A_PROJ_WIDTH), slopes_a, a_sinks[l], g_a[l])
        out_b = _attn_b(proj_b.reshape(b, length, 3 * B_WIDTH), slopes_b)
        h, h_bf = _mix_out(mixed_a.reshape(n, A_WIDTH), out_b.reshape(n, B_WIDTH), g_b3,
                           w_out_bf, h, ln1_g3, ln1_b3, l)

        km = _matmul(mem_bf, wk_bf, l, b * mlen, d // 2, BF16)
        vm = _matmul(mem_bf, wv_bf, l, b * mlen, d // 2, BF16)
        o = _mem_attn(h_bf.reshape(b, length, d), wq_bf,
                      km.reshape(b, mlen, d), vm.reshape(b, mlen, d), l)
        h, h_bf = _proj_out(o.reshape(n, d), wo_bf, h, ln2_g3, ln2_b3, l)

        route_tm = 512
        idx_t, gate_t, mask_t, cnt = _router(h, w_router[l].T, router_bias[l], route_tm)
        tile_offsets, block_expert, used, pad_from, pad_to = _block_layout(cnt[:, :, 0], n)
        pos = _plan(mask_t, idx_t, tile_offsets, route_tm)
        xs = _dispatch(pos, h, pad_from, pad_to, used, block_expert.shape[0] * MOE_ROWS)
        y_rows = _experts(block_expert, used, xs, w_gate, w_up, w_down, l)
        h, h_bf = _combine(pos, y_rows, gate_t.T, h, wsg_bf, wsu_bf, wsd_bf, ln3_g3, ln3_b3, l)
    return h.reshape(b, length, d)
```

```python
import math

import jax
import jax.numpy as jnp
from jax import lax
from jax.experimental import pallas as pl
from jax.experimental.pallas import tpu as pltpu

F32 = jnp.float32
BF16 = jnp.bfloat16

HEAD_DIM = 64
A_Q_HEADS = 16
A_KV_HEADS = 2
A_WINDOW = 128
B_HEADS = 16
B_BRANCH_DILATIONS = (1, 4, 16)
A_WIDTH = A_Q_HEADS * HEAD_DIM
A_KV_WIDTH = A_KV_HEADS * HEAD_DIM
B_WIDTH = B_HEADS * HEAD_DIM
A_PROJ_WIDTH = A_WIDTH + 2 * A_KV_WIDTH
BLK = 128

MEM_HEADS = 4

N_EXPERTS = 64
TOP_K = 8
N_GROUPS = 8
GROUP_SIZE = N_EXPERTS // N_GROUPS
TOPK_GROUPS = 4
ROUTE_SCALE = 2.5

DEPTH = 2
ALPHA = (2.0 * DEPTH) ** 0.25
LN_EPS = 1e-5
RMS_EPS = 1e-6

LANES = 128
NEG = -1e30

MOE_ROWS = 256
COMBINE_ROWS = 128
VMEM_LIMIT = 56 * 1024 * 1024


def _params(sem, vmem=VMEM_LIMIT):
    return pltpu.CompilerParams(dimension_semantics=sem, vmem_limit_bytes=vmem)


def _dot(a, b):
    return jnp.dot(a, b, preferred_element_type=F32)


def _dot_nt(a, b):
    return lax.dot_general(a, b, (((1,), (1,)), ((), ())), preferred_element_type=F32)


def _layer_norm(y, g, b):
    mu = jnp.mean(y, axis=-1, keepdims=True)
    yc = y - mu
    var = jnp.mean(yc * yc, axis=-1, keepdims=True)
    return yc * lax.rsqrt(var + LN_EPS) * g + b


def _layer_spec(shape, layer, index=None, single=True):
    index = index or (0,) * len(shape)
    mode = dict(pipeline_mode=pl.Buffered(1)) if single else {}
    return pl.BlockSpec((None,) + tuple(shape), lambda *_: (layer,) + tuple(index), **mode)


def _mm_kernel(x_ref, w_ref, o_ref):
    o_ref[...] = _dot(x_ref[...], w_ref[...]).astype(o_ref.dtype)


def _matmul(x, w, layer, tm, tn, out_dtype):
    m, k = x.shape
    n = w.shape[2]
    return pl.pallas_call(
        _mm_kernel,
        grid=(n // tn, m // tm),
        in_specs=[pl.BlockSpec((tm, k), lambda j, i: (i, 0)),
                  pl.BlockSpec((None, k, tn), lambda j, i: (layer, 0, j))],
        out_specs=pl.BlockSpec((tm, tn), lambda j, i: (i, j)),
        out_shape=jax.ShapeDtypeStruct((m, n), out_dtype),
        compiler_params=_params(("arbitrary", "arbitrary")),
        name="matmul",
    )(x, w)


def _band_bias(stacked_heads, max_offset, slope_of_stack):
    rows = stacked_heads * BLK
    row = lax.broadcasted_iota(jnp.int32, (rows, 2 * BLK), 0)
    col = lax.broadcasted_iota(jnp.int32, (rows, 2 * BLK), 1)
    offset = (row & (BLK - 1)) + BLK - col
    slope = jnp.full((rows, 2 * BLK), slope_of_stack[stacked_heads - 1], F32)
    for a in range(stacked_heads - 2, -1, -1):
        slope = jnp.where(row < (a + 1) * BLK, slope_of_stack[a], slope)
    valid = jnp.logical_and(offset >= 0, offset <= max_offset)
    bias = jnp.where(valid, -slope * offset.astype(F32), NEG)
    return bias, jnp.where(col < BLK, NEG, bias)


def _roll_half(t):
    return pltpu.roll(t.astype(F32), LANES // 2, axis=1).astype(BF16)


A_STACK = 4


def _attn_a_kernel(slopes_ref, sinks_ref, q_ref, kc_ref, kp_ref, vc_ref, vp_ref, g_ref,
                   o_ref, o_scr, bias_scr):
    n = pl.program_id(1)
    tiles_per_kv = A_Q_HEADS // A_KV_HEADS // 2

    def head_of(kv, half, a):
        return 2 * (kv * tiles_per_kv + a) + half

    @pl.when(jnp.logical_and(pl.program_id(0) == 0, n == 0))
    def _():
        for kv in range(A_KV_HEADS):
            for half in range(2):
                slopes = [slopes_ref[head_of(kv, half, a)] for a in range(A_STACK)]
                bias, bias_first = _band_bias(A_STACK, A_WINDOW - 1, slopes)
                bias_scr[0, 2 * kv + half] = bias
                bias_scr[1, 2 * kv + half] = bias_first

    first = (n == 0).astype(jnp.int32)
    lane = lax.broadcasted_iota(jnp.int32, (BLK, LANES), 1)
    lo = lane < HEAD_DIM
    srow = lax.broadcasted_iota(jnp.int32, (A_STACK * BLK, 1), 0)

    k_ver = (jnp.concatenate([kp_ref[0], kc_ref[0]], axis=0),
             jnp.concatenate([_roll_half(kp_ref[0]), _roll_half(kc_ref[0])], axis=0))
    v_ver = (jnp.concatenate([vp_ref[0], vc_ref[0]], axis=0),
             jnp.concatenate([_roll_half(vp_ref[0]), _roll_half(vc_ref[0])], axis=0))

    ss = jnp.zeros((BLK, 1), F32)
    for kv in range(A_KV_HEADS):
        outs = []
        for half in range(2):
            keep = lo if half == 0 else jnp.logical_not(lo)
            parts = []
            for a in range(A_STACK):
                t = kv * tiles_per_kv + a
                qt = q_ref[0, :, t * LANES:(t + 1) * LANES] * jnp.asarray(0.125, BF16)
                parts.append(jnp.where(keep, qt, jnp.zeros_like(qt)))
            qs = jnp.concatenate(parts, axis=0)
            ver = kv ^ half
            logits = _dot_nt(qs, k_ver[ver]) + bias_scr[first, 2 * kv + half]
            sink = jnp.full((A_STACK * BLK, 1), sinks_ref[head_of(kv, half, A_STACK - 1)], F32)
            for a in range(A_STACK - 2, -1, -1):
                sink = jnp.where(srow < (a + 1) * BLK, sinks_ref[head_of(kv, half, a)], sink)
            m = jnp.maximum(jnp.max(logits, axis=1, keepdims=True), sink)
            e = jnp.exp(logits - m)
            s = jnp.sum(e, axis=1, keepdims=True) + jnp.exp(sink - m)
            acc = _dot(e.astype(BF16), v_ver[ver])
            outs.append(acc * (1.0 / s))
        for a in range(A_STACK):
            t = kv * tiles_per_kv + a
            o_tile = jnp.where(lo, outs[0][a * BLK:(a + 1) * BLK], outs[1][a * BLK:(a + 1) * BLK])
            ss = ss + jnp.sum(o_tile * o_tile, axis=1, keepdims=True)
            o_scr[:, t * LANES:(t + 1) * LANES] = o_tile
    scale = lax.rsqrt(ss * (1.0 / A_WIDTH) + RMS_EPS)
    o_ref[0] = (o_scr[...] * scale * g_ref[...]).astype(o_ref.dtype)


def _attn_a(proj_a, slopes, sinks, g_a):
    b, length, _ = proj_a.shape
    kcol = A_WIDTH // LANES
    vcol = kcol + 1
    grid_spec = pltpu.PrefetchScalarGridSpec(
        num_scalar_prefetch=2,
        grid=(b, length // BLK),
        in_specs=[
            pl.BlockSpec((1, BLK, A_WIDTH), lambda i, n, *_: (i, n, 0)),
            pl.BlockSpec((1, BLK, LANES), lambda i, n, *_: (i, n, kcol)),
            pl.BlockSpec((1, BLK, LANES), lambda i, n, *_: (i, jnp.maximum(n - 1, 0), kcol)),
            pl.BlockSpec((1, BLK, LANES), lambda i, n, *_: (i, n, vcol)),
            pl.BlockSpec((1, BLK, LANES), lambda i, n, *_: (i, jnp.maximum(n - 1, 0), vcol)),
            pl.BlockSpec((1, A_WIDTH), lambda i, n, *_: (0, 0)),
        ],
        out_specs=pl.BlockSpec((1, BLK, A_WIDTH), lambda i, n, *_: (i, n, 0)),
        scratch_shapes=[pltpu.VMEM((BLK, A_WIDTH), F32),
                        pltpu.VMEM((2, 2 * A_KV_HEADS, A_STACK * BLK, 2 * BLK), F32)],
    )
    return pl.pallas_call(
        _attn_a_kernel,
        grid_spec=grid_spec,
        out_shape=jax.ShapeDtypeStruct((b, length, A_WIDTH), BF16),
        compiler_params=_params(("arbitrary", "arbitrary")),
        name="attn_a",
    )(slopes, sinks, proj_a, proj_a, proj_a, proj_a, proj_a, g_a.reshape(1, A_WIDTH))


def _attn_b_kernel(slopes_ref, q_ref, k_ref, v_ref, o_ref, acc_scr, m_scr, s_scr, bias_scr):
    hp = pl.program_id(1)
    length = q_ref.shape[1]
    lane = lax.broadcasted_iota(jnp.int32, (BLK, LANES), 1)
    lo = lane < HEAD_DIM
    qrow = lax.broadcasted_iota(jnp.int32, (2 * BLK, LANES), 0)
    qlane = lax.broadcasted_iota(jnp.int32, (2 * BLK, LANES), 1)
    qmask = (qrow >= BLK) == (qlane >= HEAD_DIM)

    for branch, dil in enumerate(B_BRANCH_DILATIONS):
        slopes = [slopes_ref[2 * hp] * float(dil), slopes_ref[2 * hp + 1] * float(dil)]
        bias, bias_first = _band_bias(2, BLK, slopes)
        bias_scr[branch, 0] = bias
        bias_scr[branch, 1] = bias_first

    for branch, dil in enumerate(B_BRANCH_DILATIONS):
        blocks_per_class = length // (dil * BLK)

        def rows(start, dil=dil):
            if dil == 1:
                return pl.ds(start, BLK)
            return pl.ds(start, BLK, stride=dil)

        def body(it, carry, branch=branch, dil=dil, blocks_per_class=blocks_per_class, rows=rows):
            res = it // blocks_per_class
            blk = it % blocks_per_class
            cur = rows(res + dil * BLK * blk)
            prev = rows(res + dil * BLK * jnp.maximum(blk - 1, 0))
            first = jnp.where(blk == 0, 1, 0)
            q2 = q_ref[0, cur, :] * 0.125
            qs = jnp.concatenate([q2, q2], axis=0)
            qs = jnp.where(qmask, qs, 0.0).astype(BF16)
            k2 = jnp.concatenate([k_ref[0, prev, :], k_ref[0, cur, :]], axis=0).astype(BF16)
            v2 = jnp.concatenate([v_ref[0, prev, :], v_ref[0, cur, :]], axis=0).astype(BF16)
            logits = _dot_nt(qs, k2) + bias_scr[branch, first]
            m = jnp.max(logits, axis=1, keepdims=True)
            e = jnp.exp(logits - m)
            s = jnp.sum(e, axis=1, keepdims=True)
            acc = _dot(e.astype(BF16), v2)
            m_t = jnp.where(lo, m[:BLK], m[BLK:])
            s_t = jnp.where(lo, s[:BLK], s[BLK:])
            acc_t = jnp.where(lo, acc[:BLK], acc[BLK:])
            if branch == 0:
                m_scr[cur, :] = m_t
                s_scr[cur, :] = s_t
                acc_scr[cur, :] = acc_t
            else:
                m_old = m_scr[cur, :]
                s_old = s_scr[cur, :]
                acc_old = acc_scr[cur, :]
                m_new = jnp.maximum(m_old, m_t)
                w_old = jnp.exp(m_old - m_new)
                w_blk = jnp.exp(m_t - m_new)
                m_scr[cur, :] = m_new
                s_scr[cur, :] = s_old * w_old + s_t * w_blk
                acc_scr[cur, :] = acc_old * w_old + acc_t * w_blk
            return carry

        lax.fori_loop(0, length // BLK, body, 0, unroll=2)

    o_ref[0] = acc_scr[...] / s_scr[...]


def _attn_b(proj_b, slopes):
    b, length, _ = proj_b.shape
    pairs = B_WIDTH // LANES
    grid_spec = pltpu.PrefetchScalarGridSpec(
        num_scalar_prefetch=1,
        grid=(b, pairs),
        in_specs=[
            pl.BlockSpec((1, length, LANES), lambda i, p, *_: (i, 0, p)),
            pl.BlockSpec((1, length, LANES), lambda i, p, *_: (i, 0, pairs + p)),
            pl.BlockSpec((1, length, LANES), lambda i, p, *_: (i, 0, 2 * pairs + p)),
        ],
        out_specs=pl.BlockSpec((1, length, LANES), lambda i, p, *_: (i, 0, p)),
        scratch_shapes=[pltpu.VMEM((length, LANES), F32)] * 3
        + [pltpu.VMEM((len(B_BRANCH_DILATIONS), 2, 2 * BLK, 2 * BLK), F32)],
    )
    return pl.pallas_call(
        _attn_b_kernel,
        grid_spec=grid_spec,
        out_shape=jax.ShapeDtypeStruct((b, length, B_WIDTH), F32),
        compiler_params=_params(("arbitrary", "arbitrary")),
        name="attn_b",
    )(slopes, proj_b, proj_b, proj_b)


def _mix_out_kernel(xa_ref, xb_ref, gb_ref, wa_ref, wb_ref, h_ref, g_ref, b_ref,
                    o_ref, obf_ref):
    xb = xb_ref[...]
    scale = lax.rsqrt(jnp.mean(xb * xb, axis=-1, keepdims=True) + RMS_EPS)
    xb = (xb * scale * gb_ref[...]).astype(BF16)
    y = _dot(xa_ref[...], wa_ref[...]) + _dot(xb, wb_ref[...])
    out = _layer_norm(ALPHA * h_ref[...] + y, g_ref[...], b_ref[...])
    o_ref[...] = out
    obf_ref[...] = out.astype(BF16)


def _proj_out_kernel(x_ref, w_ref, h_ref, g_ref, b_ref, o_ref, obf_ref):
    y = _dot(x_ref[...], w_ref[...])
    out = _layer_norm(ALPHA * h_ref[...] + y, g_ref[...], b_ref[...])
    o_ref[...] = out
    obf_ref[...] = out.astype(BF16)


def _mix_out(xa, xb, g_b, w_out, h, ln_g, ln_b, layer, tm=512):
    n, d = h.shape
    ka = xa.shape[1]
    kb = xb.shape[1]
    return pl.pallas_call(
        _mix_out_kernel,
        grid=(n // tm,),
        in_specs=[pl.BlockSpec((tm, ka), lambda i: (i, 0)),
                  pl.BlockSpec((tm, kb), lambda i: (i, 0)),
                  _layer_spec((1, kb), layer),
                  _layer_spec((ka, d), layer, (0, 0)),
                  _layer_spec((kb, d), layer, (ka // kb, 0)),
                  pl.BlockSpec((tm, d), lambda i: (i, 0)),
                  _layer_spec((1, d), layer),
                  _layer_spec((1, d), layer)],
        out_specs=[pl.BlockSpec((tm, d), lambda i: (i, 0)),
                   pl.BlockSpec((tm, d), lambda i: (i, 0))],
        out_shape=[jax.ShapeDtypeStruct((n, d), F32), jax.ShapeDtypeStruct((n, d), BF16)],
        compiler_params=_params(("arbitrary",)),
        name="mix_out",
    )(xa, xb, g_b, w_out, w_out, h, ln_g, ln_b)


def _proj_out(x, w, h, ln_g, ln_b, layer, tm=512):
    n, d = h.shape
    k = x.shape[1]
    return pl.pallas_call(
        _proj_out_kernel,
        grid=(n // tm,),
        in_specs=[pl.BlockSpec((tm, k), lambda i: (i, 0)),
                  _layer_spec((k, d), layer),
                  pl.BlockSpec((tm, d), lambda i: (i, 0)),
                  _layer_spec((1, d), layer),
                  _layer_spec((1, d), layer)],
        out_specs=[pl.BlockSpec((tm, d), lambda i: (i, 0)),
                   pl.BlockSpec((tm, d), lambda i: (i, 0))],
        out_shape=[jax.ShapeDtypeStruct((n, d), F32), jax.ShapeDtypeStruct((n, d), BF16)],
        compiler_params=_params(("arbitrary",)),
        name="proj_out",
    )(x, w, h, ln_g, ln_b)


def _mem_attn_kernel(x_ref, wq_ref, k_ref, v_ref, o_ref):
    d = x_ref.shape[2]
    hd = d // MEM_HEADS
    q = _dot(x_ref[0], wq_ref[...]) * (1.0 / math.sqrt(hd))
    for head in range(MEM_HEADS):
        cols = slice(head * hd, (head + 1) * hd)
        logits = _dot_nt(q[:, cols].astype(BF16), k_ref[0, :, cols])
        m = jnp.max(logits, axis=1, keepdims=True)
        e = jnp.exp(logits - m)
        s = jnp.sum(e, axis=1, keepdims=True)
        o = _dot(e.astype(BF16), v_ref[0, :, cols]) / s
        o_ref[0, :, cols] = o.astype(o_ref.dtype)


def _mem_attn(x, wq, km, vm, layer, tm=512):
    b, length, d = x.shape
    mlen = km.shape[1]
    return pl.pallas_call(
        _mem_attn_kernel,
        grid=(b, length // tm),
        in_specs=[pl.BlockSpec((1, tm, d), lambda i, j: (i, j, 0)),
                  _layer_spec((d, d), layer),
                  pl.BlockSpec((1, mlen, d), lambda i, j: (i, 0, 0)),
                  pl.BlockSpec((1, mlen, d), lambda i, j: (i, 0, 0))],
        out_specs=pl.BlockSpec((1, tm, d), lambda i, j: (i, j, 0)),
        out_shape=jax.ShapeDtypeStruct((b, length, d), BF16),
        compiler_params=_params(("arbitrary", "arbitrary")),
        name="mem_attn",
    )(x, wq, km, vm)


def _split_bf16(a):
    hi = a.astype(BF16)
    lo = (a - hi.astype(F32)).astype(BF16)
    return hi, lo


def _router_kernel(x_ref, wt_ref, bias_ref, idx_ref, gate_ref, mask_ref, cnt_ref):
    tm = x_ref.shape[0]
    xh, xl = _split_bf16(x_ref[...])
    wh, wl = _split_bf16(wt_ref[...])
    logits = _dot_nt(wh, xh) + (_dot_nt(wh, xl) + _dot_nt(wl, xh))
    scores = jax.nn.sigmoid(logits)
    sel = scores + bias_ref[...]

    member = lax.broadcasted_iota(jnp.int32, (GROUP_SIZE, tm), 0)
    group_scores = []
    for g in range(N_GROUPS):
        v = sel[g * GROUP_SIZE:(g + 1) * GROUP_SIZE, :]
        m1 = jnp.max(v, axis=0, keepdims=True)
        first = jnp.min(jnp.where(v == m1, member, GROUP_SIZE), axis=0, keepdims=True)
        m2 = jnp.max(jnp.where(member == first, -jnp.inf, v), axis=0, keepdims=True)
        group_scores.append(m1 + m2)
    masked = []
    for g in range(N_GROUPS):
        beaten_by = jnp.zeros((1, tm), jnp.int32)
        for o in range(N_GROUPS):
            if o == g:
                continue
            wins = group_scores[o] > group_scores[g]
            if o < g:
                wins = jnp.logical_or(wins, group_scores[o] == group_scores[g])
            beaten_by = beaten_by + wins.astype(jnp.int32)
        keep = beaten_by < TOPK_GROUPS
        masked.append(jnp.where(keep, sel[g * GROUP_SIZE:(g + 1) * GROUP_SIZE, :], -jnp.inf))
    cand = jnp.concatenate(masked, axis=0)

    expert = lax.broadcasted_iota(jnp.int32, (N_EXPERTS, tm), 0)
    beaten_by = jnp.zeros((N_EXPERTS, tm), jnp.int32)
    for o in range(N_EXPERTS):
        other = cand[o:o + 1, :]
        wins = jnp.logical_or(other > cand, jnp.logical_and(other == cand, expert > o))
        beaten_by = beaten_by + wins.astype(jnp.int32)
    chosen = beaten_by < TOP_K
    chosen_i = chosen.astype(jnp.int32)
    mask_ref[...] = chosen_i
    cnt_ref[...] = jnp.broadcast_to(jnp.sum(chosen_i, axis=1, keepdims=True), cnt_ref.shape)

    picked = jnp.where(chosen, scores, 0.0)
    gate = picked / jnp.sum(picked, axis=0, keepdims=True) * ROUTE_SCALE

    running = jnp.zeros((1, tm), jnp.int32)
    slots = []
    for g in range(N_GROUPS):
        part = chosen_i[g * GROUP_SIZE:(g + 1) * GROUP_SIZE, :]
        rows_ = []
        for r in range(GROUP_SIZE):
            rows_.append(running)
            running = running + part[r:r + 1, :]
        slots.append(jnp.concatenate(rows_, axis=0))
    slot = jnp.concatenate(slots, axis=0)
    idx_rows = []
    gate_rows = []
    for k in range(TOP_K):
        hit = jnp.logical_and(chosen, slot == k)
        idx_rows.append(jnp.sum(jnp.where(hit, expert, 0), axis=0, keepdims=True))
        gate_rows.append(jnp.sum(jnp.where(hit, gate, 0.0), axis=0, keepdims=True))
    idx_ref[...] = jnp.concatenate(idx_rows, axis=0)
    gate_ref[...] = jnp.concatenate(gate_rows, axis=0)


def _router(x, w_router_t, bias, tm=512):
    n, d = x.shape
    return pl.pallas_call(
        _router_kernel,
        grid=(n // tm,),
        in_specs=[pl.BlockSpec((tm, d), lambda i: (i, 0)),
                  pl.BlockSpec((N_EXPERTS, d), lambda i: (0, 0)),
                  pl.BlockSpec((N_EXPERTS, 1), lambda i: (0, 0))],
        out_specs=[pl.BlockSpec((TOP_K, tm), lambda i: (0, i)),
                   pl.BlockSpec((TOP_K, tm), lambda i: (0, i)),
                   pl.BlockSpec((N_EXPERTS, tm), lambda i: (0, i)),
                   pl.BlockSpec((None, N_EXPERTS, LANES), lambda i: (i, 0, 0))],
        out_shape=[jax.ShapeDtypeStruct((TOP_K, n), jnp.int32),
                   jax.ShapeDtypeStruct((TOP_K, n), F32),
                   jax.ShapeDtypeStruct((N_EXPERTS, n), jnp.int32),
                   jax.ShapeDtypeStruct((n // tm, N_EXPERTS, LANES), jnp.int32)],
        compiler_params=_params(("arbitrary",)),
        name="router",
    )(x, w_router_t, bias.reshape(N_EXPERTS, 1))


def _plan_kernel(mask_ref, idx_ref, off_ref, pos_ref):
    tm = mask_ref.shape[1]
    before = (lax.broadcasted_iota(jnp.int32, (tm, tm), 0)
              < lax.broadcasted_iota(jnp.int32, (tm, tm), 1)).astype(BF16)
    rank = _dot(mask_ref[...].astype(BF16), before).astype(jnp.int32)
    row_of = off_ref[...] + rank
    expert = lax.broadcasted_iota(jnp.int32, (N_EXPERTS, tm), 0)
    rows_ = []
    for k in range(TOP_K):
        hit = expert == idx_ref[k:k + 1, :]
        rows_.append(jnp.sum(jnp.where(hit, row_of, 0), axis=0, keepdims=True))
    pos_ref[...] = jnp.concatenate(rows_, axis=0)


def _plan(mask_t, idx_t, tile_offsets, tm):
    n = mask_t.shape[1]
    return pl.pallas_call(
        _plan_kernel,
        grid=(n // tm,),
        in_specs=[pl.BlockSpec((N_EXPERTS, tm), lambda i: (0, i)),
                  pl.BlockSpec((TOP_K, tm), lambda i: (0, i)),
                  pl.BlockSpec((None, N_EXPERTS, 1), lambda i: (i, 0, 0))],
        out_specs=pl.BlockSpec((TOP_K, tm), lambda i: (0, i)),
        out_shape=jax.ShapeDtypeStruct((TOP_K, n), jnp.int32),
        compiler_params=_params(("arbitrary",)),
        name="plan",
    )(mask_t, idx_t, tile_offsets)


def _block_layout(counts_per_tile, n):
    rows = MOE_ROWS
    nb = (n * TOP_K) // rows + N_EXPERTS
    counts = jnp.sum(counts_per_tile, axis=0)
    padded = (counts + rows - 1) // rows * rows
    padded_ends = jnp.cumsum(padded)
    padded_starts = padded_ends - padded
    tile_offsets = padded_starts[None, :] + jnp.cumsum(counts_per_tile, axis=0) - counts_per_tile
    used = (padded_ends[-1] // rows).astype(jnp.int32)
    block_start = jnp.arange(nb, dtype=jnp.int32) * rows
    block_expert = jnp.minimum(
        jnp.sum((padded_ends[None, :] <= block_start[:, None]).astype(jnp.int32), axis=1),
        N_EXPERTS - 1)
    last = block_expert[jnp.maximum(used - 1, 0)]
    block_expert = jnp.where(jnp.arange(nb) < used, block_expert, last)
    pad_from = (padded_starts + counts).astype(jnp.int32)
    return (tile_offsets.astype(jnp.int32)[:, :, None], block_expert, used.reshape(1),
            pad_from, padded_ends.astype(jnp.int32))


DISPATCH_ROWS = 128


def _dispatch_kernel(pad_from_ref, pad_to_ref, used_ref, pos_hbm, x_hbm, xs_hbm, pos_smem,
                     xbuf, zeros, row_sem, tile_sem, pos_sem, pad_sem):
    i = pl.program_id(0)
    steps = pl.num_programs(0)
    tm = pos_smem.shape[2]
    slot = i % 3

    def wait_rows(s):
        for _ in range(TOP_K):
            pltpu.make_async_copy(xbuf.at[s], xs_hbm.at[pl.ds(0, tm)], row_sem.at[s]).wait()

    def tile_copy(blk, s):
        start = pl.multiple_of(blk * tm, tm)
        return pltpu.make_async_copy(x_hbm.at[pl.ds(start, tm)], xbuf.at[s], tile_sem.at[s])

    def pos_copy(blk):
        return pltpu.make_async_copy(pos_hbm.at[blk], pos_smem.at[blk % 2], pos_sem.at[blk % 2])

    @pl.when(i == 0)
    def _():
        tile_copy(0, 0).start()
        pos_copy(0).start()
        zeros[...] = jnp.zeros_like(zeros)
        block = zeros.shape[0]

        def tail_copy(blk):
            start = pl.multiple_of(blk * block, block)
            return pltpu.make_async_copy(zeros, xs_hbm.at[pl.ds(start, block)], pad_sem)

        def tail_start(blk, c):
            tail_copy(blk).start()
            return c

        def tail_wait(blk, c):
            tail_copy(blk).wait()
            return c

        lax.fori_loop(used_ref[0], xs_hbm.shape[0] // block, tail_start, 0)
        lax.fori_loop(used_ref[0], xs_hbm.shape[0] // block, tail_wait, 0)

        def per_expert(e, carry):
            def pad_copy(r):
                return pltpu.make_async_copy(zeros.at[pl.ds(0, 1)], xs_hbm.at[pl.ds(r, 1)],
                                             pad_sem)

            def start(r, c):
                pad_copy(r).start()
                return c

            def wait(r, c):
                pad_copy(r).wait()
                return c

            lax.fori_loop(pad_from_ref[e], pad_to_ref[e], start, 0)
            lax.fori_loop(pad_from_ref[e], pad_to_ref[e], wait, 0)
            return carry

        lax.fori_loop(0, N_EXPERTS, per_expert, 0)

    nxt = (i + 1) % 3

    @pl.when(i >= 2)
    def _():
        wait_rows(nxt)

    @pl.when(i + 1 < steps)
    def _():
        tile_copy(i + 1, nxt).start()
        pos_copy(i + 1).start()

    pos_copy(i).wait()
    tile_copy(i, slot).wait()
    pslot = i % 2
    for k in range(TOP_K):
        def issue(r, carry, k=k):
            pltpu.make_async_copy(xbuf.at[slot, pl.ds(r, 1)],
                                  xs_hbm.at[pl.ds(pos_smem[pslot, k, r], 1)],
                                  row_sem.at[slot]).start()
            return carry
        lax.fori_loop(0, tm, issue, 0, unroll=8)

    @pl.when(i == steps - 1)
    def _():
        wait_rows(slot)

        @pl.when(steps > 1)
        def _():
            wait_rows((i + 2) % 3)


def _dispatch(pos, x, pad_from, pad_to, used, total_rows):
    n, d = x.shape
    tm = DISPATCH_ROWS
    grid_spec = pltpu.PrefetchScalarGridSpec(
        num_scalar_prefetch=3,
        grid=(n // tm,),
        in_specs=[pl.BlockSpec(memory_space=pl.ANY),
                  pl.BlockSpec(memory_space=pl.ANY)],
        out_specs=pl.BlockSpec(memory_space=pl.ANY),
        scratch_shapes=[pltpu.SMEM((2, TOP_K, tm), jnp.int32),
                        pltpu.VMEM((3, tm, d), x.dtype),
                        pltpu.VMEM((MOE_ROWS, d), x.dtype),
                        pltpu.SemaphoreType.DMA((3,)),
                        pltpu.SemaphoreType.DMA((3,)),
                        pltpu.SemaphoreType.DMA((2,)),
                        pltpu.SemaphoreType.DMA],
    )
    return pl.pallas_call(
        _dispatch_kernel,
        grid_spec=grid_spec,
        out_shape=jax.ShapeDtypeStruct((total_rows, d), x.dtype),
        compiler_params=_params(("arbitrary",)),
        name="dispatch",
    )(pad_from, pad_to, used, pos.reshape(TOP_K, n // tm, tm).transpose(1, 0, 2), x)


def _experts_kernel(be_ref, used_ref, x_ref, wg_ref, wu_ref, wd_ref, y_ref,
                    wg_bf, wu_bf, wd_bf):
    i = pl.program_id(0)
    used = used_ref[0]

    @pl.when(i < used)
    def _():
        changed = jnp.logical_or(i == 0, be_ref[jnp.maximum(i - 1, 0)] != be_ref[i])

        @pl.when(changed)
        def _():
            wg_bf[...] = wg_ref[...].astype(BF16)
            wu_bf[...] = wu_ref[...].astype(BF16)
            wd_bf[...] = wd_ref[...].astype(BF16)

        x = x_ref[...].astype(BF16)
        hid = jax.nn.silu(_dot(x, wg_bf[...])) * _dot(x, wu_bf[...])
        y_ref[...] = _dot(hid.astype(BF16), wd_bf[...])

    @pl.when(i >= used)
    def _():
        y_ref[...] = jnp.zeros_like(y_ref)


def _experts(block_expert, used, xs, w_gate, w_up, w_down, layer):
    nb = block_expert.shape[0]
    rows = MOE_ROWS
    d = xs.shape[1]
    de = w_gate.shape[3]
    grid_spec = pltpu.PrefetchScalarGridSpec(
        num_scalar_prefetch=2,
        grid=(nb,),
        in_specs=[
            pl.BlockSpec((rows, d), lambda i, be, u: (jnp.minimum(i, u[0] - 1), 0)),
            pl.BlockSpec((None, None, d, de), lambda i, be, u: (layer, be[i], 0, 0)),
            pl.BlockSpec((None, None, d, de), lambda i, be, u: (layer, be[i], 0, 0)),
            pl.BlockSpec((None, None, de, d), lambda i, be, u: (layer, be[i], 0, 0)),
        ],
        out_specs=pl.BlockSpec((rows, d), lambda i, be, u: (i, 0)),
        scratch_shapes=[
            pltpu.VMEM((d, de), BF16),
            pltpu.VMEM((d, de), BF16),
            pltpu.VMEM((de, d), BF16),
        ],
    )
    return pl.pallas_call(
        _experts_kernel,
        grid_spec=grid_spec,
        out_shape=jax.ShapeDtypeStruct((nb * rows, d), F32),
        compiler_params=_params(("arbitrary",)),
        name="experts",
    )(block_expert, used, xs, w_gate, w_up, w_down)


def _combine_kernel(pos_hbm, y_hbm, gate_ref, h_ref, wsg_ref, wsu_ref, wsd_ref, g_ref, b_ref,
                    o_ref, obf_ref, ybuf, pos_smem, row_sem, pos_sem):
    i = pl.program_id(0)
    steps = pl.num_programs(0)
    tm = h_ref.shape[0]
    slot = i % 2

    def start_gather(block, s):
        cp = pltpu.make_async_copy(pos_hbm.at[block], pos_smem.at[s], pos_sem.at[s])
        cp.start()
        cp.wait()
        for k in range(TOP_K):
            def issue(r, carry, k=k):
                pltpu.make_async_copy(y_hbm.at[pl.ds(pos_smem[s, k, r], 1)],
                                      ybuf.at[s, k, pl.ds(r, 1)], row_sem.at[s]).start()
                return carry
            lax.fori_loop(0, tm, issue, 0, unroll=8)

    @pl.when(i == 0)
    def _():
        start_gather(0, 0)

    @pl.when(i + 1 < steps)
    def _():
        start_gather(i + 1, 1 - slot)

    h = h_ref[...]
    x = h.astype(BF16)
    hid = jax.nn.silu(_dot(x, wsg_ref[...])) * _dot(x, wsu_ref[...])
    y = ALPHA * h + _dot(hid.astype(BF16), wsd_ref[...])

    for k in range(TOP_K):
        pltpu.make_async_copy(y_hbm.at[pl.ds(0, tm)], ybuf.at[slot, k], row_sem.at[slot]).wait()
    gates = gate_ref[...]
    for k in range(TOP_K):
        y = y + ybuf[slot, k] * gates[:, k:k + 1]
    out = _layer_norm(y, g_ref[...], b_ref[...])
    o_ref[...] = out
    obf_ref[...] = out.astype(BF16)


def _combine(pos, y_rows, gates, h, ws_gate, ws_up, ws_down, ln_g, ln_b, layer):
    n, d = h.shape
    tm = COMBINE_ROWS
    ds = ws_gate.shape[2]
    return pl.pallas_call(
        _combine_kernel,
        grid=(n // tm,),
        in_specs=[pl.BlockSpec(memory_space=pl.ANY),
                  pl.BlockSpec(memory_space=pl.ANY),
                  pl.BlockSpec((tm, TOP_K), lambda i: (i, 0)),
                  pl.BlockSpec((tm, d), lambda i: (i, 0)),
                  _layer_spec((d, ds), layer),
                  _layer_spec((d, ds), layer),
                  _layer_spec((ds, d), layer),
                  _layer_spec((1, d), layer),
                  _layer_spec((1, d), layer)],
        out_specs=[pl.BlockSpec((tm, d), lambda i: (i, 0)),
                   pl.BlockSpec((tm, d), lambda i: (i, 0))],
        out_shape=[jax.ShapeDtypeStruct((n, d), F32), jax.ShapeDtypeStruct((n, d), BF16)],
        scratch_shapes=[pltpu.VMEM((2, TOP_K, tm, d), F32),
                        pltpu.SMEM((2, TOP_K, tm), jnp.int32),
                        pltpu.SemaphoreType.DMA((2,)),
                        pltpu.SemaphoreType.DMA((2,))],
        compiler_params=_params(("arbitrary",)),
        name="combine",
    )(pos.reshape(TOP_K, n // tm, tm).transpose(1, 0, 2), y_rows, gates, h,
      ws_gate, ws_up, ws_down, ln_g, ln_b)


def _alibi_slopes(n):
    return 2.0 ** (-8.0 * jnp.arange(1, n + 1, dtype=F32) / n)


def kernel(x, mem, w_in, a_sinks, g_a, g_b, w_out, ln1_g, ln1_b, wq_m, wk_m, wv_m, wo_m,
           ln2_g, ln2_b, w_router, router_bias, w_gate, w_up, w_down, ws_gate, ws_up, ws_down,
           ln3_g, ln3_b):
    b, length, d = x.shape
    n = b * length
    mlen = mem.shape[1]
    depth = w_in.shape[0]
    h = x.reshape(n, d)
    h_bf = h.astype(BF16)
    mem_bf = mem.reshape(b * mlen, d).astype(BF16)
    slopes_a = _alibi_slopes(A_Q_HEADS)
    slopes_b = _alibi_slopes(B_HEADS)

    w_in_bf = w_in.astype(BF16)
    w_in_a = w_in_bf[:, :, :A_PROJ_WIDTH]
    w_in_b = w_in_bf[:, :, A_PROJ_WIDTH:]
    w_out_bf = w_out.astype(BF16)
    wq_bf, wk_bf, wv_bf, wo_bf = (w.astype(BF16) for w in (wq_m, wk_m, wv_m, wo_m))
    wsg_bf, wsu_bf, wsd_bf = (w.astype(BF16) for w in (ws_gate, ws_up, ws_down))
    row = lambda p: p.reshape(depth, 1, p.shape[1])
    g_b3, ln1_g3, ln1_b3, ln2_g3, ln2_b3, ln3_g3, ln3_b3 = (
        row(p) for p in (g_b, ln1_g, ln1_b, ln2_g, ln2_b, ln3_g, ln3_b))

    for l in range(depth):
        proj_a = _matmul(h_bf, w_in_a, l, 512, A_PROJ_WIDTH, BF16)
        proj_b = _matmul(h_bf, w_in_b, l, 512, B_WIDTH, F32)
        mixed_a = _attn_a(proj_a.reshape(b, length, A_PROJ_WIDTH), slopes_a, a_sinks[l], g_a[l])
        out_b = _attn_b(proj_b.reshape(b, length, 3 * B_WIDTH), slopes_b)
        h, h_bf = _mix_out(mixed_a.reshape(n, A_WIDTH), out_b.reshape(n, B_WIDTH), g_b3,
                           w_out_bf, h, ln1_g3, ln1_b3, l)

        km = _matmul(mem_bf, wk_bf, l, b * mlen, d // 2, BF16)
        vm = _matmul(mem_bf, wv_bf, l, b * mlen, d // 2, BF16)
        o = _mem_attn(h_bf.reshape(b, length, d), wq_bf,
                      km.reshape(b, mlen, d), vm.reshape(b, mlen, d), l)
        h, h_bf = _proj_out(o.reshape(n, d), wo_bf, h, ln2_g3, ln2_b3, l)

        route_tm = 512
        idx_t, gate_t, mask_t, cnt = _router(h, w_router[l].T, router_bias[l], route_tm)
        tile_offsets, block_expert, used, pad_from, pad_to = _block_layout(cnt[:, :, 0], n)
        pos = _plan(mask_t, idx_t, tile_offsets, route_tm)
        xs = _dispatch(pos, h, pad_from, pad_to, used, block_expert.shape[0] * MOE_ROWS)
        y_rows = _experts(block_expert, used, xs, w_gate, w_up, w_down, l)
        h, h_bf = _combine(pos, y_rows, gate_t.T, h, wsg_bf, wsu_bf, wsd_bf, ln3_g3, ln3_b3, l)
    return h.reshape(b, length, d)
```

```python
import math

import jax
import jax.numpy as jnp
from jax import lax
from jax.experimental import pallas as pl
from jax.experimental.pallas import tpu as pltpu

F32 = jnp.float32
BF16 = jnp.bfloat16

HEAD_DIM = 64
A_Q_HEADS = 16
A_KV_HEADS = 2
A_WINDOW = 128
B_HEADS = 16
B_BRANCH_DILATIONS = (1, 4, 16)
A_WIDTH = A_Q_HEADS * HEAD_DIM
A_KV_WIDTH = A_KV_HEADS * HEAD_DIM
B_WIDTH = B_HEADS * HEAD_DIM
A_PROJ_WIDTH = A_WIDTH + 2 * A_KV_WIDTH
BLK = 128

MEM_HEADS = 4

N_EXPERTS = 64
TOP_K = 8
N_GROUPS = 8
GROUP_SIZE = N_EXPERTS // N_GROUPS
TOPK_GROUPS = 4
ROUTE_SCALE = 2.5

DEPTH = 2
ALPHA = (2.0 * DEPTH) ** 0.25
LN_EPS = 1e-5
RMS_EPS = 1e-6

LANES = 128
SUBLANES = 8
NEG = -1e30

MOE_ROWS = 256
COMBINE_ROWS = 128
VMEM_LIMIT = 56 * 1024 * 1024


def _params(sem, vmem=VMEM_LIMIT):
    return pltpu.CompilerParams(dimension_semantics=sem, vmem_limit_bytes=vmem)


def _dot(a, b):
    return jnp.dot(a, b, preferred_element_type=F32)


def _dot_nt(a, b):
    return lax.dot_general(a, b, (((1,), (1,)), ((), ())), preferred_element_type=F32)


def _layer_norm(y, g, b):
    mu = jnp.mean(y, axis=-1, keepdims=True)
    yc = y - mu
    var = jnp.mean(yc * yc, axis=-1, keepdims=True)
    return yc * lax.rsqrt(var + LN_EPS) * g + b


def _layer_spec(shape, layer, index=None, single=True):
    index = index or (0,) * len(shape)
    mode = dict(pipeline_mode=pl.Buffered(1)) if single else {}
    return pl.BlockSpec((None,) + tuple(shape), lambda *_: (layer,) + tuple(index), **mode)


def _mm_kernel(x_ref, w_ref, o_ref):
    o_ref[...] = _dot(x_ref[...], w_ref[...]).astype(o_ref.dtype)


def _matmul(x, w, layer, tm, tn, out_dtype):
    m, k = x.shape
    n = w.shape[2]
    return pl.pallas_call(
        _mm_kernel,
        grid=(n // tn, m // tm),
        in_specs=[pl.BlockSpec((tm, k), lambda j, i: (i, 0)),
                  pl.BlockSpec((None, k, tn), lambda j, i: (layer, 0, j))],
        out_specs=pl.BlockSpec((tm, tn), lambda j, i: (i, j)),
        out_shape=jax.ShapeDtypeStruct((m, n), out_dtype),
        compiler_params=_params(("arbitrary", "arbitrary")),
        name="matmul",
    )(x, w)


def _band_bias(stacked_heads, max_offset, slope_of_stack):
    rows = stacked_heads * BLK
    row = lax.broadcasted_iota(jnp.int32, (rows, 2 * BLK), 0)
    col = lax.broadcasted_iota(jnp.int32, (rows, 2 * BLK), 1)
    offset = (row & (BLK - 1)) + BLK - col
    slope = jnp.full((rows, 2 * BLK), slope_of_stack[stacked_heads - 1], F32)
    for a in range(stacked_heads - 2, -1, -1):
        slope = jnp.where(row < (a + 1) * BLK, slope_of_stack[a], slope)
    valid = jnp.logical_and(offset >= 0, offset <= max_offset)
    bias = jnp.where(valid, -slope * offset.astype(F32), NEG)
    return bias, jnp.where(col < BLK, NEG, bias)


def _roll_half(t):
    return pltpu.roll(t.astype(F32), LANES // 2, axis=1).astype(BF16)


A_STACK = 4


def _attn_a_kernel(slopes_ref, sinks_ref, q_ref, kc_ref, kp_ref, vc_ref, vp_ref, g_ref,
                   o_ref, o_scr, bias_scr):
    n = pl.program_id(1)
    tiles_per_kv = A_Q_HEADS // A_KV_HEADS // 2

    def head_of(kv, half, a):
        return 2 * (kv * tiles_per_kv + a) + half

    @pl.when(jnp.logical_and(pl.program_id(0) == 0, n == 0))
    def _():
        for kv in range(A_KV_HEADS):
            for half in range(2):
                slopes = [slopes_ref[head_of(kv, half, a)] for a in range(A_STACK)]
                bias, bias_first = _band_bias(A_STACK, A_WINDOW - 1, slopes)
                bias_scr[0, 2 * kv + half] = bias
                bias_scr[1, 2 * kv + half] = bias_first

    first = (n == 0).astype(jnp.int32)
    lane = lax.broadcasted_iota(jnp.int32, (BLK, LANES), 1)
    lo = lane < HEAD_DIM
    srow = lax.broadcasted_iota(jnp.int32, (A_STACK * BLK, 1), 0)

    k_ver = (jnp.concatenate([kp_ref[0], kc_ref[0]], axis=0),
             jnp.concatenate([_roll_half(kp_ref[0]), _roll_half(kc_ref[0])], axis=0))
    v_ver = (jnp.concatenate([vp_ref[0], vc_ref[0]], axis=0),
             jnp.concatenate([_roll_half(vp_ref[0]), _roll_half(vc_ref[0])], axis=0))

    ss = jnp.zeros((BLK, 1), F32)
    for kv in range(A_KV_HEADS):
        outs = []
        for half in range(2):
            keep = lo if half == 0 else jnp.logical_not(lo)
            parts = []
            for a in range(A_STACK):
                t = kv * tiles_per_kv + a
                qt = q_ref[0, :, t * LANES:(t + 1) * LANES] * jnp.asarray(0.125, BF16)
                parts.append(jnp.where(keep, qt, jnp.zeros_like(qt)))
            qs = jnp.concatenate(parts, axis=0)
            ver = kv ^ half
            logits = _dot_nt(qs, k_ver[ver]) + bias_scr[first, 2 * kv + half]
            sink = jnp.full((A_STACK * BLK, 1), sinks_ref[head_of(kv, half, A_STACK - 1)], F32)
            for a in range(A_STACK - 2, -1, -1):
                sink = jnp.where(srow < (a + 1) * BLK, sinks_ref[head_of(kv, half, a)], sink)
            m = jnp.maximum(jnp.max(logits, axis=1, keepdims=True), sink)
            e = jnp.exp(logits - m)
            s = jnp.sum(e, axis=1, keepdims=True) + jnp.exp(sink - m)
            acc = _dot(e.astype(BF16), v_ver[ver])
            outs.append(acc * (1.0 / s))
        for a in range(A_STACK):
            t = kv * tiles_per_kv + a
            o_tile = jnp.where(lo, outs[0][a * BLK:(a + 1) * BLK], outs[1][a * BLK:(a + 1) * BLK])
            ss = ss + jnp.sum(o_tile * o_tile, axis=1, keepdims=True)
            o_scr[:, t * LANES:(t + 1) * LANES] = o_tile
    scale = lax.rsqrt(ss * (1.0 / A_WIDTH) + RMS_EPS)
    o_ref[0] = (o_scr[...] * scale * g_ref[...]).astype(o_ref.dtype)


def _attn_a(proj_a, slopes, sinks, g_a):
    b, length, _ = proj_a.shape
    kcol = A_WIDTH // LANES
    vcol = kcol + 1
    grid_spec = pltpu.PrefetchScalarGridSpec(
        num_scalar_prefetch=2,
        grid=(b, length // BLK),
        in_specs=[
            pl.BlockSpec((1, BLK, A_WIDTH), lambda i, n, *_: (i, n, 0)),
            pl.BlockSpec((1, BLK, LANES), lambda i, n, *_: (i, n, kcol)),
            pl.BlockSpec((1, BLK, LANES), lambda i, n, *_: (i, jnp.maximum(n - 1, 0), kcol)),
            pl.BlockSpec((1, BLK, LANES), lambda i, n, *_: (i, n, vcol)),
            pl.BlockSpec((1, BLK, LANES), lambda i, n, *_: (i, jnp.maximum(n - 1, 0), vcol)),
            pl.BlockSpec((1, A_WIDTH), lambda i, n, *_: (0, 0)),
        ],
        out_specs=pl.BlockSpec((1, BLK, A_WIDTH), lambda i, n, *_: (i, n, 0)),
        scratch_shapes=[pltpu.VMEM((BLK, A_WIDTH), F32),
                        pltpu.VMEM((2, 2 * A_KV_HEADS, A_STACK * BLK, 2 * BLK), F32)],
    )
    return pl.pallas_call(
        _attn_a_kernel,
        grid_spec=grid_spec,
        out_shape=jax.ShapeDtypeStruct((b, length, A_WIDTH), BF16),
        compiler_params=_params(("arbitrary", "arbitrary")),
        name="attn_a",
    )(slopes, sinks, proj_a, proj_a, proj_a, proj_a, proj_a, g_a.reshape(1, A_WIDTH))


def _attn_b_kernel(slopes_ref, q_ref, k_ref, v_ref, o_ref, acc_scr, m_scr, s_scr, bias_scr):
    hp = pl.program_id(1)
    length = q_ref.shape[1]
    lane = lax.broadcasted_iota(jnp.int32, (BLK, LANES), 1)
    lo = lane < HEAD_DIM
    qrow = lax.broadcasted_iota(jnp.int32, (2 * BLK, LANES), 0)
    qlane = lax.broadcasted_iota(jnp.int32, (2 * BLK, LANES), 1)
    qmask = (qrow >= BLK) == (qlane >= HEAD_DIM)

    for branch, dil in enumerate(B_BRANCH_DILATIONS):
        slopes = [slopes_ref[2 * hp] * float(dil), slopes_ref[2 * hp + 1] * float(dil)]
        bias, bias_first = _band_bias(2, BLK, slopes)
        bias_scr[branch, 0] = bias
        bias_scr[branch, 1] = bias_first

    for branch, dil in enumerate(B_BRANCH_DILATIONS):
        blocks_per_class = length // (dil * BLK)

        def rows(start, dil=dil):
            if dil == 1:
                return pl.ds(start, BLK)
            return pl.ds(start, BLK, stride=dil)

        def body(it, carry, branch=branch, dil=dil, blocks_per_class=blocks_per_class, rows=rows):
            res = it // blocks_per_class
            blk = it % blocks_per_class
            cur = rows(res + dil * BLK * blk)
            prev = rows(res + dil * BLK * jnp.maximum(blk - 1, 0))
            first = jnp.where(blk == 0, 1, 0)
            q2 = q_ref[0, cur, :] * 0.125
            qs = jnp.concatenate([q2, q2], axis=0)
            qs = jnp.where(qmask, qs, 0.0).astype(BF16)
            k2 = jnp.concatenate([k_ref[0, prev, :], k_ref[0, cur, :]], axis=0).astype(BF16)
            v2 = jnp.concatenate([v_ref[0, prev, :], v_ref[0, cur, :]], axis=0).astype(BF16)
            logits = _dot_nt(qs, k2) + bias_scr[branch, first]
            m = jnp.max(logits, axis=1, keepdims=True)
            e = jnp.exp(logits - m)
            s = jnp.sum(e, axis=1, keepdims=True)
            acc = _dot(e.astype(BF16), v2)
            m_t = jnp.where(lo, m[:BLK], m[BLK:])
            s_t = jnp.where(lo, s[:BLK], s[BLK:])
            acc_t = jnp.where(lo, acc[:BLK], acc[BLK:])
            if branch == 0:
                m_scr[cur, :] = m_t
                s_scr[cur, :] = s_t
                acc_scr[cur, :] = acc_t
            else:
                m_old = m_scr[cur, :]
                s_old = s_scr[cur, :]
                acc_old = acc_scr[cur, :]
                m_new = jnp.maximum(m_old, m_t)
                w_old = jnp.exp(m_old - m_new)
                w_blk = jnp.exp(m_t - m_new)
                m_scr[cur, :] = m_new
                s_scr[cur, :] = s_old * w_old + s_t * w_blk
                acc_scr[cur, :] = acc_old * w_old + acc_t * w_blk
            return carry

        lax.fori_loop(0, length // BLK, body, 0, unroll=2)

    o_ref[0] = acc_scr[...] / s_scr[...]


def _attn_b(proj_b, slopes):
    b, length, _ = proj_b.shape
    pairs = B_WIDTH // LANES
    grid_spec = pltpu.PrefetchScalarGridSpec(
        num_scalar_prefetch=1,
        grid=(b, pairs),
        in_specs=[
            pl.BlockSpec((1, length, LANES), lambda i, p, *_: (i, 0, p)),
            pl.BlockSpec((1, length, LANES), lambda i, p, *_: (i, 0, pairs + p)),
            pl.BlockSpec((1, length, LANES), lambda i, p, *_: (i, 0, 2 * pairs + p)),
        ],
        out_specs=pl.BlockSpec((1, length, LANES), lambda i, p, *_: (i, 0, p)),
        scratch_shapes=[pltpu.VMEM((length, LANES), F32)] * 3
        + [pltpu.VMEM((len(B_BRANCH_DILATIONS), 2, 2 * BLK, 2 * BLK), F32)],
    )
    return pl.pallas_call(
        _attn_b_kernel,
        grid_spec=grid_spec,
        out_shape=jax.ShapeDtypeStruct((b, length, B_WIDTH), F32),
        compiler_params=_params(("arbitrary", "arbitrary")),
        name="attn_b",
    )(slopes, proj_b, proj_b, proj_b)


def _mix_out_kernel(xa_ref, xb_ref, gb_ref, wa_ref, wb_ref, h_ref, g_ref, b_ref,
                    o_ref, obf_ref):
    xb = xb_ref[...]
    scale = lax.rsqrt(jnp.mean(xb * xb, axis=-1, keepdims=True) + RMS_EPS)
    xb = (xb * scale * gb_ref[...]).astype(BF16)
    y = _dot(xa_ref[...], wa_ref[...]) + _dot(xb, wb_ref[...])
    out = _layer_norm(ALPHA * h_ref[...] + y, g_ref[...], b_ref[...])
    o_ref[...] = out
    obf_ref[...] = out.astype(BF16)


def _proj_out_kernel(x_ref, w_ref, h_ref, g_ref, b_ref, o_ref, obf_ref):
    y = _dot(x_ref[...], w_ref[...])
    out = _layer_norm(ALPHA * h_ref[...] + y, g_ref[...], b_ref[...])
    o_ref[...] = out
    obf_ref[...] = out.astype(BF16)


def _mix_out(xa, xb, g_b, w_out, h, ln_g, ln_b, layer, tm=512):
    n, d = h.shape
    ka = xa.shape[1]
    kb = xb.shape[1]
    return pl.pallas_call(
        _mix_out_kernel,
        grid=(n // tm,),
        in_specs=[pl.BlockSpec((tm, ka), lambda i: (i, 0)),
                  pl.BlockSpec((tm, kb), lambda i: (i, 0)),
                  _layer_spec((1, kb), layer),
                  _layer_spec((ka, d), layer, (0, 0)),
                  _layer_spec((kb, d), layer, (ka // kb, 0)),
                  pl.BlockSpec((tm, d), lambda i: (i, 0)),
                  _layer_spec((1, d), layer),
                  _layer_spec((1, d), layer)],
        out_specs=[pl.BlockSpec((tm, d), lambda i: (i, 0)),
                   pl.BlockSpec((tm, d), lambda i: (i, 0))],
        out_shape=[jax.ShapeDtypeStruct((n, d), F32), jax.ShapeDtypeStruct((n, d), BF16)],
        compiler_params=_params(("arbitrary",)),
        name="mix_out",
    )(xa, xb, g_b, w_out, w_out, h, ln_g, ln_b)


def _proj_out(x, w, h, ln_g, ln_b, layer, tm=512):
    n, d = h.shape
    k = x.shape[1]
    return pl.pallas_call(
        _proj_out_kernel,
        grid=(n // tm,),
        in_specs=[pl.BlockSpec((tm, k), lambda i: (i, 0)),
                  _layer_spec((k, d), layer),
                  pl.BlockSpec((tm, d), lambda i: (i, 0)),
                  _layer_spec((1, d), layer),
                  _layer_spec((1, d), layer)],
        out_specs=[pl.BlockSpec((tm, d), lambda i: (i, 0)),
                   pl.BlockSpec((tm, d), lambda i: (i, 0))],
        out_shape=[jax.ShapeDtypeStruct((n, d), F32), jax.ShapeDtypeStruct((n, d), BF16)],
        compiler_params=_params(("arbitrary",)),
        name="proj_out",
    )(x, w, h, ln_g, ln_b)


def _mem_attn_kernel(x_ref, wq_ref, k_ref, v_ref, o_ref):
    d = x_ref.shape[2]
    hd = d // MEM_HEADS
    q = _dot(x_ref[0], wq_ref[...]) * (1.0 / math.sqrt(hd))
    for head in range(MEM_HEADS):
        cols = slice(head * hd, (head + 1) * hd)
        logits = _dot_nt(q[:, cols].astype(BF16), k_ref[0, :, cols])
        m = jnp.max(logits, axis=1, keepdims=True)
        e = jnp.exp(logits - m)
        s = jnp.sum(e, axis=1, keepdims=True)
        o = _dot(e.astype(BF16), v_ref[0, :, cols]) / s
        o_ref[0, :, cols] = o.astype(o_ref.dtype)


def _mem_attn(x, wq, km, vm, layer, tm=512):
    b, length, d = x.shape
    mlen = km.shape[1]
    return pl.pallas_call(
        _mem_attn_kernel,
        grid=(b, length // tm),
        in_specs=[pl.BlockSpec((1, tm, d), lambda i, j: (i, j, 0)),
                  _layer_spec((d, d), layer),
                  pl.BlockSpec((1, mlen, d), lambda i, j: (i, 0, 0)),
                  pl.BlockSpec((1, mlen, d), lambda i, j: (i, 0, 0))],
        out_specs=pl.BlockSpec((1, tm, d), lambda i, j: (i, j, 0)),
        out_shape=jax.ShapeDtypeStruct((b, length, d), BF16),
        compiler_params=_params(("arbitrary", "arbitrary")),
        name="mem_attn",
    )(x, wq, km, vm)


def _split_bf16(a):
    hi = a.astype(BF16)
    lo = (a - hi.astype(F32)).astype(BF16)
    return hi, lo


def _router_kernel(x_ref, wt_ref, bias_ref, idx_ref, gate_ref, mask_ref, cnt_ref):
    tm = x_ref.shape[0]
    xh, xl = _split_bf16(x_ref[...])
    wh, wl = _split_bf16(wt_ref[...])
    logits = _dot_nt(wh, xh) + (_dot_nt(wh, xl) + _dot_nt(wl, xh))
    scores = jax.nn.sigmoid(logits)
    sel = scores + bias_ref[...]

    member = lax.broadcasted_iota(jnp.int32, (GROUP_SIZE, tm), 0)
    group_scores = []
    for g in range(N_GROUPS):
        v = sel[g * GROUP_SIZE:(g + 1) * GROUP_SIZE, :]
        m1 = jnp.max(v, axis=0, keepdims=True)
        first = jnp.min(jnp.where(v == m1, member, GROUP_SIZE), axis=0, keepdims=True)
        m2 = jnp.max(jnp.where(member == first, -jnp.inf, v), axis=0, keepdims=True)
        group_scores.append(m1 + m2)
    masked = []
    for g in range(N_GROUPS):
        beaten_by = jnp.zeros((1, tm), jnp.int32)
        for o in range(N_GROUPS):
            if o == g:
                continue
            wins = group_scores[o] > group_scores[g]
            if o < g:
                wins = jnp.logical_or(wins, group_scores[o] == group_scores[g])
            beaten_by = beaten_by + wins.astype(jnp.int32)
        keep = beaten_by < TOPK_GROUPS
        masked.append(jnp.where(keep, sel[g * GROUP_SIZE:(g + 1) * GROUP_SIZE, :], -jnp.inf))
    cand = jnp.concatenate(masked, axis=0)

    expert = lax.broadcasted_iota(jnp.int32, (N_EXPERTS, tm), 0)
    beaten_by = jnp.zeros((N_EXPERTS, tm), jnp.int32)
    for o in range(N_EXPERTS):
        other = cand[o:o + 1, :]
        wins = jnp.logical_or(other > cand, jnp.logical_and(other == cand, expert > o))
        beaten_by = beaten_by + wins.astype(jnp.int32)
    chosen = beaten_by < TOP_K
    chosen_i = chosen.astype(jnp.int32)
    mask_ref[...] = chosen_i
    cnt_ref[...] = jnp.broadcast_to(jnp.sum(chosen_i, axis=1, keepdims=True), cnt_ref.shape)

    picked = jnp.where(chosen, scores, 0.0)
    gate = picked / jnp.sum(picked, axis=0, keepdims=True) * ROUTE_SCALE

    running = jnp.zeros((1, tm), jnp.int32)
    slots = []
    for g in range(N_GROUPS):
        part = chosen_i[g * GROUP_SIZE:(g + 1) * GROUP_SIZE, :]
        rows_ = []
        for r in range(GROUP_SIZE):
            rows_.append(running)
            running = running + part[r:r + 1, :]
        slots.append(jnp.concatenate(rows_, axis=0))
    slot = jnp.concatenate(slots, axis=0)
    idx_rows = []
    gate_rows = []
    for k in range(TOP_K):
        hit = jnp.logical_and(chosen, slot == k)
        idx_rows.append(jnp.sum(jnp.where(hit, expert, 0), axis=0, keepdims=True))
        gate_rows.append(jnp.sum(jnp.where(hit, gate, 0.0), axis=0, keepdims=True))
    idx_ref[...] = jnp.concatenate(idx_rows, axis=0)
    gate_ref[...] = jnp.concatenate(gate_rows, axis=0)


def _router(x, w_router_t, bias, tm=512):
    n, d = x.shape
    return pl.pallas_call(
        _router_kernel,
        grid=(n // tm,),
        in_specs=[pl.BlockSpec((tm, d), lambda i: (i, 0)),
                  pl.BlockSpec((N_EXPERTS, d), lambda i: (0, 0)),
                  pl.BlockSpec((N_EXPERTS, 1), lambda i: (0, 0))],
        out_specs=[pl.BlockSpec((TOP_K, tm), lambda i: (0, i)),
                   pl.BlockSpec((TOP_K, tm), lambda i: (0, i)),
                   pl.BlockSpec((N_EXPERTS, tm), lambda i: (0, i)),
                   pl.BlockSpec((None, N_EXPERTS, LANES), lambda i: (i, 0, 0))],
        out_shape=[jax.ShapeDtypeStruct((TOP_K, n), jnp.int32),
                   jax.ShapeDtypeStruct((TOP_K, n), F32),
                   jax.ShapeDtypeStruct((N_EXPERTS, n), jnp.int32),
                   jax.ShapeDtypeStruct((n // tm, N_EXPERTS, LANES), jnp.int32)],
        compiler_params=_params(("arbitrary",)),
        name="router",
    )(x, w_router_t, bias.reshape(N_EXPERTS, 1))


def _plan_kernel(mask_ref, idx_ref, off_ref, pos_ref):
    tm = mask_ref.shape[1]
    before = (lax.broadcasted_iota(jnp.int32, (tm, tm), 0)
              < lax.broadcasted_iota(jnp.int32, (tm, tm), 1)).astype(BF16)
    rank = _dot(mask_ref[...].astype(BF16), before).astype(jnp.int32)
    row_of = off_ref[...] + rank
    expert = lax.broadcasted_iota(jnp.int32, (N_EXPERTS, tm), 0)
    rows_ = []
    for k in range(TOP_K):
        hit = expert == idx_ref[k:k + 1, :]
        rows_.append(jnp.sum(jnp.where(hit, row_of, 0), axis=0, keepdims=True))
    pos_ref[...] = jnp.concatenate(rows_, axis=0)


def _plan(mask_t, idx_t, tile_offsets, tm):
    n = mask_t.shape[1]
    return pl.pallas_call(
        _plan_kernel,
        grid=(n // tm,),
        in_specs=[pl.BlockSpec((N_EXPERTS, tm), lambda i: (0, i)),
                  pl.BlockSpec((TOP_K, tm), lambda i: (0, i)),
                  pl.BlockSpec((None, N_EXPERTS, 1), lambda i: (i, 0, 0))],
        out_specs=pl.BlockSpec((TOP_K, tm), lambda i: (0, i)),
        out_shape=jax.ShapeDtypeStruct((TOP_K, n), jnp.int32),
        compiler_params=_params(("arbitrary",)),
        name="plan",
    )(mask_t, idx_t, tile_offsets)


def _block_layout(counts_per_tile, n):
    rows = MOE_ROWS
    nb = (n * TOP_K) // rows + N_EXPERTS
    counts = jnp.sum(counts_per_tile, axis=0)
    padded = (counts + rows - 1) // rows * rows
    padded_ends = jnp.cumsum(padded)
    padded_starts = padded_ends - padded
    tile_offsets = padded_starts[None, :] + jnp.cumsum(counts_per_tile, axis=0) - counts_per_tile
    used = (padded_ends[-1] // rows).astype(jnp.int32)
    block_start = jnp.arange(nb, dtype=jnp.int32) * rows
    block_expert = jnp.minimum(
        jnp.sum((padded_ends[None, :] <= block_start[:, None]).astype(jnp.int32), axis=1),
        N_EXPERTS - 1)
    last = block_expert[jnp.maximum(used - 1, 0)]
    block_expert = jnp.where(jnp.arange(nb) < used, block_expert, last)
    pad_from = (padded_starts + counts).astype(jnp.int32)
    return (tile_offsets.astype(jnp.int32)[:, :, None], block_expert, used.reshape(1),
            pad_from, padded_ends.astype(jnp.int32))


DISPATCH_ROWS = 128


def _pack_rows(x):
    k = x.shape[1] // 2
    lo = pltpu.bitcast(x[:, :k].astype(BF16).astype(F32), jnp.uint32) >> 16
    hi = pltpu.bitcast(x[:, k:].astype(BF16).astype(F32), jnp.uint32) & jnp.uint32(0xFFFF0000)
    return lo | hi


def _unpack_rows(w):
    lo = pltpu.bitcast(w << 16, F32)
    hi = pltpu.bitcast(w & jnp.uint32(0xFFFF0000), F32)
    return lo, hi


def _dispatch_kernel(pad_from_ref, pad_to_ref, used_ref, pos_hbm, x_ref, xs_hbm, pos_smem,
                     xbuf, zeros, row_sem, pos_sem, pad_sem):
    i = pl.program_id(0)
    steps = pl.num_programs(0)
    tm = pos_smem.shape[2]
    slot = i % 3

    def wait_rows(s):
        for _ in range(TOP_K):
            pltpu.make_async_copy(xbuf.at[s], xs_hbm.at[pl.ds(0, tm)], row_sem.at[s]).wait()

    def pos_copy(blk):
        return pltpu.make_async_copy(pos_hbm.at[blk], pos_smem.at[blk % 2], pos_sem.at[blk % 2])

    @pl.when(i == 0)
    def _():
        pos_copy(0).start()
        zeros[...] = jnp.zeros_like(zeros)
        block = zeros.shape[0]

        def tail_copy(blk):
            start = pl.multiple_of(blk * block, block)
            return pltpu.make_async_copy(zeros, xs_hbm.at[pl.ds(start, block)], pad_sem)

        def tail_start(blk, c):
            tail_copy(blk).start()
            return c

        def tail_wait(blk, c):
            tail_copy(blk).wait()
            return c

        lax.fori_loop(used_ref[0], xs_hbm.shape[0] // block, tail_start, 0)
        lax.fori_loop(used_ref[0], xs_hbm.shape[0] // block, tail_wait, 0)

        def per_expert(e, carry):
            def pad_copy(r):
                return pltpu.make_async_copy(zeros.at[pl.ds(0, 1)], xs_hbm.at[pl.ds(r, 1)],
                                             pad_sem)

            def start(r, c):
                pad_copy(r).start()
                return c

            def wait(r, c):
                pad_copy(r).wait()
                return c

            lax.fori_loop(pad_from_ref[e], pad_to_ref[e], start, 0)
            lax.fori_loop(pad_from_ref[e], pad_to_ref[e], wait, 0)
            return carry

        lax.fori_loop(0, N_EXPERTS, per_expert, 0)

    @pl.when(i >= 3)
    def _():
        wait_rows(slot)

    @pl.when(i + 1 < steps)
    def _():
        pos_copy(i + 1).start()

    xbuf[slot] = _pack_rows(x_ref[...])
    pos_copy(i).wait()
    pslot = i % 2

    for r in range(tm):
        for k in range(TOP_K):
            pltpu.make_async_copy(xbuf.at[slot, pl.ds(r, 1)],
                                  xs_hbm.at[pl.ds(pos_smem[pslot, k, r], 1)],
                                  row_sem.at[slot]).start()

    @pl.when(i == steps - 1)
    def _():
        wait_rows(slot)

        @pl.when(steps > 1)
        def _():
            wait_rows((i + 2) % 3)

        @pl.when(steps > 2)
        def _():
            wait_rows((i + 1) % 3)


def _dispatch(pos, x, pad_from, pad_to, used, total_rows):
    n, d = x.shape
    tm = DISPATCH_ROWS
    grid_spec = pltpu.PrefetchScalarGridSpec(
        num_scalar_prefetch=3,
        grid=(n // tm,),
        in_specs=[pl.BlockSpec(memory_space=pl.ANY),
                  pl.BlockSpec((tm, d), lambda i, *_: (i, 0))],
        out_specs=pl.BlockSpec(memory_space=pl.ANY),
        scratch_shapes=[pltpu.SMEM((2, TOP_K, tm), jnp.int32),
                        pltpu.VMEM((3, tm, d // 2), jnp.uint32),
                        pltpu.VMEM((MOE_ROWS, d // 2), jnp.uint32),
                        pltpu.SemaphoreType.DMA((3,)),
                        pltpu.SemaphoreType.DMA((2,)),
                        pltpu.SemaphoreType.DMA],
    )
    return pl.pallas_call(
        _dispatch_kernel,
        grid_spec=grid_spec,
        out_shape=jax.ShapeDtypeStruct((total_rows, d // 2), jnp.uint32),
        compiler_params=_params(("arbitrary",)),
        name="dispatch",
    )(pad_from, pad_to, used, pos.reshape(TOP_K, n // tm, tm).transpose(1, 0, 2), x)


def _experts_kernel(be_ref, used_ref, x_ref, wg_ref, wu_ref, wd_ref, y_ref,
                    wg_bf, wu_bf, wd_bf):
    i = pl.program_id(0)
    used = used_ref[0]

    @pl.when(i < used)
    def _():
        changed = jnp.logical_or(i == 0, be_ref[jnp.maximum(i - 1, 0)] != be_ref[i])

        @pl.when(changed)
        def _():
            wg_bf[...] = wg_ref[...].astype(BF16)
            wu_bf[...] = wu_ref[...].astype(BF16)
            wd_bf[...] = wd_ref[...].astype(BF16)

        half = x_ref.shape[1]
        x_lo, x_hi = (t.astype(BF16) for t in _unpack_rows(x_ref[...]))
        gate = _dot(x_lo, wg_bf[:half, :]) + _dot(x_hi, wg_bf[half:, :])
        up = _dot(x_lo, wu_bf[:half, :]) + _dot(x_hi, wu_bf[half:, :])
        hid = jax.nn.silu(gate) * up
        y_ref[...] = _pack_rows(_dot(hid.astype(BF16), wd_bf[...]))

    @pl.when(i >= used)
    def _():
        y_ref[...] = jnp.zeros_like(y_ref)


def _experts(block_expert, used, xs, w_gate, w_up, w_down, layer):
    nb = block_expert.shape[0]
    rows = MOE_ROWS
    d = w_gate.shape[2]
    de = w_gate.shape[3]
    grid_spec = pltpu.PrefetchScalarGridSpec(
        num_scalar_prefetch=2,
        grid=(nb,),
        in_specs=[
            pl.BlockSpec((rows, d // 2), lambda i, be, u: (jnp.minimum(i, u[0] - 1), 0)),
            pl.BlockSpec((None, None, d, de), lambda i, be, u: (layer, be[i], 0, 0)),
            pl.BlockSpec((None, None, d, de), lambda i, be, u: (layer, be[i], 0, 0)),
            pl.BlockSpec((None, None, de, d), lambda i, be, u: (layer, be[i], 0, 0)),
        ],
        out_specs=pl.BlockSpec((rows, d // 2), lambda i, be, u: (i, 0)),
        scratch_shapes=[
            pltpu.VMEM((d, de), BF16),
            pltpu.VMEM((d, de), BF16),
            pltpu.VMEM((de, d), BF16),
        ],
    )
    return pl.pallas_call(
        _experts_kernel,
        grid_spec=grid_spec,
        out_shape=jax.ShapeDtypeStruct((nb * rows, d // 2), jnp.uint32),
        compiler_params=_params(("arbitrary",)),
        name="experts",
    )(block_expert, used, xs, w_gate, w_up, w_down)


def _combine_kernel(pos_hbm, y_hbm, gate_ref, h_ref, wsg_ref, wsu_ref, wsd_ref, g_ref, b_ref,
                    o_ref, obf_ref, ybuf, pos_smem, row_sem, pos_sem):
    i = pl.program_id(0)
    steps = pl.num_programs(0)
    tm = h_ref.shape[0]
    slot = i % 3
    last = steps - 1

    def pos_copy(t):
        return pltpu.make_async_copy(pos_hbm.at[jnp.minimum(t, last)], pos_smem.at[t % 2],
                                     pos_sem.at[t % 2])

    def row_copy(t, k, r):
        return pltpu.make_async_copy(y_hbm.at[pl.ds(pos_smem[t % 2, k, r], 1)],
                                     ybuf.at[t % 3, k, pl.ds(r, 1)], row_sem.at[t % 3])

    def wait_rows(t):
        for k in range(TOP_K):
            pltpu.make_async_copy(y_hbm.at[pl.ds(0, tm)], ybuf.at[t % 3, k],
                                  row_sem.at[t % 3]).wait()

    @pl.when(i == 0)
    def _():
        for t in range(2):
            pos_copy(t).start()
            pos_copy(t).wait()

            def issue(r, carry, t=t):
                for k in range(TOP_K):
                    row_copy(t, k, r).start()
                return carry
            lax.fori_loop(0, tm, issue, 0)

    wait_rows(i)

    @pl.when(i > 0)
    def _():
        pos_copy(i + 2).wait()

    @pl.when(i == 0)
    def _():
        pos_copy(2).start()
        pos_copy(2).wait()

    for r in range(tm):
        for k in range(TOP_K):
            row_copy(i + 2, k, r).start()
    pos_copy(i + 3).start()

    h = h_ref[...]
    x = h.astype(BF16)
    hid = jax.nn.silu(_dot(x, wsg_ref[...])) * _dot(x, wsu_ref[...])
    y = ALPHA * h + _dot(hid.astype(BF16), wsd_ref[...])

    gates = gate_ref[...]
    half = ybuf.shape[3]
    routed_lo = jnp.zeros((tm, half), F32)
    routed_hi = jnp.zeros((tm, half), F32)
    for k in range(TOP_K):
        lo, hi = _unpack_rows(ybuf[slot, k])
        routed_lo = routed_lo + lo * gates[:, k:k + 1]
        routed_hi = routed_hi + hi * gates[:, k:k + 1]
    y = y + jnp.concatenate([routed_lo, routed_hi], axis=1)
    out = _layer_norm(y, g_ref[...], b_ref[...])
    o_ref[...] = out
    obf_ref[...] = out.astype(BF16)

    @pl.when(i == last)
    def _():
        wait_rows(i + 1)
        wait_rows(i + 2)
        pos_copy(i + 3).wait()


def _combine(pos, y_rows, gates, h, ws_gate, ws_up, ws_down, ln_g, ln_b, layer):
    n, d = h.shape
    tm = COMBINE_ROWS
    ds = ws_gate.shape[2]
    return pl.pallas_call(
        _combine_kernel,
        grid=(n // tm,),
        in_specs=[pl.BlockSpec(memory_space=pl.ANY),
                  pl.BlockSpec(memory_space=pl.ANY),
                  pl.BlockSpec((tm, TOP_K), lambda i: (i, 0)),
                  pl.BlockSpec((tm, d), lambda i: (i, 0)),
                  _layer_spec((d, ds), layer),
                  _layer_spec((d, ds), layer),
                  _layer_spec((ds, d), layer),
                  _layer_spec((1, d), layer),
                  _layer_spec((1, d), layer)],
        out_specs=[pl.BlockSpec((tm, d), lambda i: (i, 0)),
                   pl.BlockSpec((tm, d), lambda i: (i, 0))],
        out_shape=[jax.ShapeDtypeStruct((n, d), F32), jax.ShapeDtypeStruct((n, d), BF16)],
        scratch_shapes=[pltpu.VMEM((3, TOP_K, tm, d // 2), jnp.uint32),
                        pltpu.SMEM((2, TOP_K, tm), jnp.int32),
                        pltpu.SemaphoreType.DMA((3,)),
                        pltpu.SemaphoreType.DMA((2,))],
        compiler_params=_params(("arbitrary",)),
        name="combine",
    )(pos.reshape(TOP_K, n // tm, tm).transpose(1, 0, 2), y_rows, gates, h,
      ws_gate, ws_up, ws_down, ln_g, ln_b)


def _alibi_slopes(n):
    return 2.0 ** (-8.0 * jnp.arange(1, n + 1, dtype=F32) / n)


def kernel(x, mem, w_in, a_sinks, g_a, g_b, w_out, ln1_g, ln1_b, wq_m, wk_m, wv_m, wo_m,
           ln2_g, ln2_b, w_router, router_bias, w_gate, w_up, w_down, ws_gate, ws_up, ws_down,
           ln3_g, ln3_b):
    b, length, d = x.shape
    n = b * length
    mlen = mem.shape[1]
    depth = w_in.shape[0]
    h = x.reshape(n, d)
    h_bf = h.astype(BF16)
    mem_bf = mem.reshape(b * mlen, d).astype(BF16)
    slopes_a = _alibi_slopes(A_Q_HEADS)
    slopes_b = _alibi_slopes(B_HEADS)

    w_in_bf = w_in.astype(BF16)
    w_in_a = w_in_bf[:, :, :A_PROJ_WIDTH]
    w_in_b = w_in_bf[:, :, A_PROJ_WIDTH:]
    w_out_bf = w_out.astype(BF16)
    wq_bf, wk_bf, wv_bf, wo_bf = (w.astype(BF16) for w in (wq_m, wk_m, wv_m, wo_m))
    wsg_bf, wsu_bf, wsd_bf = (w.astype(BF16) for w in (ws_gate, ws_up, ws_down))
    row = lambda p: p.reshape(depth, 1, p.shape[1])
    g_b3, ln1_g3, ln1_b3, ln2_g3, ln2_b3, ln3_g3, ln3_b3 = (
        row(p) for p in (g_b, ln1_g, ln1_b, ln2_g, ln2_b, ln3_g, ln3_b))

    for l in range(depth):
        proj_a = _matmul(h_bf, w_in_a, l, 512, A_PROJ_WIDTH, BF16)
        proj_b = _matmul(h_bf, w_in_b, l, 512, B_WIDTH, F32)
        mixed_a = _attn_a(proj_a.reshape(b, length, A_PROJ_WIDTH), slopes_a, a_sinks[l], g_a[l])
        out_b = _attn_b(proj_b.reshape(b, length, 3 * B_WIDTH), slopes_b)
        h, h_bf = _mix_out(mixed_a.reshape(n, A_WIDTH), out_b.reshape(n, B_WIDTH), g_b3,
                           w_out_bf, h, ln1_g3, ln1_b3, l)

        km = _matmul(mem_bf, wk_bf, l, b * mlen, d // 2, BF16)
        vm = _matmul(mem_bf, wv_bf, l, b * mlen, d // 2, BF16)
        o = _mem_attn(h_bf.reshape(b, length, d), wq_bf,
                      km.reshape(b, mlen, d), vm.reshape(b, mlen, d), l)
        h, h_bf = _proj_out(o.reshape(n, d), wo_bf, h, ln2_g3, ln2_b3, l)

        route_tm = 512
        idx_t, gate_t, mask_t, cnt = _router(h, w_router[l].T, router_bias[l], route_tm)
        tile_offsets, block_expert, used, pad_from, pad_to = _block_layout(cnt[:, :, 0], n)
        pos = _plan(mask_t, idx_t, tile_offsets, route_tm)
        xs = _dispatch(pos, h, pad_from, pad_to, used, block_expert.shape[0] * MOE_ROWS)
        y_rows = _experts(block_expert, used, xs, w_gate, w_up, w_down, l)
        h, h_bf = _combine(pos, y_rows, gate_t.T, h, wsg_bf, wsu_bf, wsd_bf, ln3_g3, ln3_b3, l)
    return h.reshape(b, length, d)
```

```python
import math

import jax
import jax.numpy as jnp
from jax import lax
from jax.experimental import pallas as pl
from jax.experimental.pallas import tpu as pltpu

F32 = jnp.float32
BF16 = jnp.bfloat16

HEAD_DIM = 64
A_Q_HEADS = 16
A_KV_HEADS = 2
A_WINDOW = 128
B_HEADS = 16
B_BRANCH_DILATIONS = (1, 4, 16)
A_WIDTH = A_Q_HEADS * HEAD_DIM
A_KV_WIDTH = A_KV_HEADS * HEAD_DIM
B_WIDTH = B_HEADS * HEAD_DIM
A_PROJ_WIDTH = A_WIDTH + 2 * A_KV_WIDTH
BLK = 128

MEM_HEADS = 4

N_EXPERTS = 64
TOP_K = 8
N_GROUPS = 8
GROUP_SIZE = N_EXPERTS // N_GROUPS
TOPK_GROUPS = 4
ROUTE_SCALE = 2.5

DEPTH = 2
ALPHA = (2.0 * DEPTH) ** 0.25
LN_EPS = 1e-5
RMS_EPS = 1e-6

LANES = 128
SUBLANES = 8
NEG = -1e30

MOE_ROWS = 256
COMBINE_ROWS = 128
VMEM_LIMIT = 56 * 1024 * 1024


def _params(sem, vmem=VMEM_LIMIT):
    return pltpu.CompilerParams(dimension_semantics=sem, vmem_limit_bytes=vmem)


def _dot(a, b):
    return jnp.dot(a, b, preferred_element_type=F32)


def _dot_nt(a, b):
    return lax.dot_general(a, b, (((1,), (1,)), ((), ())), preferred_element_type=F32)


def _layer_norm(y, g, b):
    mu = jnp.mean(y, axis=-1, keepdims=True)
    yc = y - mu
    var = jnp.mean(yc * yc, axis=-1, keepdims=True)
    return yc * lax.rsqrt(var + LN_EPS) * g + b


def _layer_spec(shape, layer, index=None, single=True):
    index = index or (0,) * len(shape)
    mode = dict(pipeline_mode=pl.Buffered(1)) if single else {}
    return pl.BlockSpec((None,) + tuple(shape), lambda *_: (layer,) + tuple(index), **mode)


def _mm_kernel(x_ref, w_ref, o_ref):
    o_ref[...] = _dot(x_ref[...], w_ref[...]).astype(o_ref.dtype)


def _matmul(x, w, layer, tm, tn, out_dtype):
    m, k = x.shape
    n = w.shape[2]
    return pl.pallas_call(
        _mm_kernel,
        grid=(n // tn, m // tm),
        in_specs=[pl.BlockSpec((tm, k), lambda j, i: (i, 0)),
                  pl.BlockSpec((None, k, tn), lambda j, i: (layer, 0, j))],
        out_specs=pl.BlockSpec((tm, tn), lambda j, i: (i, j)),
        out_shape=jax.ShapeDtypeStruct((m, n), out_dtype),
        compiler_params=_params(("arbitrary", "arbitrary")),
        name="matmul",
    )(x, w)


def _band_bias(stacked_heads, max_offset, slope_of_stack):
    rows = stacked_heads * BLK
    row = lax.broadcasted_iota(jnp.int32, (rows, 2 * BLK), 0)
    col = lax.broadcasted_iota(jnp.int32, (rows, 2 * BLK), 1)
    offset = (row & (BLK - 1)) + BLK - col
    slope = jnp.full((rows, 2 * BLK), slope_of_stack[stacked_heads - 1], F32)
    for a in range(stacked_heads - 2, -1, -1):
        slope = jnp.where(row < (a + 1) * BLK, slope_of_stack[a], slope)
    valid = jnp.logical_and(offset >= 0, offset <= max_offset)
    bias = jnp.where(valid, -slope * offset.astype(F32), NEG)
    return bias, jnp.where(col < BLK, NEG, bias)


def _roll_half(t):
    return pltpu.roll(t.astype(F32), LANES // 2, axis=1).astype(BF16)


A_STACK = 4


def _attn_a_kernel(slopes_ref, sinks_ref, q_ref, kc_ref, kp_ref, vc_ref, vp_ref, g_ref,
                   o_ref, o_scr, bias_scr):
    n = pl.program_id(1)
    tiles_per_kv = A_Q_HEADS // A_KV_HEADS // 2

    def head_of(kv, half, a):
        return 2 * (kv * tiles_per_kv + a) + half

    @pl.when(jnp.logical_and(pl.program_id(0) == 0, n == 0))
    def _():
        for kv in range(A_KV_HEADS):
            for half in range(2):
                slopes = [slopes_ref[head_of(kv, half, a)] for a in range(A_STACK)]
                bias, bias_first = _band_bias(A_STACK, A_WINDOW - 1, slopes)
                bias_scr[0, 2 * kv + half] = bias
                bias_scr[1, 2 * kv + half] = bias_first

    first = (n == 0).astype(jnp.int32)
    lane = lax.broadcasted_iota(jnp.int32, (BLK, LANES), 1)
    lo = lane < HEAD_DIM
    srow = lax.broadcasted_iota(jnp.int32, (A_STACK * BLK, 1), 0)

    k_ver = (jnp.concatenate([kp_ref[0], kc_ref[0]], axis=0),
             jnp.concatenate([_roll_half(kp_ref[0]), _roll_half(kc_ref[0])], axis=0))
    v_ver = (jnp.concatenate([vp_ref[0], vc_ref[0]], axis=0),
             jnp.concatenate([_roll_half(vp_ref[0]), _roll_half(vc_ref[0])], axis=0))

    ss = jnp.zeros((BLK, 1), F32)
    for kv in range(A_KV_HEADS):
        outs = []
        for half in range(2):
            keep = lo if half == 0 else jnp.logical_not(lo)
            parts = []
            for a in range(A_STACK):
                t = kv * tiles_per_kv + a
                qt = q_ref[0, :, t * LANES:(t + 1) * LANES] * jnp.asarray(0.125, BF16)
                parts.append(jnp.where(keep, qt, jnp.zeros_like(qt)))
            qs = jnp.concatenate(parts, axis=0)
            ver = kv ^ half
            logits = _dot_nt(qs, k_ver[ver]) + bias_scr[first, 2 * kv + half]
            sink = jnp.full((A_STACK * BLK, 1), sinks_ref[head_of(kv, half, A_STACK - 1)], F32)
            for a in range(A_STACK - 2, -1, -1):
                sink = jnp.where(srow < (a + 1) * BLK, sinks_ref[head_of(kv, half, a)], sink)
            m = jnp.maximum(jnp.max(logits, axis=1, keepdims=True), sink)
            e = jnp.exp(logits - m)
            s = jnp.sum(e, axis=1, keepdims=True) + jnp.exp(sink - m)
            acc = _dot(e.astype(BF16), v_ver[ver])
            outs.append(acc * (1.0 / s))
        for a in range(A_STACK):
            t = kv * tiles_per_kv + a
            o_tile = jnp.where(lo, outs[0][a * BLK:(a + 1) * BLK], outs[1][a * BLK:(a + 1) * BLK])
            ss = ss + jnp.sum(o_tile * o_tile, axis=1, keepdims=True)
            o_scr[:, t * LANES:(t + 1) * LANES] = o_tile
    scale = lax.rsqrt(ss * (1.0 / A_WIDTH) + RMS_EPS)
    o_ref[0] = (o_scr[...] * scale * g_ref[...]).astype(o_ref.dtype)


def _attn_a(proj_a, slopes, sinks, g_a):
    b, length, _ = proj_a.shape
    kcol = A_WIDTH // LANES
    vcol = kcol + 1
    grid_spec = pltpu.PrefetchScalarGridSpec(
        num_scalar_prefetch=2,
        grid=(b, length // BLK),
        in_specs=[
            pl.BlockSpec((1, BLK, A_WIDTH), lambda i, n, *_: (i, n, 0)),
            pl.BlockSpec((1, BLK, LANES), lambda i, n, *_: (i, n, kcol)),
            pl.BlockSpec((1, BLK, LANES), lambda i, n, *_: (i, jnp.maximum(n - 1, 0), kcol)),
            pl.BlockSpec((1, BLK, LANES), lambda i, n, *_: (i, n, vcol)),
            pl.BlockSpec((1, BLK, LANES), lambda i, n, *_: (i, jnp.maximum(n - 1, 0), vcol)),
            pl.BlockSpec((1, A_WIDTH), lambda i, n, *_: (0, 0)),
        ],
        out_specs=pl.BlockSpec((1, BLK, A_WIDTH), lambda i, n, *_: (i, n, 0)),
        scratch_shapes=[pltpu.VMEM((BLK, A_WIDTH), F32),
                        pltpu.VMEM((2, 2 * A_KV_HEADS, A_STACK * BLK, 2 * BLK), F32)],
    )
    return pl.pallas_call(
        _attn_a_kernel,
        grid_spec=grid_spec,
        out_shape=jax.ShapeDtypeStruct((b, length, A_WIDTH), BF16),
        compiler_params=_params(("arbitrary", "arbitrary")),
        name="attn_a",
    )(slopes, sinks, proj_a, proj_a, proj_a, proj_a, proj_a, g_a.reshape(1, A_WIDTH))


def _attn_b_kernel(slopes_ref, q_ref, k_ref, v_ref, o_ref, acc_scr, m_scr, s_scr, bias_scr,
                   l0_scr, l1_scr, m0_scr, m1_scr):
    hp = pl.program_id(1)
    length = q_ref.shape[1]
    lane = lax.broadcasted_iota(jnp.int32, (BLK, LANES), 1)
    lo = lane < HEAD_DIM
    qrow = lax.broadcasted_iota(jnp.int32, (2 * BLK, LANES), 0)
    qlane = lax.broadcasted_iota(jnp.int32, (2 * BLK, LANES), 1)
    qmask = (qrow >= BLK) == (qlane >= HEAD_DIM)

    for branch, dil in enumerate(B_BRANCH_DILATIONS):
        slopes = [slopes_ref[2 * hp] * float(dil), slopes_ref[2 * hp + 1] * float(dil)]
        bias, bias_first = _band_bias(2, BLK, slopes)
        bias_scr[branch, 0] = bias
        bias_scr[branch, 1] = bias_first

    for branch, dil in enumerate(B_BRANCH_DILATIONS):
        blocks_per_class = length // (dil * BLK)

        def rows(start, dil=dil):
            if dil == 1:
                return pl.ds(start, BLK)
            return pl.ds(start, BLK, stride=dil)

        def block_rows(it, dil=dil, blocks_per_class=blocks_per_class, rows=rows):
            res = it // blocks_per_class
            blk = it % blocks_per_class
            cur = rows(res + dil * BLK * blk)
            prev = rows(res + dil * BLK * jnp.maximum(blk - 1, 0))
            return cur, prev, jnp.where(blk == 0, 1, 0)

        def logits_stage(it, l_ref, m_ref, branch=branch, block_rows=block_rows):
            cur, prev, first = block_rows(it)
            q2 = q_ref[0, cur, :] * 0.125
            qs = jnp.concatenate([q2, q2], axis=0)
            qs = jnp.where(qmask, qs, 0.0).astype(BF16)
            k2 = jnp.concatenate([k_ref[0, prev, :], k_ref[0, cur, :]], axis=0).astype(BF16)
            logits = _dot_nt(qs, k2) + bias_scr[branch, first]
            l_ref[...] = logits
            m_ref[...] = jnp.broadcast_to(jnp.max(logits, axis=1, keepdims=True), m_ref.shape)

        def value_stage(it, l_ref, m_ref, branch=branch, block_rows=block_rows):
            cur, prev, _ = block_rows(it)
            v2 = jnp.concatenate([v_ref[0, prev, :], v_ref[0, cur, :]], axis=0).astype(BF16)
            m = m_ref[...]
            e = jnp.exp(l_ref[...] - jnp.concatenate([m, m], axis=1))
            s = jnp.sum(e, axis=1, keepdims=True)
            acc = _dot(e.astype(BF16), v2)
            m_t = jnp.where(lo, m[:BLK], m[BLK:])
            s_t = jnp.where(lo, s[:BLK], s[BLK:])
            acc_t = jnp.where(lo, acc[:BLK], acc[BLK:])
            if branch == 0:
                m_scr[cur, :] = m_t
                s_scr[cur, :] = s_t
                acc_scr[cur, :] = acc_t
            else:
                m_old = m_scr[cur, :]
                s_old = s_scr[cur, :]
                acc_old = acc_scr[cur, :]
                m_new = jnp.maximum(m_old, m_t)
                w_old = jnp.exp(m_old - m_new)
                w_blk = jnp.exp(m_t - m_new)
                m_scr[cur, :] = m_new
                s_scr[cur, :] = s_old * w_old + s_t * w_blk
                acc_scr[cur, :] = acc_old * w_old + acc_t * w_blk

        n_blocks = length // BLK
        logits_stage(0, l0_scr, m0_scr)

        def body(j, carry, logits_stage=logits_stage, value_stage=value_stage, n_blocks=n_blocks):
            it = 2 * j
            logits_stage(it + 1, l1_scr, m1_scr)
            value_stage(it, l0_scr, m0_scr)
            logits_stage(jnp.minimum(it + 2, n_blocks - 1), l0_scr, m0_scr)
            value_stage(it + 1, l1_scr, m1_scr)
            return carry

        lax.fori_loop(0, n_blocks // 2, body, 0)

    o_ref[0] = acc_scr[...] / s_scr[...]


def _attn_b(proj_b, slopes):
    b, length, _ = proj_b.shape
    pairs = B_WIDTH // LANES
    grid_spec = pltpu.PrefetchScalarGridSpec(
        num_scalar_prefetch=1,
        grid=(b, pairs),
        in_specs=[
            pl.BlockSpec((1, length, LANES), lambda i, p, *_: (i, 0, p)),
            pl.BlockSpec((1, length, LANES), lambda i, p, *_: (i, 0, pairs + p)),
            pl.BlockSpec((1, length, LANES), lambda i, p, *_: (i, 0, 2 * pairs + p)),
        ],
        out_specs=pl.BlockSpec((1, length, LANES), lambda i, p, *_: (i, 0, p)),
        scratch_shapes=[pltpu.VMEM((length, LANES), F32)] * 3
        + [pltpu.VMEM((len(B_BRANCH_DILATIONS), 2, 2 * BLK, 2 * BLK), F32)]
        + [pltpu.VMEM((2 * BLK, 2 * BLK), F32)] * 2
        + [pltpu.VMEM((2 * BLK, LANES), F32)] * 2,
    )
    return pl.pallas_call(
        _attn_b_kernel,
        grid_spec=grid_spec,
        out_shape=jax.ShapeDtypeStruct((b, length, B_WIDTH), F32),
        compiler_params=_params(("arbitrary", "arbitrary")),
        name="attn_b",
    )(slopes, proj_b, proj_b, proj_b)


def _mix_out_kernel(xa_ref, xb_ref, gb_ref, wa_ref, wb_ref, h_ref, g_ref, b_ref,
                    o_ref, obf_ref):
    xb = xb_ref[...]
    scale = lax.rsqrt(jnp.mean(xb * xb, axis=-1, keepdims=True) + RMS_EPS)
    xb = (xb * scale * gb_ref[...]).astype(BF16)
    y = _dot(xa_ref[...], wa_ref[...]) + _dot(xb, wb_ref[...])
    out = _layer_norm(ALPHA * h_ref[...] + y, g_ref[...], b_ref[...])
    o_ref[...] = out
    obf_ref[...] = out.astype(BF16)


def _proj_out_kernel(x_ref, w_ref, h_ref, g_ref, b_ref, o_ref, obf_ref):
    y = _dot(x_ref[...], w_ref[...])
    out = _layer_norm(ALPHA * h_ref[...] + y, g_ref[...], b_ref[...])
    o_ref[...] = out
    obf_ref[...] = out.astype(BF16)


def _mix_out(xa, xb, g_b, w_out, h, ln_g, ln_b, layer, tm=512):
    n, d = h.shape
    ka = xa.shape[1]
    kb = xb.shape[1]
    return pl.pallas_call(
        _mix_out_kernel,
        grid=(n // tm,),
        in_specs=[pl.BlockSpec((tm, ka), lambda i: (i, 0)),
                  pl.BlockSpec((tm, kb), lambda i: (i, 0)),
                  _layer_spec((1, kb), layer),
                  _layer_spec((ka, d), layer, (0, 0)),
                  _layer_spec((kb, d), layer, (ka // kb, 0)),
                  pl.BlockSpec((tm, d), lambda i: (i, 0)),
                  _layer_spec((1, d), layer),
                  _layer_spec((1, d), layer)],
        out_specs=[pl.BlockSpec((tm, d), lambda i: (i, 0)),
                   pl.BlockSpec((tm, d), lambda i: (i, 0))],
        out_shape=[jax.ShapeDtypeStruct((n, d), F32), jax.ShapeDtypeStruct((n, d), BF16)],
        compiler_params=_params(("arbitrary",)),
        name="mix_out",
    )(xa, xb, g_b, w_out, w_out, h, ln_g, ln_b)


def _proj_out(x, w, h, ln_g, ln_b, layer, tm=512):
    n, d = h.shape
    k = x.shape[1]
    return pl.pallas_call(
        _proj_out_kernel,
        grid=(n // tm,),
        in_specs=[pl.BlockSpec((tm, k), lambda i: (i, 0)),
                  _layer_spec((k, d), layer),
                  pl.BlockSpec((tm, d), lambda i: (i, 0)),
                  _layer_spec((1, d), layer),
                  _layer_spec((1, d), layer)],
        out_specs=[pl.BlockSpec((tm, d), lambda i: (i, 0)),
                   pl.BlockSpec((tm, d), lambda i: (i, 0))],
        out_shape=[jax.ShapeDtypeStruct((n, d), F32), jax.ShapeDtypeStruct((n, d), BF16)],
        compiler_params=_params(("arbitrary",)),
        name="proj_out",
    )(x, w, h, ln_g, ln_b)


def _mem_attn_kernel(x_ref, wq_ref, k_ref, v_ref, o_ref):
    d = x_ref.shape[2]
    hd = d // MEM_HEADS
    q = _dot(x_ref[0], wq_ref[...]) * (1.0 / math.sqrt(hd))
    for head in range(MEM_HEADS):
        cols = slice(head * hd, (head + 1) * hd)
        logits = _dot_nt(q[:, cols].astype(BF16), k_ref[0, :, cols])
        m = jnp.max(logits, axis=1, keepdims=True)
        e = jnp.exp(logits - m)
        s = jnp.sum(e, axis=1, keepdims=True)
        o = _dot(e.astype(BF16), v_ref[0, :, cols]) / s
        o_ref[0, :, cols] = o.astype(o_ref.dtype)


def _mem_attn(x, wq, km, vm, layer, tm=512):
    b, length, d = x.shape
    mlen = km.shape[1]
    return pl.pallas_call(
        _mem_attn_kernel,
        grid=(b, length // tm),
        in_specs=[pl.BlockSpec((1, tm, d), lambda i, j: (i, j, 0)),
                  _layer_spec((d, d), layer),
                  pl.BlockSpec((1, mlen, d), lambda i, j: (i, 0, 0)),
                  pl.BlockSpec((1, mlen, d), lambda i, j: (i, 0, 0))],
        out_specs=pl.BlockSpec((1, tm, d), lambda i, j: (i, j, 0)),
        out_shape=jax.ShapeDtypeStruct((b, length, d), BF16),
        compiler_params=_params(("arbitrary", "arbitrary")),
        name="mem_attn",
    )(x, wq, km, vm)


def _split_bf16(a):
    hi = a.astype(BF16)
    lo = (a - hi.astype(F32)).astype(BF16)
    return hi, lo


def _router_kernel(x_ref, wt_ref, bias_ref, idx_ref, gate_ref, mask_ref, cnt_ref):
    tm = x_ref.shape[0]
    xh, xl = _split_bf16(x_ref[...])
    wh, wl = _split_bf16(wt_ref[...])
    logits = _dot_nt(wh, xh) + (_dot_nt(wh, xl) + _dot_nt(wl, xh))
    scores = jax.nn.sigmoid(logits)
    sel = scores + bias_ref[...]

    member = lax.broadcasted_iota(jnp.int32, (GROUP_SIZE, tm), 0)
    group_scores = []
    for g in range(N_GROUPS):
        v = sel[g * GROUP_SIZE:(g + 1) * GROUP_SIZE, :]
        m1 = jnp.max(v, axis=0, keepdims=True)
        first = jnp.min(jnp.where(v == m1, member, GROUP_SIZE), axis=0, keepdims=True)
        m2 = jnp.max(jnp.where(member == first, -jnp.inf, v), axis=0, keepdims=True)
        group_scores.append(m1 + m2)
    masked = []
    for g in range(N_GROUPS):
        beaten_by = jnp.zeros((1, tm), jnp.int32)
        for o in range(N_GROUPS):
            if o == g:
                continue
            wins = group_scores[o] > group_scores[g]
            if o < g:
                wins = jnp.logical_or(wins, group_scores[o] == group_scores[g])
            beaten_by = beaten_by + wins.astype(jnp.int32)
        keep = beaten_by < TOPK_GROUPS
        masked.append(jnp.where(keep, sel[g * GROUP_SIZE:(g + 1) * GROUP_SIZE, :], -jnp.inf))
    cand = jnp.concatenate(masked, axis=0)

    expert = lax.broadcasted_iota(jnp.int32, (N_EXPERTS, tm), 0)
    beaten_by = jnp.zeros((N_EXPERTS, tm), jnp.int32)
    for o in range(N_EXPERTS):
        other = cand[o:o + 1, :]
        wins = jnp.logical_or(other > cand, jnp.logical_and(other == cand, expert > o))
        beaten_by = beaten_by + wins.astype(jnp.int32)
    chosen = beaten_by < TOP_K
    chosen_i = chosen.astype(jnp.int32)
    mask_ref[...] = chosen_i
    cnt_ref[...] = jnp.broadcast_to(jnp.sum(chosen_i, axis=1, keepdims=True), cnt_ref.shape)

    picked = jnp.where(chosen, scores, 0.0)
    gate = picked / jnp.sum(picked, axis=0, keepdims=True) * ROUTE_SCALE

    running = jnp.zeros((1, tm), jnp.int32)
    slots = []
    for g in range(N_GROUPS):
        part = chosen_i[g * GROUP_SIZE:(g + 1) * GROUP_SIZE, :]
        rows_ = []
        for r in range(GROUP_SIZE):
            rows_.append(running)
            running = running + part[r:r + 1, :]
        slots.append(jnp.concatenate(rows_, axis=0))
    slot = jnp.concatenate(slots, axis=0)
    idx_rows = []
    gate_rows = []
    for k in range(TOP_K):
        hit = jnp.logical_and(chosen, slot == k)
        idx_rows.append(jnp.sum(jnp.where(hit, expert, 0), axis=0, keepdims=True))
        gate_rows.append(jnp.sum(jnp.where(hit, gate, 0.0), axis=0, keepdims=True))
    idx_ref[...] = jnp.concatenate(idx_rows, axis=0)
    gate_ref[...] = jnp.concatenate(gate_rows, axis=0)


def _router(x, w_router_t, bias, tm=512):
    n, d = x.shape
    return pl.pallas_call(
        _router_kernel,
        grid=(n // tm,),
        in_specs=[pl.BlockSpec((tm, d), lambda i: (i, 0)),
                  pl.BlockSpec((N_EXPERTS, d), lambda i: (0, 0)),
                  pl.BlockSpec((N_EXPERTS, 1), lambda i: (0, 0))],
        out_specs=[pl.BlockSpec((TOP_K, tm), lambda i: (0, i)),
                   pl.BlockSpec((TOP_K, tm), lambda i: (0, i)),
                   pl.BlockSpec((N_EXPERTS, tm), lambda i: (0, i)),
                   pl.BlockSpec((None, N_EXPERTS, LANES), lambda i: (i, 0, 0))],
        out_shape=[jax.ShapeDtypeStruct((TOP_K, n), jnp.int32),
                   jax.ShapeDtypeStruct((TOP_K, n), F32),
                   jax.ShapeDtypeStruct((N_EXPERTS, n), jnp.int32),
                   jax.ShapeDtypeStruct((n // tm, N_EXPERTS, LANES), jnp.int32)],
        compiler_params=_params(("arbitrary",)),
        name="router",
    )(x, w_router_t, bias.reshape(N_EXPERTS, 1))


def _plan_kernel(mask_ref, idx_ref, off_ref, pos_ref):
    tm = mask_ref.shape[1]
    before = (lax.broadcasted_iota(jnp.int32, (tm, tm), 0)
              < lax.broadcasted_iota(jnp.int32, (tm, tm), 1)).astype(BF16)
    rank = _dot(mask_ref[...].astype(BF16), before).astype(jnp.int32)
    row_of = off_ref[...] + rank
    expert = lax.broadcasted_iota(jnp.int32, (N_EXPERTS, tm), 0)
    rows_ = []
    for k in range(TOP_K):
        hit = expert == idx_ref[k:k + 1, :]
        rows_.append(jnp.sum(jnp.where(hit, row_of, 0), axis=0, keepdims=True))
    pos_ref[...] = jnp.concatenate(rows_, axis=0)


def _plan(mask_t, idx_t, tile_offsets, tm):
    n = mask_t.shape[1]
    return pl.pallas_call(
        _plan_kernel,
        grid=(n // tm,),
        in_specs=[pl.BlockSpec((N_EXPERTS, tm), lambda i: (0, i)),
                  pl.BlockSpec((TOP_K, tm), lambda i: (0, i)),
                  pl.BlockSpec((None, N_EXPERTS, 1), lambda i: (i, 0, 0))],
        out_specs=pl.BlockSpec((TOP_K, tm), lambda i: (0, i)),
        out_shape=jax.ShapeDtypeStruct((TOP_K, n), jnp.int32),
        compiler_params=_params(("arbitrary",)),
        name="plan",
    )(mask_t, idx_t, tile_offsets)


def _block_layout(counts_per_tile, n):
    rows = MOE_ROWS
    nb = (n * TOP_K) // rows + N_EXPERTS
    counts = jnp.sum(counts_per_tile, axis=0)
    padded = (counts + rows - 1) // rows * rows
    padded_ends = jnp.cumsum(padded)
    padded_starts = padded_ends - padded
    tile_offsets = padded_starts[None, :] + jnp.cumsum(counts_per_tile, axis=0) - counts_per_tile
    used = (padded_ends[-1] // rows).astype(jnp.int32)
    block_start = jnp.arange(nb, dtype=jnp.int32) * rows
    block_expert = jnp.minimum(
        jnp.sum((padded_ends[None, :] <= block_start[:, None]).astype(jnp.int32), axis=1),
        N_EXPERTS - 1)
    last = block_expert[jnp.maximum(used - 1, 0)]
    block_expert = jnp.where(jnp.arange(nb) < used, block_expert, last)
    pad_from = (padded_starts + counts).astype(jnp.int32)
    return (tile_offsets.astype(jnp.int32)[:, :, None], block_expert, used.reshape(1),
            pad_from, padded_ends.astype(jnp.int32))


DISPATCH_ROWS = 128


def _pack_rows(x):
    k = x.shape[1] // 2
    lo = pltpu.bitcast(x[:, :k].astype(BF16).astype(F32), jnp.uint32) >> 16
    hi = pltpu.bitcast(x[:, k:].astype(BF16).astype(F32), jnp.uint32) & jnp.uint32(0xFFFF0000)
    return lo | hi


def _unpack_rows(w):
    lo = pltpu.bitcast(w << 16, F32)
    hi = pltpu.bitcast(w & jnp.uint32(0xFFFF0000), F32)
    return lo, hi


def _dispatch_kernel(pad_from_ref, pad_to_ref, used_ref, pos_hbm, x_ref, xs_hbm, pos_smem,
                     xbuf, zeros, row_sem, pos_sem, pad_sem):
    i = pl.program_id(0)
    steps = pl.num_programs(0)
    tm = pos_smem.shape[2]
    slot = i % 3

    def wait_rows(s):
        for _ in range(TOP_K):
            pltpu.make_async_copy(xbuf.at[s], xs_hbm.at[pl.ds(0, tm)], row_sem.at[s]).wait()

    def pos_copy(blk):
        return pltpu.make_async_copy(pos_hbm.at[blk], pos_smem.at[blk % 2], pos_sem.at[blk % 2])

    @pl.when(i == 0)
    def _():
        pos_copy(0).start()
        zeros[...] = jnp.zeros_like(zeros)
        block = zeros.shape[0]

        def tail_copy(blk):
            start = pl.multiple_of(blk * block, block)
            return pltpu.make_async_copy(zeros, xs_hbm.at[pl.ds(start, block)], pad_sem)

        def tail_start(blk, c):
            tail_copy(blk).start()
            return c

        def tail_wait(blk, c):
            tail_copy(blk).wait()
            return c

        lax.fori_loop(used_ref[0], xs_hbm.shape[0] // block, tail_start, 0)
        lax.fori_loop(used_ref[0], xs_hbm.shape[0] // block, tail_wait, 0)

        def per_expert(e, carry):
            def pad_copy(r):
                return pltpu.make_async_copy(zeros.at[pl.ds(0, 1)], xs_hbm.at[pl.ds(r, 1)],
                                             pad_sem)

            def start(r, c):
                pad_copy(r).start()
                return c

            def wait(r, c):
                pad_copy(r).wait()
                return c

            lax.fori_loop(pad_from_ref[e], pad_to_ref[e], start, 0)
            lax.fori_loop(pad_from_ref[e], pad_to_ref[e], wait, 0)
            return carry

        lax.fori_loop(0, N_EXPERTS, per_expert, 0)

    @pl.when(i >= 3)
    def _():
        wait_rows(slot)

    @pl.when(i + 1 < steps)
    def _():
        pos_copy(i + 1).start()

    xbuf[slot] = _pack_rows(x_ref[...])
    pos_copy(i).wait()
    pslot = i % 2

    for r in range(tm):
        for k in range(TOP_K):
            pltpu.make_async_copy(xbuf.at[slot, pl.ds(r, 1)],
                                  xs_hbm.at[pl.ds(pos_smem[pslot, k, r], 1)],
                                  row_sem.at[slot]).start()

    @pl.when(i == steps - 1)
    def _():
        wait_rows(slot)

        @pl.when(steps > 1)
        def _():
            wait_rows((i + 2) % 3)

        @pl.when(steps > 2)
        def _():
            wait_rows((i + 1) % 3)


def _dispatch(pos, x, pad_from, pad_to, used, total_rows):
    n, d = x.shape
    tm = DISPATCH_ROWS
    grid_spec = pltpu.PrefetchScalarGridSpec(
        num_scalar_prefetch=3,
        grid=(n // tm,),
        in_specs=[pl.BlockSpec(memory_space=pl.ANY),
                  pl.BlockSpec((tm, d), lambda i, *_: (i, 0))],
        out_specs=pl.BlockSpec(memory_space=pl.ANY),
        scratch_shapes=[pltpu.SMEM((2, TOP_K, tm), jnp.int32),
                        pltpu.VMEM((3, tm, d // 2), jnp.uint32),
                        pltpu.VMEM((MOE_ROWS, d // 2), jnp.uint32),
                        pltpu.SemaphoreType.DMA((3,)),
                        pltpu.SemaphoreType.DMA((2,)),
                        pltpu.SemaphoreType.DMA],
    )
    return pl.pallas_call(
        _dispatch_kernel,
        grid_spec=grid_spec,
        out_shape=jax.ShapeDtypeStruct((total_rows, d // 2), jnp.uint32),
        compiler_params=_params(("arbitrary",)),
        name="dispatch",
    )(pad_from, pad_to, used, pos.reshape(TOP_K, n // tm, tm).transpose(1, 0, 2), x)


def _experts_kernel(be_ref, used_ref, x_ref, wg_ref, wu_ref, wd_ref, y_ref,
                    wg_bf, wu_bf, wd_bf):
    i = pl.program_id(0)
    used = used_ref[0]

    @pl.when(i < used)
    def _():
        changed = jnp.logical_or(i == 0, be_ref[jnp.maximum(i - 1, 0)] != be_ref[i])

        @pl.when(changed)
        def _():
            wg_bf[...] = wg_ref[...].astype(BF16)
            wu_bf[...] = wu_ref[...].astype(BF16)
            wd_bf[...] = wd_ref[...].astype(BF16)

        half = x_ref.shape[1]
        x_lo, x_hi = (t.astype(BF16) for t in _unpack_rows(x_ref[...]))
        gate = _dot(x_lo, wg_bf[:half, :]) + _dot(x_hi, wg_bf[half:, :])
        up = _dot(x_lo, wu_bf[:half, :]) + _dot(x_hi, wu_bf[half:, :])
        hid = jax.nn.silu(gate) * up
        y_ref[...] = _pack_rows(_dot(hid.astype(BF16), wd_bf[...]))

    @pl.when(i >= used)
    def _():
        y_ref[...] = jnp.zeros_like(y_ref)


def _experts(block_expert, used, xs, w_gate, w_up, w_down, layer):
    nb = block_expert.shape[0]
    rows = MOE_ROWS
    d = w_gate.shape[2]
    de = w_gate.shape[3]
    grid_spec = pltpu.PrefetchScalarGridSpec(
        num_scalar_prefetch=2,
        grid=(nb,),
        in_specs=[
            pl.BlockSpec((rows, d // 2), lambda i, be, u: (jnp.minimum(i, u[0] - 1), 0)),
            pl.BlockSpec((None, None, d, de), lambda i, be, u: (layer, be[i], 0, 0)),
            pl.BlockSpec((None, None, d, de), lambda i, be, u: (layer, be[i], 0, 0)),
            pl.BlockSpec((None, None, de, d), lambda i, be, u: (layer, be[i], 0, 0)),
        ],
        out_specs=pl.BlockSpec((rows, d // 2), lambda i, be, u: (i, 0)),
        scratch_shapes=[
            pltpu.VMEM((d, de), BF16),
            pltpu.VMEM((d, de), BF16),
            pltpu.VMEM((de, d), BF16),
        ],
    )
    return pl.pallas_call(
        _experts_kernel,
        grid_spec=grid_spec,
        out_shape=jax.ShapeDtypeStruct((nb * rows, d // 2), jnp.uint32),
        compiler_params=_params(("arbitrary",)),
        name="experts",
    )(block_expert, used, xs, w_gate, w_up, w_down)


def _combine_kernel(pos_hbm, y_hbm, gate_ref, h_ref, wsg_ref, wsu_ref, wsd_ref, g_ref, b_ref,
                    o_ref, obf_ref, ybuf, pos_smem, row_sem, pos_sem):
    i = pl.program_id(0)
    steps = pl.num_programs(0)
    tm = h_ref.shape[0]
    slot = i % 3
    last = steps - 1

    def pos_copy(t):
        return pltpu.make_async_copy(pos_hbm.at[jnp.minimum(t, last)], pos_smem.at[t % 2],
                                     pos_sem.at[t % 2])

    def row_copy(t, k, r):
        return pltpu.make_async_copy(y_hbm.at[pl.ds(pos_smem[t % 2, k, r], 1)],
                                     ybuf.at[t % 3, k, pl.ds(r, 1)], row_sem.at[t % 3])

    def wait_rows(t):
        for k in range(TOP_K):
            pltpu.make_async_copy(y_hbm.at[pl.ds(0, tm)], ybuf.at[t % 3, k],
                                  row_sem.at[t % 3]).wait()

    @pl.when(i == 0)
    def _():
        for t in range(2):
            pos_copy(t).start()
            pos_copy(t).wait()

            def issue(r, carry, t=t):
                for k in range(TOP_K):
                    row_copy(t, k, r).start()
                return carry
            lax.fori_loop(0, tm, issue, 0)

    wait_rows(i)

    @pl.when(i > 0)
    def _():
        pos_copy(i + 2).wait()

    @pl.when(i == 0)
    def _():
        pos_copy(2).start()
        pos_copy(2).wait()

    for r in range(tm):
        for k in range(TOP_K):
            row_copy(i + 2, k, r).start()
    pos_copy(i + 3).start()

    h = h_ref[...]
    x = h.astype(BF16)
    hid = jax.nn.silu(_dot(x, wsg_ref[...])) * _dot(x, wsu_ref[...])
    y = ALPHA * h + _dot(hid.astype(BF16), wsd_ref[...])

    gates = gate_ref[...]
    half = ybuf.shape[3]
    routed_lo = jnp.zeros((tm, half), F32)
    routed_hi = jnp.zeros((tm, half), F32)
    for k in range(TOP_K):
        lo, hi = _unpack_rows(ybuf[slot, k])
        routed_lo = routed_lo + lo * gates[:, k:k + 1]
        routed_hi = routed_hi + hi * gates[:, k:k + 1]
    y = y + jnp.concatenate([routed_lo, routed_hi], axis=1)
    out = _layer_norm(y, g_ref[...], b_ref[...])
    o_ref[...] = out
    obf_ref[...] = out.astype(BF16)

    @pl.when(i == last)
    def _():
        wait_rows(i + 1)
        wait_rows(i + 2)
        pos_copy(i + 3).wait()


def _combine(pos, y_rows, gates, h, ws_gate, ws_up, ws_down, ln_g, ln_b, layer):
    n, d = h.shape
    tm = COMBINE_ROWS
    ds = ws_gate.shape[2]
    return pl.pallas_call(
        _combine_kernel,
        grid=(n // tm,),
        in_specs=[pl.BlockSpec(memory_space=pl.ANY),
                  pl.BlockSpec(memory_space=pl.ANY),
                  pl.BlockSpec((tm, TOP_K), lambda i: (i, 0)),
                  pl.BlockSpec((tm, d), lambda i: (i, 0)),
                  _layer_spec((d, ds), layer),
                  _layer_spec((d, ds), layer),
                  _layer_spec((ds, d), layer),
                  _layer_spec((1, d), layer),
                  _layer_spec((1, d), layer)],
        out_specs=[pl.BlockSpec((tm, d), lambda i: (i, 0)),
                   pl.BlockSpec((tm, d), lambda i: (i, 0))],
        out_shape=[jax.ShapeDtypeStruct((n, d), F32), jax.ShapeDtypeStruct((n, d), BF16)],
        scratch_shapes=[pltpu.VMEM((3, TOP_K, tm, d // 2), jnp.uint32),
                        pltpu.SMEM((2, TOP_K, tm), jnp.int32),
                        pltpu.SemaphoreType.DMA((3,)),
                        pltpu.SemaphoreType.DMA((2,))],
        compiler_params=_params(("arbitrary",)),
        name="combine",
    )(pos.reshape(TOP_K, n // tm, tm).transpose(1, 0, 2), y_rows, gates, h,
      ws_gate, ws_up, ws_down, ln_g, ln_b)


def _alibi_slopes(n):
    return 2.0 ** (-8.0 * jnp.arange(1, n + 1, dtype=F32) / n)


def kernel(x, mem, w_in, a_sinks, g_a, g_b, w_out, ln1_g, ln1_b, wq_m, wk_m, wv_m, wo_m,
           ln2_g, ln2_b, w_router, router_bias, w_gate, w_up, w_down, ws_gate, ws_up, ws_down,
           ln3_g, ln3_b):
    b, length, d = x.shape
    n = b * length
    mlen = mem.shape[1]
    depth = w_in.shape[0]
    h = x.reshape(n, d)
    h_bf = h.astype(BF16)
    mem_bf = mem.reshape(b * mlen, d).astype(BF16)
    slopes_a = _alibi_slopes(A_Q_HEADS)
    slopes_b = _alibi_slopes(B_HEADS)

    w_in_bf = w_in.astype(BF16)
    w_in_a = w_in_bf[:, :, :A_PROJ_WIDTH]
    w_in_b = w_in_bf[:, :, A_PROJ_WIDTH:]
    w_out_bf = w_out.astype(BF16)
    wq_bf, wk_bf, wv_bf, wo_bf = (w.astype(BF16) for w in (wq_m, wk_m, wv_m, wo_m))
    wsg_bf, wsu_bf, wsd_bf = (w.astype(BF16) for w in (ws_gate, ws_up, ws_down))
    row = lambda p: p.reshape(depth, 1, p.shape[1])
    g_b3, ln1_g3, ln1_b3, ln2_g3, ln2_b3, ln3_g3, ln3_b3 = (
        row(p) for p in (g_b, ln1_g, ln1_b, ln2_g, ln2_b, ln3_g, ln3_b))

    for l in range(depth):
        proj_a = _matmul(h_bf, w_in_a, l, 512, A_PROJ_WIDTH, BF16)
        proj_b = _matmul(h_bf, w_in_b, l, 512, B_WIDTH, F32)
        mixed_a = _attn_a(proj_a.reshape(b, length, A_PROJ_WIDTH), slopes_a, a_sinks[l], g_a[l])
        out_b = _attn_b(proj_b.reshape(b, length, 3 * B_WIDTH), slopes_b)
        h, h_bf = _mix_out(mixed_a.reshape(n, A_WIDTH), out_b.reshape(n, B_WIDTH), g_b3,
                           w_out_bf, h, ln1_g3, ln1_b3, l)

        km = _matmul(mem_bf, wk_bf, l, b * mlen, d // 2, BF16)
        vm = _matmul(mem_bf, wv_bf, l, b * mlen, d // 2, BF16)
        o = _mem_attn(h_bf.reshape(b, length, d), wq_bf,
                      km.reshape(b, mlen, d), vm.reshape(b, mlen, d), l)
        h, h_bf = _proj_out(o.reshape(n, d), wo_bf, h, ln2_g3, ln2_b3, l)

        route_tm = 512
        idx_t, gate_t, mask_t, cnt = _router(h, w_router[l].T, router_bias[l], route_tm)
        tile_offsets, block_expert, used, pad_from, pad_to = _block_layout(cnt[:, :, 0], n)
        pos = _plan(mask_t, idx_t, tile_offsets, route_tm)
        xs = _dispatch(pos, h, pad_from, pad_to, used, block_expert.shape[0] * MOE_ROWS)
        y_rows = _experts(block_expert, used, xs, w_gate, w_up, w_down, l)
        h, h_bf = _combine(pos, y_rows, gate_t.T, h, wsg_bf, wsu_bf, wsd_bf, ln3_g3, ln3_b3, l)
    return h.reshape(b, length, d)
```

```python
import math

import jax
import jax.numpy as jnp
from jax import lax
from jax.experimental import pallas as pl
from jax.experimental.pallas import tpu as pltpu

F32 = jnp.float32
BF16 = jnp.bfloat16

HEAD_DIM = 64
A_Q_HEADS = 16
A_KV_HEADS = 2
A_WINDOW = 128
B_HEADS = 16
B_BRANCH_DILATIONS = (16, 4, 1)
A_WIDTH = A_Q_HEADS * HEAD_DIM
A_KV_WIDTH = A_KV_HEADS * HEAD_DIM
B_WIDTH = B_HEADS * HEAD_DIM
A_PROJ_WIDTH = A_WIDTH + 2 * A_KV_WIDTH
BLK = 128

MEM_HEADS = 4

N_EXPERTS = 64
TOP_K = 8
N_GROUPS = 8
GROUP_SIZE = N_EXPERTS // N_GROUPS
TOPK_GROUPS = 4
ROUTE_SCALE = 2.5

DEPTH = 2
ALPHA = (2.0 * DEPTH) ** 0.25
LN_EPS = 1e-5
RMS_EPS = 1e-6

LANES = 128
SUBLANES = 8
NEG = -1e30

MOE_ROWS = 512
COMBINE_ROWS = 128
VMEM_LIMIT = 56 * 1024 * 1024


def _params(sem, vmem=VMEM_LIMIT):
    return pltpu.CompilerParams(dimension_semantics=sem, vmem_limit_bytes=vmem)


def _dot(a, b):
    return jnp.dot(a, b, preferred_element_type=F32)


def _dot_nt(a, b):
    return lax.dot_general(a, b, (((1,), (1,)), ((), ())), preferred_element_type=F32)


def _layer_norm(y, g, b):
    mu = jnp.mean(y, axis=-1, keepdims=True)
    yc = y - mu
    var = jnp.mean(yc * yc, axis=-1, keepdims=True)
    return yc * lax.rsqrt(var + LN_EPS) * g + b


def _layer_spec(shape, layer, index=None, single=True):
    index = index or (0,) * len(shape)
    mode = dict(pipeline_mode=pl.Buffered(1)) if single else {}
    return pl.BlockSpec((None,) + tuple(shape), lambda *_: (layer,) + tuple(index), **mode)


def _mm_kernel(x_ref, w_ref, o_ref):
    o_ref[...] = _dot(x_ref[...], w_ref[...]).astype(o_ref.dtype)


def _matmul(x, w, layer, tm, tn, out_dtype):
    m, k = x.shape
    n = w.shape[2]
    return pl.pallas_call(
        _mm_kernel,
        grid=(n // tn, m // tm),
        in_specs=[pl.BlockSpec((tm, k), lambda j, i: (i, 0)),
                  pl.BlockSpec((None, k, tn), lambda j, i: (layer, 0, j))],
        out_specs=pl.BlockSpec((tm, tn), lambda j, i: (i, j)),
        out_shape=jax.ShapeDtypeStruct((m, n), out_dtype),
        compiler_params=_params(("arbitrary", "arbitrary")),
        name="matmul",
    )(x, w)


def _band_bias(stacked_heads, max_offset, slope_of_stack):
    rows = stacked_heads * BLK
    row = lax.broadcasted_iota(jnp.int32, (rows, 2 * BLK), 0)
    col = lax.broadcasted_iota(jnp.int32, (rows, 2 * BLK), 1)
    offset = (row & (BLK - 1)) + BLK - col
    slope = jnp.full((rows, 2 * BLK), slope_of_stack[stacked_heads - 1], F32)
    for a in range(stacked_heads - 2, -1, -1):
        slope = jnp.where(row < (a + 1) * BLK, slope_of_stack[a], slope)
    valid = jnp.logical_and(offset >= 0, offset <= max_offset)
    bias = jnp.where(valid, -slope * offset.astype(F32), NEG)
    return bias, jnp.where(col < BLK, NEG, bias)


def _roll_half(t):
    return pltpu.roll(t.astype(F32), LANES // 2, axis=1).astype(BF16)


A_STACK = 4


def _attn_a_kernel(slopes_ref, sinks_ref, q_ref, kc_ref, kp_ref, vc_ref, vp_ref, g_ref,
                   o_ref, o_scr, bias_scr):
    n = pl.program_id(1)
    tiles_per_kv = A_Q_HEADS // A_KV_HEADS // 2

    def head_of(kv, half, a):
        return 2 * (kv * tiles_per_kv + a) + half

    @pl.when(jnp.logical_and(pl.program_id(0) == 0, n == 0))
    def _():
        for kv in range(A_KV_HEADS):
            for half in range(2):
                slopes = [slopes_ref[head_of(kv, half, a)] for a in range(A_STACK)]
                bias, bias_first = _band_bias(A_STACK, A_WINDOW - 1, slopes)
                bias_scr[0, 2 * kv + half] = bias
                bias_scr[1, 2 * kv + half] = bias_first

    first = (n == 0).astype(jnp.int32)
    lane = lax.broadcasted_iota(jnp.int32, (BLK, LANES), 1)
    lo = lane < HEAD_DIM
    srow = lax.broadcasted_iota(jnp.int32, (A_STACK * BLK, 1), 0)

    k_ver = (jnp.concatenate([kp_ref[0], kc_ref[0]], axis=0),
             jnp.concatenate([_roll_half(kp_ref[0]), _roll_half(kc_ref[0])], axis=0))
    v_ver = (jnp.concatenate([vp_ref[0], vc_ref[0]], axis=0),
             jnp.concatenate([_roll_half(vp_ref[0]), _roll_half(vc_ref[0])], axis=0))

    ss = jnp.zeros((BLK, 1), F32)
    for kv in range(A_KV_HEADS):
        outs = []
        for half in range(2):
            keep = lo if half == 0 else jnp.logical_not(lo)
            parts = []
            for a in range(A_STACK):
                t = kv * tiles_per_kv + a
                qt = q_ref[0, :, t * LANES:(t + 1) * LANES] * jnp.asarray(0.125, BF16)
                parts.append(jnp.where(keep, qt, jnp.zeros_like(qt)))
            qs = jnp.concatenate(parts, axis=0)
            ver = kv ^ half
            logits = _dot_nt(qs, k_ver[ver]) + bias_scr[first, 2 * kv + half]
            sink = jnp.full((A_STACK * BLK, 1), sinks_ref[head_of(kv, half, A_STACK - 1)], F32)
            for a in range(A_STACK - 2, -1, -1):
                sink = jnp.where(srow < (a + 1) * BLK, sinks_ref[head_of(kv, half, a)], sink)
            m = jnp.maximum(jnp.max(logits, axis=1, keepdims=True), sink)
            e = jnp.exp(logits - m)
            s = jnp.sum(e, axis=1, keepdims=True) + jnp.exp(sink - m)
            acc = _dot(e.astype(BF16), v_ver[ver])
            outs.append(acc * (1.0 / s))
        for a in range(A_STACK):
            t = kv * tiles_per_kv + a
            o_tile = jnp.where(lo, outs[0][a * BLK:(a + 1) * BLK], outs[1][a * BLK:(a + 1) * BLK])
            ss = ss + jnp.sum(o_tile * o_tile, axis=1, keepdims=True)
            o_scr[:, t * LANES:(t + 1) * LANES] = o_tile
    scale = lax.rsqrt(ss * (1.0 / A_WIDTH) + RMS_EPS)
    o_ref[0] = (o_scr[...] * scale * g_ref[...]).astype(o_ref.dtype)


def _attn_a(proj_a, slopes, sinks, g_a):
    b, length, _ = proj_a.shape
    kcol = A_WIDTH // LANES
    vcol = kcol + 1
    grid_spec = pltpu.PrefetchScalarGridSpec(
        num_scalar_prefetch=2,
        grid=(b, length // BLK),
        in_specs=[
            pl.BlockSpec((1, BLK, A_WIDTH), lambda i, n, *_: (i, n, 0)),
            pl.BlockSpec((1, BLK, LANES), lambda i, n, *_: (i, n, kcol)),
            pl.BlockSpec((1, BLK, LANES), lambda i, n, *_: (i, jnp.maximum(n - 1, 0), kcol)),
            pl.BlockSpec((1, BLK, LANES), lambda i, n, *_: (i, n, vcol)),
            pl.BlockSpec((1, BLK, LANES), lambda i, n, *_: (i, jnp.maximum(n - 1, 0), vcol)),
            pl.BlockSpec((1, A_WIDTH), lambda i, n, *_: (0, 0)),
        ],
        out_specs=pl.BlockSpec((1, BLK, A_WIDTH), lambda i, n, *_: (i, n, 0)),
        scratch_shapes=[pltpu.VMEM((BLK, A_WIDTH), F32),
                        pltpu.VMEM((2, 2 * A_KV_HEADS, A_STACK * BLK, 2 * BLK), F32)],
    )
    return pl.pallas_call(
        _attn_a_kernel,
        grid_spec=grid_spec,
        out_shape=jax.ShapeDtypeStruct((b, length, A_WIDTH), BF16),
        compiler_params=_params(("arbitrary", "arbitrary")),
        name="attn_a",
    )(slopes, sinks, proj_a, proj_a, proj_a, proj_a, proj_a, g_a.reshape(1, A_WIDTH))


B_PIPE_DEPTH = 4


def _attn_b_kernel(slopes_ref, q_ref, k_ref, v_ref, o_ref, acc_scr, m_scr, s_scr, bias_scr,
                   *stage_scrs):
    l_scrs = stage_scrs[:B_PIPE_DEPTH]
    mx_scrs = stage_scrs[B_PIPE_DEPTH:]
    hp = pl.program_id(1)
    length = q_ref.shape[1]
    lane = lax.broadcasted_iota(jnp.int32, (BLK, LANES), 1)
    lo = lane < HEAD_DIM
    qrow = lax.broadcasted_iota(jnp.int32, (2 * BLK, LANES), 0)
    qlane = lax.broadcasted_iota(jnp.int32, (2 * BLK, LANES), 1)
    qmask = (qrow >= BLK) == (qlane >= HEAD_DIM)

    for branch, dil in enumerate(B_BRANCH_DILATIONS):
        slopes = [slopes_ref[2 * hp] * float(dil), slopes_ref[2 * hp + 1] * float(dil)]
        bias, bias_first = _band_bias(2, BLK, slopes)
        bias_scr[branch, 0] = bias
        bias_scr[branch, 1] = bias_first

    for branch, dil in enumerate(B_BRANCH_DILATIONS):
        blocks_per_class = length // (dil * BLK)

        def rows(start, dil=dil):
            if dil == 1:
                return pl.ds(start, BLK)
            return pl.ds(start, BLK, stride=dil)

        def block_rows(it, dil=dil, blocks_per_class=blocks_per_class, rows=rows):
            res = it // blocks_per_class
            blk = it % blocks_per_class
            cur = rows(res + dil * BLK * blk)
            prev = rows(res + dil * BLK * jnp.maximum(blk - 1, 0))
            return cur, prev, jnp.where(blk == 0, 1, 0)

        def logits_stage(it, l_ref, m_ref, branch=branch, block_rows=block_rows):
            cur, prev, first = block_rows(it)
            q2 = q_ref[0, cur, :] * 0.125
            qs = jnp.concatenate([q2, q2], axis=0)
            qs = jnp.where(qmask, qs, 0.0).astype(BF16)
            k2 = jnp.concatenate([k_ref[0, prev, :], k_ref[0, cur, :]], axis=0).astype(BF16)
            logits = _dot_nt(qs, k2) + bias_scr[branch, first]
            l_ref[...] = logits
            m_ref[...] = jnp.broadcast_to(jnp.max(logits, axis=1, keepdims=True), m_ref.shape)

        def value_stage(it, l_ref, m_ref, branch=branch, block_rows=block_rows):
            cur, prev, _ = block_rows(it)
            v2 = jnp.concatenate([v_ref[0, prev, :], v_ref[0, cur, :]], axis=0).astype(BF16)
            m = m_ref[...]
            e = jnp.exp(l_ref[...] - jnp.concatenate([m, m], axis=1))
            s = jnp.sum(e, axis=1, keepdims=True)
            acc = _dot(e.astype(BF16), v2)
            m_t = jnp.where(lo, m[:BLK], m[BLK:])
            s_t = jnp.where(lo, s[:BLK], s[BLK:])
            acc_t = jnp.where(lo, acc[:BLK], acc[BLK:])
            if branch == 0:
                m_scr[cur, :] = m_t
                s_scr[cur, :] = s_t
                acc_scr[cur, :] = acc_t
            else:
                m_old = m_scr[cur, :]
                s_old = s_scr[cur, :]
                acc_old = acc_scr[cur, :]
                m_new = jnp.maximum(m_old, m_t)
                w_old = jnp.exp(m_old - m_new)
                w_blk = jnp.exp(m_t - m_new)
                m_scr[cur, :] = m_new
                s_scr[cur, :] = s_old * w_old + s_t * w_blk
                acc_scr[cur, :] = acc_old * w_old + acc_t * w_blk

        n_blocks = length // BLK
        depth = len(l_scrs)
        for u in range(depth):
            logits_stage(u, l_scrs[u], mx_scrs[u])

        def body(j, carry, logits_stage=logits_stage, value_stage=value_stage, depth=depth):
            for u in range(depth):
                value_stage(depth * j + u, l_scrs[u], mx_scrs[u])
            for u in range(depth):
                logits_stage(depth * (j + 1) + u, l_scrs[u], mx_scrs[u])
            return carry

        lax.fori_loop(0, n_blocks // depth - 1, body, 0)
        for u in range(depth):
            value_stage(n_blocks - depth + u, l_scrs[u], mx_scrs[u])

    o_ref[0] = acc_scr[...] / s_scr[...]


def _attn_b(proj_b, slopes):
    b, length, _ = proj_b.shape
    pairs = B_WIDTH // LANES
    grid_spec = pltpu.PrefetchScalarGridSpec(
        num_scalar_prefetch=1,
        grid=(b, pairs),
        in_specs=[
            pl.BlockSpec((1, length, LANES), lambda i, p, *_: (i, 0, p)),
            pl.BlockSpec((1, length, LANES), lambda i, p, *_: (i, 0, pairs + p)),
            pl.BlockSpec((1, length, LANES), lambda i, p, *_: (i, 0, 2 * pairs + p)),
        ],
        out_specs=pl.BlockSpec((1, length, LANES), lambda i, p, *_: (i, 0, p)),
        scratch_shapes=[pltpu.VMEM((length, LANES), F32)] * 3
        + [pltpu.VMEM((len(B_BRANCH_DILATIONS), 2, 2 * BLK, 2 * BLK), F32)]
        + [pltpu.VMEM((2 * BLK, 2 * BLK), F32)] * B_PIPE_DEPTH
        + [pltpu.VMEM((2 * BLK, LANES), F32)] * B_PIPE_DEPTH,
    )
    return pl.pallas_call(
        _attn_b_kernel,
        grid_spec=grid_spec,
        out_shape=jax.ShapeDtypeStruct((b, length, B_WIDTH), F32),
        compiler_params=_params(("arbitrary", "arbitrary")),
        name="attn_b",
    )(slopes, proj_b, proj_b, proj_b)


def _mix_out_kernel(xa_ref, xb_ref, gb_ref, wa_ref, wb_ref, h_ref, g_ref, b_ref,
                    o_ref, obf_ref):
    xb = xb_ref[...]
    scale = lax.rsqrt(jnp.mean(xb * xb, axis=-1, keepdims=True) + RMS_EPS)
    xb = (xb * scale * gb_ref[...]).astype(BF16)
    y = _dot(xa_ref[...], wa_ref[...]) + _dot(xb, wb_ref[...])
    out = _layer_norm(ALPHA * h_ref[...] + y, g_ref[...], b_ref[...])
    o_ref[...] = out
    obf_ref[...] = out.astype(BF16)


def _proj_out_kernel(x_ref, w_ref, h_ref, g_ref, b_ref, o_ref, obf_ref):
    y = _dot(x_ref[...], w_ref[...])
    out = _layer_norm(ALPHA * h_ref[...] + y, g_ref[...], b_ref[...])
    o_ref[...] = out
    obf_ref[...] = out.astype(BF16)


def _mix_out(xa, xb, g_b, w_out, h, ln_g, ln_b, layer, tm=512):
    n, d = h.shape
    ka = xa.shape[1]
    kb = xb.shape[1]
    return pl.pallas_call(
        _mix_out_kernel,
        grid=(n // tm,),
        in_specs=[pl.BlockSpec((tm, ka), lambda i: (i, 0)),
                  pl.BlockSpec((tm, kb), lambda i: (i, 0)),
                  _layer_spec((1, kb), layer),
                  _layer_spec((ka, d), layer, (0, 0)),
                  _layer_spec((kb, d), layer, (ka // kb, 0)),
                  pl.BlockSpec((tm, d), lambda i: (i, 0)),
                  _layer_spec((1, d), layer),
                  _layer_spec((1, d), layer)],
        out_specs=[pl.BlockSpec((tm, d), lambda i: (i, 0)),
                   pl.BlockSpec((tm, d), lambda i: (i, 0))],
        out_shape=[jax.ShapeDtypeStruct((n, d), F32), jax.ShapeDtypeStruct((n, d), BF16)],
        compiler_params=_params(("arbitrary",)),
        name="mix_out",
    )(xa, xb, g_b, w_out, w_out, h, ln_g, ln_b)


def _proj_out(x, w, h, ln_g, ln_b, layer, tm=512):
    n, d = h.shape
    k = x.shape[1]
    return pl.pallas_call(
        _proj_out_kernel,
        grid=(n // tm,),
        in_specs=[pl.BlockSpec((tm, k), lambda i: (i, 0)),
                  _layer_spec((k, d), layer),
                  pl.BlockSpec((tm, d), lambda i: (i, 0)),
                  _layer_spec((1, d), layer),
                  _layer_spec((1, d), layer)],
        out_specs=[pl.BlockSpec((tm, d), lambda i: (i, 0)),
                   pl.BlockSpec((tm, d), lambda i: (i, 0))],
        out_shape=[jax.ShapeDtypeStruct((n, d), F32), jax.ShapeDtypeStruct((n, d), BF16)],
        compiler_params=_params(("arbitrary",)),
        name="proj_out",
    )(x, w, h, ln_g, ln_b)


def _mem_attn_kernel(x_ref, wq_ref, k_ref, v_ref, o_ref):
    d = x_ref.shape[2]
    hd = d // MEM_HEADS
    q = _dot(x_ref[0], wq_ref[...]) * (1.0 / math.sqrt(hd))
    for head in range(MEM_HEADS):
        cols = slice(head * hd, (head + 1) * hd)
        logits = _dot_nt(q[:, cols].astype(BF16), k_ref[0, :, cols])
        m = jnp.max(logits, axis=1, keepdims=True)
        e = jnp.exp(logits - m)
        s = jnp.sum(e, axis=1, keepdims=True)
        o = _dot(e.astype(BF16), v_ref[0, :, cols]) / s
        o_ref[0, :, cols] = o.astype(o_ref.dtype)


def _mem_attn(x, wq, km, vm, layer, tm=512):
    b, length, d = x.shape
    mlen = km.shape[1]
    return pl.pallas_call(
        _mem_attn_kernel,
        grid=(b, length // tm),
        in_specs=[pl.BlockSpec((1, tm, d), lambda i, j: (i, j, 0)),
                  _layer_spec((d, d), layer),
                  pl.BlockSpec((1, mlen, d), lambda i, j: (i, 0, 0)),
                  pl.BlockSpec((1, mlen, d), lambda i, j: (i, 0, 0))],
        out_specs=pl.BlockSpec((1, tm, d), lambda i, j: (i, j, 0)),
        out_shape=jax.ShapeDtypeStruct((b, length, d), BF16),
        compiler_params=_params(("arbitrary", "arbitrary")),
        name="mem_attn",
    )(x, wq, km, vm)


def _split_bf16(a):
    hi = a.astype(BF16)
    lo = (a - hi.astype(F32)).astype(BF16)
    return hi, lo


def _router_kernel(x_ref, wt_ref, bias_ref, idx_ref, gate_ref, mask_ref, cnt_ref):
    tm = x_ref.shape[0]
    xh, xl = _split_bf16(x_ref[...])
    wh, wl = _split_bf16(wt_ref[...])
    logits = _dot_nt(wh, xh) + (_dot_nt(wh, xl) + _dot_nt(wl, xh))
    scores = jax.nn.sigmoid(logits)
    sel = scores + bias_ref[...]

    member = lax.broadcasted_iota(jnp.int32, (GROUP_SIZE, tm), 0)
    group_scores = []
    for g in range(N_GROUPS):
        v = sel[g * GROUP_SIZE:(g + 1) * GROUP_SIZE, :]
        m1 = jnp.max(v, axis=0, keepdims=True)
        first = jnp.min(jnp.where(v == m1, member, GROUP_SIZE), axis=0, keepdims=True)
        m2 = jnp.max(jnp.where(member == first, -jnp.inf, v), axis=0, keepdims=True)
        group_scores.append(m1 + m2)
    masked = []
    for g in range(N_GROUPS):
        beaten_by = jnp.zeros((1, tm), jnp.int32)
        for o in range(N_GROUPS):
            if o == g:
                continue
            wins = group_scores[o] > group_scores[g]
            if o < g:
                wins = jnp.logical_or(wins, group_scores[o] == group_scores[g])
            beaten_by = beaten_by + wins.astype(jnp.int32)
        keep = beaten_by < TOPK_GROUPS
        masked.append(jnp.where(keep, sel[g * GROUP_SIZE:(g + 1) * GROUP_SIZE, :], -jnp.inf))
    cand = jnp.concatenate(masked, axis=0)

    expert = lax.broadcasted_iota(jnp.int32, (N_EXPERTS, tm), 0)
    beaten_by = jnp.zeros((N_EXPERTS, tm), jnp.int32)
    for o in range(N_EXPERTS):
        other = cand[o:o + 1, :]
        wins = jnp.logical_or(other > cand, jnp.logical_and(other == cand, expert > o))
        beaten_by = beaten_by + wins.astype(jnp.int32)
    chosen = beaten_by < TOP_K
    chosen_i = chosen.astype(jnp.int32)
    mask_ref[...] = chosen_i
    cnt_ref[...] = jnp.broadcast_to(jnp.sum(chosen_i, axis=1, keepdims=True), cnt_ref.shape)

    picked = jnp.where(chosen, scores, 0.0)
    gate = picked / jnp.sum(picked, axis=0, keepdims=True) * ROUTE_SCALE

    running = jnp.zeros((1, tm), jnp.int32)
    slots = []
    for g in range(N_GROUPS):
        part = chosen_i[g * GROUP_SIZE:(g + 1) * GROUP_SIZE, :]
        rows_ = []
        for r in range(GROUP_SIZE):
            rows_.append(running)
            running = running + part[r:r + 1, :]
        slots.append(jnp.concatenate(rows_, axis=0))
    slot = jnp.concatenate(slots, axis=0)
    idx_rows = []
    gate_rows = []
    for k in range(TOP_K):
        hit = jnp.logical_and(chosen, slot == k)
        idx_rows.append(jnp.sum(jnp.where(hit, expert, 0), axis=0, keepdims=True))
        gate_rows.append(jnp.sum(jnp.where(hit, gate, 0.0), axis=0, keepdims=True))
    idx_ref[...] = jnp.concatenate(idx_rows, axis=0)
    gate_ref[...] = jnp.concatenate(gate_rows, axis=0)


def _router(x, w_router_t, bias, tm=512):
    n, d = x.shape
    return pl.pallas_call(
        _router_kernel,
        grid=(n // tm,),
        in_specs=[pl.BlockSpec((tm, d), lambda i: (i, 0)),
                  pl.BlockSpec((N_EXPERTS, d), lambda i: (0, 0)),
                  pl.BlockSpec((N_EXPERTS, 1), lambda i: (0, 0))],
        out_specs=[pl.BlockSpec((TOP_K, tm), lambda i: (0, i)),
                   pl.BlockSpec((TOP_K, tm), lambda i: (0, i)),
                   pl.BlockSpec((N_EXPERTS, tm), lambda i: (0, i)),
                   pl.BlockSpec((None, N_EXPERTS, LANES), lambda i: (i, 0, 0))],
        out_shape=[jax.ShapeDtypeStruct((TOP_K, n), jnp.int32),
                   jax.ShapeDtypeStruct((TOP_K, n), F32),
                   jax.ShapeDtypeStruct((N_EXPERTS, n), jnp.int32),
                   jax.ShapeDtypeStruct((n // tm, N_EXPERTS, LANES), jnp.int32)],
        compiler_params=_params(("arbitrary",)),
        name="router",
    )(x, w_router_t, bias.reshape(N_EXPERTS, 1))


def _plan_kernel(mask_ref, idx_ref, off_ref, pos_ref):
    tm = mask_ref.shape[1]
    before = (lax.broadcasted_iota(jnp.int32, (tm, tm), 0)
              < lax.broadcasted_iota(jnp.int32, (tm, tm), 1)).astype(BF16)
    rank = _dot(mask_ref[...].astype(BF16), before).astype(jnp.int32)
    row_of = off_ref[...] + rank
    expert = lax.broadcasted_iota(jnp.int32, (N_EXPERTS, tm), 0)
    rows_ = []
    for k in range(TOP_K):
        hit = expert == idx_ref[k:k + 1, :]
        rows_.append(jnp.sum(jnp.where(hit, row_of, 0), axis=0, keepdims=True))
    pos_ref[...] = jnp.concatenate(rows_, axis=0)


def _plan(mask_t, idx_t, tile_offsets, tm):
    n = mask_t.shape[1]
    return pl.pallas_call(
        _plan_kernel,
        grid=(n // tm,),
        in_specs=[pl.BlockSpec((N_EXPERTS, tm), lambda i: (0, i)),
                  pl.BlockSpec((TOP_K, tm), lambda i: (0, i)),
                  pl.BlockSpec((None, N_EXPERTS, 1), lambda i: (i, 0, 0))],
        out_specs=pl.BlockSpec((TOP_K, tm), lambda i: (0, i)),
        out_shape=jax.ShapeDtypeStruct((TOP_K, n), jnp.int32),
        compiler_params=_params(("arbitrary",)),
        name="plan",
    )(mask_t, idx_t, tile_offsets)


def _block_layout(counts_per_tile, n):
    rows = MOE_ROWS
    nb = (n * TOP_K) // rows + N_EXPERTS
    counts = jnp.sum(counts_per_tile, axis=0)
    padded = (counts + rows - 1) // rows * rows
    padded_ends = jnp.cumsum(padded)
    padded_starts = padded_ends - padded
    tile_offsets = padded_starts[None, :] + jnp.cumsum(counts_per_tile, axis=0) - counts_per_tile
    used = (padded_ends[-1] // rows).astype(jnp.int32)
    block_start = jnp.arange(nb, dtype=jnp.int32) * rows
    block_expert = jnp.minimum(
        jnp.sum((padded_ends[None, :] <= block_start[:, None]).astype(jnp.int32), axis=1),
        N_EXPERTS - 1)
    last = block_expert[jnp.maximum(used - 1, 0)]
    block_expert = jnp.where(jnp.arange(nb) < used, block_expert, last)
    pad_from = (padded_starts + counts).astype(jnp.int32)
    return (tile_offsets.astype(jnp.int32)[:, :, None], block_expert, used.reshape(1),
            pad_from, padded_ends.astype(jnp.int32))


DISPATCH_ROWS = 128


def _pack_rows(x):
    k = x.shape[1] // 2
    lo = pltpu.bitcast(x[:, :k].astype(BF16).astype(F32), jnp.uint32) >> 16
    hi = pltpu.bitcast(x[:, k:].astype(BF16).astype(F32), jnp.uint32) & jnp.uint32(0xFFFF0000)
    return lo | hi


def _unpack_rows(w):
    lo = pltpu.bitcast(w << 16, F32)
    hi = pltpu.bitcast(w & jnp.uint32(0xFFFF0000), F32)
    return lo, hi


def _dispatch_kernel(pad_from_ref, pad_to_ref, used_ref, pos_hbm, x_ref, xs_hbm, pos_smem,
                     xbuf, zeros, row_sem, pos_sem, pad_sem):
    i = pl.program_id(0)
    steps = pl.num_programs(0)
    tm = pos_smem.shape[2]
    slot = i % 3

    def wait_rows(s):
        for _ in range(TOP_K):
            pltpu.make_async_copy(xbuf.at[s], xs_hbm.at[pl.ds(0, tm)], row_sem.at[s]).wait()

    def pos_copy(blk):
        return pltpu.make_async_copy(pos_hbm.at[blk], pos_smem.at[blk % 2], pos_sem.at[blk % 2])

    @pl.when(i == 0)
    def _():
        pos_copy(0).start()
        zeros[...] = jnp.zeros_like(zeros)
        block = zeros.shape[0]

        def tail_copy(blk):
            start = pl.multiple_of(blk * block, block)
            return pltpu.make_async_copy(zeros, xs_hbm.at[pl.ds(start, block)], pad_sem)

        def tail_start(blk, c):
            tail_copy(blk).start()
            return c

        def tail_wait(blk, c):
            tail_copy(blk).wait()
            return c

        lax.fori_loop(used_ref[0], xs_hbm.shape[0] // block, tail_start, 0)
        lax.fori_loop(used_ref[0], xs_hbm.shape[0] // block, tail_wait, 0)

        def per_expert(e, carry):
            def pad_copy(r):
                return pltpu.make_async_copy(zeros.at[pl.ds(0, 1)], xs_hbm.at[pl.ds(r, 1)],
                                             pad_sem)

            def start(r, c):
                pad_copy(r).start()
                return c

            def wait(r, c):
                pad_copy(r).wait()
                return c

            lax.fori_loop(pad_from_ref[e], pad_to_ref[e], start, 0)
            lax.fori_loop(pad_from_ref[e], pad_to_ref[e], wait, 0)
            return carry

        lax.fori_loop(0, N_EXPERTS, per_expert, 0)

    @pl.when(i >= 3)
    def _():
        wait_rows(slot)

    @pl.when(i + 1 < steps)
    def _():
        pos_copy(i + 1).start()

    xbuf[slot] = _pack_rows(x_ref[...])
    pos_copy(i).wait()
    pslot = i % 2

    for r in range(tm):
        for k in range(TOP_K):
            pltpu.make_async_copy(xbuf.at[slot, pl.ds(r, 1)],
                                  xs_hbm.at[pl.ds(pos_smem[pslot, k, r], 1)],
                                  row_sem.at[slot]).start()

    @pl.when(i == steps - 1)
    def _():
        wait_rows(slot)

        @pl.when(steps > 1)
        def _():
            wait_rows((i + 2) % 3)

        @pl.when(steps > 2)
        def _():
            wait_rows((i + 1) % 3)


def _dispatch(pos, x, pad_from, pad_to, used, total_rows):
    n, d = x.shape
    tm = DISPATCH_ROWS
    grid_spec = pltpu.PrefetchScalarGridSpec(
        num_scalar_prefetch=3,
        grid=(n // tm,),
        in_specs=[pl.BlockSpec(memory_space=pl.ANY),
                  pl.BlockSpec((tm, d), lambda i, *_: (i, 0))],
        out_specs=pl.BlockSpec(memory_space=pl.ANY),
        scratch_shapes=[pltpu.SMEM((2, TOP_K, tm), jnp.int32),
                        pltpu.VMEM((3, tm, d // 2), jnp.uint32),
                        pltpu.VMEM((MOE_ROWS, d // 2), jnp.uint32),
                        pltpu.SemaphoreType.DMA((3,)),
                        pltpu.SemaphoreType.DMA((2,)),
                        pltpu.SemaphoreType.DMA],
    )
    return pl.pallas_call(
        _dispatch_kernel,
        grid_spec=grid_spec,
        out_shape=jax.ShapeDtypeStruct((total_rows, d // 2), jnp.uint32),
        compiler_params=_params(("arbitrary",)),
        name="dispatch",
    )(pad_from, pad_to, used, pos.reshape(TOP_K, n // tm, tm).transpose(1, 0, 2), x)


def _experts_kernel(be_ref, used_ref, x_ref, wg_ref, wu_ref, wd_ref, y_ref,
                    wg_bf, wu_bf, wd_bf):
    i = pl.program_id(0)
    used = used_ref[0]

    @pl.when(i < used)
    def _():
        changed = jnp.logical_or(i == 0, be_ref[jnp.maximum(i - 1, 0)] != be_ref[i])

        @pl.when(changed)
        def _():
            wg_bf[...] = wg_ref[...].astype(BF16)
            wu_bf[...] = wu_ref[...].astype(BF16)
            wd_bf[...] = wd_ref[...].astype(BF16)

        half = x_ref.shape[1]
        x_lo, x_hi = (t.astype(BF16) for t in _unpack_rows(x_ref[...]))
        gate = _dot(x_lo, wg_bf[:half, :]) + _dot(x_hi, wg_bf[half:, :])
        up = _dot(x_lo, wu_bf[:half, :]) + _dot(x_hi, wu_bf[half:, :])
        hid = jax.nn.silu(gate) * up
        y_ref[...] = _pack_rows(_dot(hid.astype(BF16), wd_bf[...]))

    @pl.when(i >= used)
    def _():
        y_ref[...] = jnp.zeros_like(y_ref)


def _experts(block_expert, used, xs, w_gate, w_up, w_down, layer):
    nb = block_expert.shape[0]
    rows = MOE_ROWS
    d = w_gate.shape[2]
    de = w_gate.shape[3]
    grid_spec = pltpu.PrefetchScalarGridSpec(
        num_scalar_prefetch=2,
        grid=(nb,),
        in_specs=[
            pl.BlockSpec((rows, d // 2), lambda i, be, u: (jnp.minimum(i, u[0] - 1), 0)),
            pl.BlockSpec((None, None, d, de), lambda i, be, u: (layer, be[i], 0, 0)),
            pl.BlockSpec((None, None, d, de), lambda i, be, u: (layer, be[i], 0, 0)),
            pl.BlockSpec((None, None, de, d), lambda i, be, u: (layer, be[i], 0, 0)),
        ],
        out_specs=pl.BlockSpec((rows, d // 2), lambda i, be, u: (i, 0)),
        scratch_shapes=[
            pltpu.VMEM((d, de), BF16),
            pltpu.VMEM((d, de), BF16),
            pltpu.VMEM((de, d), BF16),
        ],
    )
    return pl.pallas_call(
        _experts_kernel,
        grid_spec=grid_spec,
        out_shape=jax.ShapeDtypeStruct((nb * rows, d // 2), jnp.uint32),
        compiler_params=_params(("arbitrary",)),
        name="experts",
    )(block_expert, used, xs, w_gate, w_up, w_down)


def _combine_kernel(pos_hbm, y_hbm, gate_ref, h_ref, wsg_ref, wsu_ref, wsd_ref, g_ref, b_ref,
                    o_ref, obf_ref, ybuf, pos_smem, row_sem, pos_sem):
    i = pl.program_id(0)
    steps = pl.num_programs(0)
    tm = h_ref.shape[0]
    slot = i % 3
    last = steps - 1

    def pos_copy(t):
        return pltpu.make_async_copy(pos_hbm.at[jnp.minimum(t, last)], pos_smem.at[t % 2],
                                     pos_sem.at[t % 2])

    def row_copy(t, k, r):
        return pltpu.make_async_copy(y_hbm.at[pl.ds(pos_smem[t % 2, k, r], 1)],
                                     ybuf.at[t % 3, k, pl.ds(r, 1)], row_sem.at[t % 3])

    def wait_rows(t):
        for k in range(TOP_K):
            pltpu.make_async_copy(y_hbm.at[pl.ds(0, tm)], ybuf.at[t % 3, k],
                                  row_sem.at[t % 3]).wait()

    @pl.when(i == 0)
    def _():
        for t in range(2):
            pos_copy(t).start()
            pos_copy(t).wait()

            def issue(r, carry, t=t):
                for k in range(TOP_K):
                    row_copy(t, k, r).start()
                return carry
            lax.fori_loop(0, tm, issue, 0)

    wait_rows(i)

    @pl.when(i > 0)
    def _():
        pos_copy(i + 2).wait()

    @pl.when(i == 0)
    def _():
        pos_copy(2).start()
        pos_copy(2).wait()

    for r in range(tm):
        for k in range(TOP_K):
            row_copy(i + 2, k, r).start()
    pos_copy(i + 3).start()

    h = h_ref[...]
    x = h.astype(BF16)
    hid = jax.nn.silu(_dot(x, wsg_ref[...])) * _dot(x, wsu_ref[...])
    y = ALPHA * h + _dot(hid.astype(BF16), wsd_ref[...])

    gates = gate_ref[...]
    half = ybuf.shape[3]
    routed_lo = jnp.zeros((tm, half), F32)
    routed_hi = jnp.zeros((tm, half), F32)
    for k in range(TOP_K):
        lo, hi = _unpack_rows(ybuf[slot, k])
        routed_lo = routed_lo + lo * gates[:, k:k + 1]
        routed_hi = routed_hi + hi * gates[:, k:k + 1]
    y = y + jnp.concatenate([routed_lo, routed_hi], axis=1)
    out = _layer_norm(y, g_ref[...], b_ref[...])
    o_ref[...] = out
    obf_ref[...] = out.astype(BF16)

    @pl.when(i == last)
    def _():
        wait_rows(i + 1)
        wait_rows(i + 2)
        pos_copy(i + 3).wait()


def _combine(pos, y_rows, gates, h, ws_gate, ws_up, ws_down, ln_g, ln_b, layer):
    n, d = h.shape
    tm = COMBINE_ROWS
    ds = ws_gate.shape[2]
    return pl.pallas_call(
        _combine_kernel,
        grid=(n // tm,),
        in_specs=[pl.BlockSpec(memory_space=pl.ANY),
                  pl.BlockSpec(memory_space=pl.ANY),
                  pl.BlockSpec((tm, TOP_K), lambda i: (i, 0)),
                  pl.BlockSpec((tm, d), lambda i: (i, 0)),
                  _layer_spec((d, ds), layer),
                  _layer_spec((d, ds), layer),
                  _layer_spec((ds, d), layer),
                  _layer_spec((1, d), layer),
                  _layer_spec((1, d), layer)],
        out_specs=[pl.BlockSpec((tm, d), lambda i: (i, 0)),
                   pl.BlockSpec((tm, d), lambda i: (i, 0))],
        out_shape=[jax.ShapeDtypeStruct((n, d), F32), jax.ShapeDtypeStruct((n, d), BF16)],
        scratch_shapes=[pltpu.VMEM((3, TOP_K, tm, d // 2), jnp.uint32),
                        pltpu.SMEM((2, TOP_K, tm), jnp.int32),
                        pltpu.SemaphoreType.DMA((3,)),
                        pltpu.SemaphoreType.DMA((2,))],
        compiler_params=_params(("arbitrary",)),
        name="combine",
    )(pos.reshape(TOP_K, n // tm, tm).transpose(1, 0, 2), y_rows, gates, h,
      ws_gate, ws_up, ws_down, ln_g, ln_b)


def _alibi_slopes(n):
    return 2.0 ** (-8.0 * jnp.arange(1, n + 1, dtype=F32) / n)


def kernel(x, mem, w_in, a_sinks, g_a, g_b, w_out, ln1_g, ln1_b, wq_m, wk_m, wv_m, wo_m,
           ln2_g, ln2_b, w_router, router_bias, w_gate, w_up, w_down, ws_gate, ws_up, ws_down,
           ln3_g, ln3_b):
    b, length, d = x.shape
    n = b * length
    mlen = mem.shape[1]
    depth = w_in.shape[0]
    h = x.reshape(n, d)
    h_bf = h.astype(BF16)
    mem_bf = mem.reshape(b * mlen, d).astype(BF16)
    slopes_a = _alibi_slopes(A_Q_HEADS)
    slopes_b = _alibi_slopes(B_HEADS)

    w_in_bf = w_in.astype(BF16)
    w_in_a = w_in_bf[:, :, :A_PROJ_WIDTH]
    w_in_b = w_in_bf[:, :, A_PROJ_WIDTH:]
    w_out_bf = w_out.astype(BF16)
    wq_bf, wk_bf, wv_bf, wo_bf = (w.astype(BF16) for w in (wq_m, wk_m, wv_m, wo_m))
    wsg_bf, wsu_bf, wsd_bf = (w.astype(BF16) for w in (ws_gate, ws_up, ws_down))
    row = lambda p: p.reshape(depth, 1, p.shape[1])
    g_b3, ln1_g3, ln1_b3, ln2_g3, ln2_b3, ln3_g3, ln3_b3 = (
        row(p) for p in (g_b, ln1_g, ln1_b, ln2_g, ln2_b, ln3_g, ln3_b))

    for l in range(depth):
        proj_a = _matmul(h_bf, w_in_a, l, 512, A_PROJ_WIDTH, BF16)
        proj_b = _matmul(h_bf, w_in_b, l, 512, B_WIDTH, F32)
        mixed_a = _attn_a(proj_a.reshape(b, length, A_PROJ_WIDTH), slopes_a, a_sinks[l], g_a[l])
        out_b = _attn_b(proj_b.reshape(b, length, 3 * B_WIDTH), slopes_b)
        h, h_bf = _mix_out(mixed_a.reshape(n, A_WIDTH), out_b.reshape(n, B_WIDTH), g_b3,
                           w_out_bf, h, ln1_g3, ln1_b3, l)

        km = _matmul(mem_bf, wk_bf, l, b * mlen, d // 2, BF16)
        vm = _matmul(mem_bf, wv_bf, l, b * mlen, d // 2, BF16)
        o = _mem_attn(h_bf.reshape(b, length, d), wq_bf,
                      km.reshape(b, mlen, d), vm.reshape(b, mlen, d), l)
        h, h_bf = _proj_out(o.reshape(n, d), wo_bf, h, ln2_g3, ln2_b3, l)

        route_tm = 512
        idx_t, gate_t, mask_t, cnt = _router(h, w_router[l].T, router_bias[l], route_tm)
        tile_offsets, block_expert, used, pad_from, pad_to = _block_layout(cnt[:, :, 0], n)
        pos = _plan(mask_t, idx_t, tile_offsets, route_tm)
        xs = _dispatch(pos, h, pad_from, pad_to, used, block_expert.shape[0] * MOE_ROWS)
        y_rows = _experts(block_expert, used, xs, w_gate, w_up, w_down, l)
        h, h_bf = _combine(pos, y_rows, gate_t.T, h, wsg_bf, wsu_bf, wsd_bf, ln3_g3, ln3_b3, l)
    return h.reshape(b, length, d)
```

```python
import math

import jax
import jax.numpy as jnp
from jax import lax
from jax.experimental import pallas as pl
from jax.experimental.pallas import tpu as pltpu

F32 = jnp.float32
BF16 = jnp.bfloat16

HEAD_DIM = 64
A_Q_HEADS = 16
A_KV_HEADS = 2
A_WINDOW = 128
B_HEADS = 16
B_BRANCH_DILATIONS = (16, 4, 1)
A_WIDTH = A_Q_HEADS * HEAD_DIM
A_KV_WIDTH = A_KV_HEADS * HEAD_DIM
B_WIDTH = B_HEADS * HEAD_DIM
A_PROJ_WIDTH = A_WIDTH + 2 * A_KV_WIDTH
BLK = 128

MEM_HEADS = 4

N_EXPERTS = 64
TOP_K = 8
N_GROUPS = 8
GROUP_SIZE = N_EXPERTS // N_GROUPS
TOPK_GROUPS = 4
ROUTE_SCALE = 2.5

DEPTH = 2
ALPHA = (2.0 * DEPTH) ** 0.25
LN_EPS = 1e-5
RMS_EPS = 1e-6

LANES = 128
SUBLANES = 8
NEG = -1e30

MOE_ROWS = 512
COMBINE_ROWS = 128
VMEM_LIMIT = 56 * 1024 * 1024


def _params(sem, vmem=VMEM_LIMIT):
    return pltpu.CompilerParams(dimension_semantics=sem, vmem_limit_bytes=vmem)


def _dot(a, b):
    return jnp.dot(a, b, preferred_element_type=F32)


def _dot_nt(a, b):
    return lax.dot_general(a, b, (((1,), (1,)), ((), ())), preferred_element_type=F32)


def _layer_norm(y, g, b):
    mu = jnp.mean(y, axis=-1, keepdims=True)
    yc = y - mu
    var = jnp.mean(yc * yc, axis=-1, keepdims=True)
    return yc * lax.rsqrt(var + LN_EPS) * g + b


def _layer_spec(shape, layer, index=None, single=True):
    index = index or (0,) * len(shape)
    mode = dict(pipeline_mode=pl.Buffered(1)) if single else {}
    return pl.BlockSpec((None,) + tuple(shape), lambda *_: (layer,) + tuple(index), **mode)


def _mm_kernel(x_ref, w_ref, o_ref):
    o_ref[...] = _dot(x_ref[...], w_ref[...]).astype(o_ref.dtype)


def _matmul(x, w, layer, tm, tn, out_dtype):
    m, k = x.shape
    n = w.shape[2]
    return pl.pallas_call(
        _mm_kernel,
        grid=(n // tn, m // tm),
        in_specs=[pl.BlockSpec((tm, k), lambda j, i: (i, 0)),
                  pl.BlockSpec((None, k, tn), lambda j, i: (layer, 0, j))],
        out_specs=pl.BlockSpec((tm, tn), lambda j, i: (i, j)),
        out_shape=jax.ShapeDtypeStruct((m, n), out_dtype),
        compiler_params=_params(("arbitrary", "arbitrary")),
        name="matmul",
    )(x, w)


def _band_bias(stacked_heads, max_offset, slope_of_stack):
    rows = stacked_heads * BLK
    row = lax.broadcasted_iota(jnp.int32, (rows, 2 * BLK), 0)
    col = lax.broadcasted_iota(jnp.int32, (rows, 2 * BLK), 1)
    offset = (row & (BLK - 1)) + BLK - col
    slope = jnp.full((rows, 2 * BLK), slope_of_stack[stacked_heads - 1], F32)
    for a in range(stacked_heads - 2, -1, -1):
        slope = jnp.where(row < (a + 1) * BLK, slope_of_stack[a], slope)
    valid = jnp.logical_and(offset >= 0, offset <= max_offset)
    bias = jnp.where(valid, -slope * offset.astype(F32), NEG)
    return bias, jnp.where(col < BLK, NEG, bias)


def _roll_half(t):
    return pltpu.roll(t.astype(F32), LANES // 2, axis=1).astype(BF16)


A_STACK = 4


def _attn_a_kernel(slopes_ref, sinks_ref, q_ref, kc_ref, kp_ref, vc_ref, vp_ref, g_ref,
                   o_ref, o_scr, bias_scr):
    n = pl.program_id(1)
    tiles_per_kv = A_Q_HEADS // A_KV_HEADS // 2

    def head_of(kv, half, a):
        return 2 * (kv * tiles_per_kv + a) + half

    @pl.when(jnp.logical_and(pl.program_id(0) == 0, n == 0))
    def _():
        for kv in range(A_KV_HEADS):
            for half in range(2):
                slopes = [slopes_ref[head_of(kv, half, a)] for a in range(A_STACK)]
                bias, bias_first = _band_bias(A_STACK, A_WINDOW - 1, slopes)
                bias_scr[0, 2 * kv + half] = bias
                bias_scr[1, 2 * kv + half] = bias_first

    first = (n == 0).astype(jnp.int32)
    lane = lax.broadcasted_iota(jnp.int32, (BLK, LANES), 1)
    lo = lane < HEAD_DIM
    srow = lax.broadcasted_iota(jnp.int32, (A_STACK * BLK, 1), 0)

    k_ver = (jnp.concatenate([kp_ref[0], kc_ref[0]], axis=0),
             jnp.concatenate([_roll_half(kp_ref[0]), _roll_half(kc_ref[0])], axis=0))
    v_ver = (jnp.concatenate([vp_ref[0], vc_ref[0]], axis=0),
             jnp.concatenate([_roll_half(vp_ref[0]), _roll_half(vc_ref[0])], axis=0))

    ss = jnp.zeros((BLK, 1), F32)
    for kv in range(A_KV_HEADS):
        outs = []
        for half in range(2):
            keep = lo if half == 0 else jnp.logical_not(lo)
            parts = []
            for a in range(A_STACK):
                t = kv * tiles_per_kv + a
                qt = q_ref[0, :, t * LANES:(t + 1) * LANES] * jnp.asarray(0.125, BF16)
                parts.append(jnp.where(keep, qt, jnp.zeros_like(qt)))
            qs = jnp.concatenate(parts, axis=0)
            ver = kv ^ half
            logits = _dot_nt(qs, k_ver[ver]) + bias_scr[first, 2 * kv + half]
            sink = jnp.full((A_STACK * BLK, 1), sinks_ref[head_of(kv, half, A_STACK - 1)], F32)
            for a in range(A_STACK - 2, -1, -1):
                sink = jnp.where(srow < (a + 1) * BLK, sinks_ref[head_of(kv, half, a)], sink)
            m = jnp.maximum(jnp.max(logits, axis=1, keepdims=True), sink)
            e = jnp.exp(logits - m)
            s = jnp.sum(e, axis=1, keepdims=True) + jnp.exp(sink - m)
            acc = _dot(e.astype(BF16), v_ver[ver])
            outs.append(acc * (1.0 / s))
        for a in range(A_STACK):
            t = kv * tiles_per_kv + a
            o_tile = jnp.where(lo, outs[0][a * BLK:(a + 1) * BLK], outs[1][a * BLK:(a + 1) * BLK])
            ss = ss + jnp.sum(o_tile * o_tile, axis=1, keepdims=True)
            o_scr[:, t * LANES:(t + 1) * LANES] = o_tile
    scale = lax.rsqrt(ss * (1.0 / A_WIDTH) + RMS_EPS)
    o_ref[0] = (o_scr[...] * scale * g_ref[...]).astype(o_ref.dtype)


def _attn_a(proj_a, slopes, sinks, g_a):
    b, length, _ = proj_a.shape
    kcol = A_WIDTH // LANES
    vcol = kcol + 1
    grid_spec = pltpu.PrefetchScalarGridSpec(
        num_scalar_prefetch=2,
        grid=(b, length // BLK),
        in_specs=[
            pl.BlockSpec((1, BLK, A_WIDTH), lambda i, n, *_: (i, n, 0)),
            pl.BlockSpec((1, BLK, LANES), lambda i, n, *_: (i, n, kcol)),
            pl.BlockSpec((1, BLK, LANES), lambda i, n, *_: (i, jnp.maximum(n - 1, 0), kcol)),
            pl.BlockSpec((1, BLK, LANES), lambda i, n, *_: (i, n, vcol)),
            pl.BlockSpec((1, BLK, LANES), lambda i, n, *_: (i, jnp.maximum(n - 1, 0), vcol)),
            pl.BlockSpec((1, A_WIDTH), lambda i, n, *_: (0, 0)),
        ],
        out_specs=pl.BlockSpec((1, BLK, A_WIDTH), lambda i, n, *_: (i, n, 0)),
        scratch_shapes=[pltpu.VMEM((BLK, A_WIDTH), F32),
                        pltpu.VMEM((2, 2 * A_KV_HEADS, A_STACK * BLK, 2 * BLK), F32)],
    )
    return pl.pallas_call(
        _attn_a_kernel,
        grid_spec=grid_spec,
        out_shape=jax.ShapeDtypeStruct((b, length, A_WIDTH), BF16),
        compiler_params=_params(("arbitrary", "arbitrary")),
        name="attn_a",
    )(slopes, sinks, proj_a, proj_a, proj_a, proj_a, proj_a, g_a.reshape(1, A_WIDTH))


B_PIPE_DEPTH = 4


def _attn_b_kernel(slopes_ref, q_ref, k_ref, v_ref, o_ref, acc_scr, m_scr, s_scr, bias_scr,
                   *stage_scrs):
    l_scrs = stage_scrs[:B_PIPE_DEPTH]
    mx_scrs = stage_scrs[B_PIPE_DEPTH:]
    hp = pl.program_id(1)
    length = q_ref.shape[1]
    lane = lax.broadcasted_iota(jnp.int32, (BLK, LANES), 1)
    lo = lane < HEAD_DIM
    qrow = lax.broadcasted_iota(jnp.int32, (2 * BLK, LANES), 0)
    qlane = lax.broadcasted_iota(jnp.int32, (2 * BLK, LANES), 1)
    qmask = (qrow >= BLK) == (qlane >= HEAD_DIM)

    for branch, dil in enumerate(B_BRANCH_DILATIONS):
        slopes = [slopes_ref[2 * hp] * float(dil), slopes_ref[2 * hp + 1] * float(dil)]
        bias, bias_first = _band_bias(2, BLK, slopes)
        bias_scr[branch, 0] = bias
        bias_scr[branch, 1] = bias_first

    for branch, dil in enumerate(B_BRANCH_DILATIONS):
        blocks_per_class = length // (dil * BLK)

        def rows(start, dil=dil):
            if dil == 1:
                return pl.ds(start, BLK)
            return pl.ds(start, BLK, stride=dil)

        def block_rows(it, dil=dil, blocks_per_class=blocks_per_class, rows=rows):
            res = it // blocks_per_class
            blk = it % blocks_per_class
            cur = rows(res + dil * BLK * blk)
            prev = rows(res + dil * BLK * jnp.maximum(blk - 1, 0))
            return cur, prev, jnp.where(blk == 0, 1, 0)

        def logits_stage(it, l_ref, m_ref, branch=branch, block_rows=block_rows):
            cur, prev, first = block_rows(it)
            q2 = q_ref[0, cur, :] * 0.125
            qs = jnp.concatenate([q2, q2], axis=0)
            qs = jnp.where(qmask, qs, 0.0).astype(BF16)
            k2 = jnp.concatenate([k_ref[0, prev, :], k_ref[0, cur, :]], axis=0).astype(BF16)
            logits = _dot_nt(qs, k2) + bias_scr[branch, first]
            l_ref[...] = logits
            m_ref[...] = jnp.broadcast_to(jnp.max(logits, axis=1, keepdims=True), m_ref.shape)

        def value_stage(it, l_ref, m_ref, branch=branch, block_rows=block_rows):
            cur, prev, _ = block_rows(it)
            v2 = jnp.concatenate([v_ref[0, prev, :], v_ref[0, cur, :]], axis=0).astype(BF16)
            m = m_ref[...]
            e = jnp.exp(l_ref[...] - jnp.concatenate([m, m], axis=1))
            s = jnp.sum(e, axis=1, keepdims=True)
            acc = _dot(e.astype(BF16), v2)
            m_t = jnp.where(lo, m[:BLK], m[BLK:])
            s_t = jnp.where(lo, s[:BLK], s[BLK:])
            acc_t = jnp.where(lo, acc[:BLK], acc[BLK:])
            if branch == 0:
                m_scr[cur, :] = m_t
                s_scr[cur, :] = s_t
                acc_scr[cur, :] = acc_t
            else:
                m_old = m_scr[cur, :]
                s_old = s_scr[cur, :]
                acc_old = acc_scr[cur, :]
                m_new = jnp.maximum(m_old, m_t)
                w_old = jnp.exp(m_old - m_new)
                w_blk = jnp.exp(m_t - m_new)
                m_scr[cur, :] = m_new
                s_scr[cur, :] = s_old * w_old + s_t * w_blk
                acc_scr[cur, :] = acc_old * w_old + acc_t * w_blk

        n_blocks = length // BLK
        depth = len(l_scrs)
        for u in range(depth):
            logits_stage(u, l_scrs[u], mx_scrs[u])

        def body(j, carry, logits_stage=logits_stage, value_stage=value_stage, depth=depth):
            for u in range(depth):
                value_stage(depth * j + u, l_scrs[u], mx_scrs[u])
            for u in range(depth):
                logits_stage(depth * (j + 1) + u, l_scrs[u], mx_scrs[u])
            return carry

        lax.fori_loop(0, n_blocks // depth - 1, body, 0)
        for u in range(depth):
            value_stage(n_blocks - depth + u, l_scrs[u], mx_scrs[u])

    o_ref[0] = acc_scr[...] / s_scr[...]


def _attn_b(proj_b, slopes):
    b, length, _ = proj_b.shape
    pairs = B_WIDTH // LANES
    grid_spec = pltpu.PrefetchScalarGridSpec(
        num_scalar_prefetch=1,
        grid=(b, pairs),
        in_specs=[
            pl.BlockSpec((1, length, LANES), lambda i, p, *_: (i, 0, p)),
            pl.BlockSpec((1, length, LANES), lambda i, p, *_: (i, 0, pairs + p)),
            pl.BlockSpec((1, length, LANES), lambda i, p, *_: (i, 0, 2 * pairs + p)),
        ],
        out_specs=pl.BlockSpec((1, length, LANES), lambda i, p, *_: (i, 0, p)),
        scratch_shapes=[pltpu.VMEM((length, LANES), F32)] * 3
        + [pltpu.VMEM((len(B_BRANCH_DILATIONS), 2, 2 * BLK, 2 * BLK), F32)]
        + [pltpu.VMEM((2 * BLK, 2 * BLK), F32)] * B_PIPE_DEPTH
        + [pltpu.VMEM((2 * BLK, LANES), F32)] * B_PIPE_DEPTH,
    )
    return pl.pallas_call(
        _attn_b_kernel,
        grid_spec=grid_spec,
        out_shape=jax.ShapeDtypeStruct((b, length, B_WIDTH), F32),
        compiler_params=_params(("arbitrary", "arbitrary")),
        name="attn_b",
    )(slopes, proj_b, proj_b, proj_b)


def _mix_out_kernel(xa_ref, xb_ref, gb_ref, wa_ref, wb_ref, h_ref, g_ref, b_ref,
                    o_ref, obf_ref):
    xb = xb_ref[...]
    scale = lax.rsqrt(jnp.mean(xb * xb, axis=-1, keepdims=True) + RMS_EPS)
    xb = (xb * scale * gb_ref[...]).astype(BF16)
    y = _dot(xa_ref[...], wa_ref[...]) + _dot(xb, wb_ref[...])
    out = _layer_norm(ALPHA * h_ref[...] + y, g_ref[...], b_ref[...])
    o_ref[...] = out
    obf_ref[...] = out.astype(BF16)


def _proj_out_kernel(x_ref, w_ref, h_ref, g_ref, b_ref, o_ref, obf_ref):
    y = _dot(x_ref[...], w_ref[...])
    out = _layer_norm(ALPHA * h_ref[...] + y, g_ref[...], b_ref[...])
    o_ref[...] = out
    obf_ref[...] = out.astype(BF16)


def _mix_out(xa, xb, g_b, w_out, h, ln_g, ln_b, layer, tm=512):
    n, d = h.shape
    ka = xa.shape[1]
    kb = xb.shape[1]
    return pl.pallas_call(
        _mix_out_kernel,
        grid=(n // tm,),
        in_specs=[pl.BlockSpec((tm, ka), lambda i: (i, 0)),
                  pl.BlockSpec((tm, kb), lambda i: (i, 0)),
                  _layer_spec((1, kb), layer),
                  _layer_spec((ka, d), layer, (0, 0)),
                  _layer_spec((kb, d), layer, (ka // kb, 0)),
                  pl.BlockSpec((tm, d), lambda i: (i, 0)),
                  _layer_spec((1, d), layer),
                  _layer_spec((1, d), layer)],
        out_specs=[pl.BlockSpec((tm, d), lambda i: (i, 0)),
                   pl.BlockSpec((tm, d), lambda i: (i, 0))],
        out_shape=[jax.ShapeDtypeStruct((n, d), F32), jax.ShapeDtypeStruct((n, d), BF16)],
        compiler_params=_params(("arbitrary",)),
        name="mix_out",
    )(xa, xb, g_b, w_out, w_out, h, ln_g, ln_b)


def _proj_out(x, w, h, ln_g, ln_b, layer, tm=512):
    n, d = h.shape
    k = x.shape[1]
    return pl.pallas_call(
        _proj_out_kernel,
        grid=(n // tm,),
        in_specs=[pl.BlockSpec((tm, k), lambda i: (i, 0)),
                  _layer_spec((k, d), layer),
                  pl.BlockSpec((tm, d), lambda i: (i, 0)),
                  _layer_spec((1, d), layer),
                  _layer_spec((1, d), layer)],
        out_specs=[pl.BlockSpec((tm, d), lambda i: (i, 0)),
                   pl.BlockSpec((tm, d), lambda i: (i, 0))],
        out_shape=[jax.ShapeDtypeStruct((n, d), F32), jax.ShapeDtypeStruct((n, d), BF16)],
        compiler_params=_params(("arbitrary",)),
        name="proj_out",
    )(x, w, h, ln_g, ln_b)


def _mem_attn_kernel(x_ref, wq_ref, k_ref, v_ref, o_ref):
    d = x_ref.shape[2]
    hd = d // MEM_HEADS
    q = _dot(x_ref[0], wq_ref[...]) * (1.0 / math.sqrt(hd))
    for head in range(MEM_HEADS):
        cols = slice(head * hd, (head + 1) * hd)
        logits = _dot_nt(q[:, cols].astype(BF16), k_ref[0, :, cols])
        m = jnp.max(logits, axis=1, keepdims=True)
        e = jnp.exp(logits - m)
        s = jnp.sum(e, axis=1, keepdims=True)
        o = _dot(e.astype(BF16), v_ref[0, :, cols]) / s
        o_ref[0, :, cols] = o.astype(o_ref.dtype)


def _mem_attn(x, wq, km, vm, layer, tm=512):
    b, length, d = x.shape
    mlen = km.shape[1]
    return pl.pallas_call(
        _mem_attn_kernel,
        grid=(b, length // tm),
        in_specs=[pl.BlockSpec((1, tm, d), lambda i, j: (i, j, 0)),
                  _layer_spec((d, d), layer),
                  pl.BlockSpec((1, mlen, d), lambda i, j: (i, 0, 0)),
                  pl.BlockSpec((1, mlen, d), lambda i, j: (i, 0, 0))],
        out_specs=pl.BlockSpec((1, tm, d), lambda i, j: (i, j, 0)),
        out_shape=jax.ShapeDtypeStruct((b, length, d), BF16),
        compiler_params=_params(("arbitrary", "arbitrary")),
        name="mem_attn",
    )(x, wq, km, vm)


def _split_bf16(a):
    hi = a.astype(BF16)
    lo = (a - hi.astype(F32)).astype(BF16)
    return hi, lo


def _router_kernel(x_ref, wt_ref, bias_ref, idx_ref, gate_ref, mask_ref, cnt_ref):
    tm = x_ref.shape[0]
    xh, xl = _split_bf16(x_ref[...])
    wh, wl = _split_bf16(wt_ref[...])
    logits = _dot_nt(wh, xh) + (_dot_nt(wh, xl) + _dot_nt(wl, xh))
    scores = jax.nn.sigmoid(logits)
    sel = scores + bias_ref[...]

    member = lax.broadcasted_iota(jnp.int32, (GROUP_SIZE, tm), 0)
    group_scores = []
    for g in range(N_GROUPS):
        v = sel[g * GROUP_SIZE:(g + 1) * GROUP_SIZE, :]
        m1 = jnp.max(v, axis=0, keepdims=True)
        first = jnp.min(jnp.where(v == m1, member, GROUP_SIZE), axis=0, keepdims=True)
        m2 = jnp.max(jnp.where(member == first, -jnp.inf, v), axis=0, keepdims=True)
        group_scores.append(m1 + m2)
    masked = []
    for g in range(N_GROUPS):
        beaten_by = jnp.zeros((1, tm), jnp.int32)
        for o in range(N_GROUPS):
            if o == g:
                continue
            wins = group_scores[o] > group_scores[g]
            if o < g:
                wins = jnp.logical_or(wins, group_scores[o] == group_scores[g])
            beaten_by = beaten_by + wins.astype(jnp.int32)
        keep = beaten_by < TOPK_GROUPS
        masked.append(jnp.where(keep, sel[g * GROUP_SIZE:(g + 1) * GROUP_SIZE, :], -jnp.inf))
    cand = jnp.concatenate(masked, axis=0)

    expert = lax.broadcasted_iota(jnp.int32, (N_EXPERTS, tm), 0)
    beaten_by = jnp.zeros((N_EXPERTS, tm), jnp.int32)
    for o in range(N_EXPERTS):
        other = cand[o:o + 1, :]
        wins = jnp.logical_or(other > cand, jnp.logical_and(other == cand, expert > o))
        beaten_by = beaten_by + wins.astype(jnp.int32)
    chosen = beaten_by < TOP_K
    chosen_i = chosen.astype(jnp.int32)
    mask_ref[...] = chosen_i
    cnt_ref[...] = jnp.broadcast_to(jnp.sum(chosen_i, axis=1, keepdims=True), cnt_ref.shape)

    picked = jnp.where(chosen, scores, 0.0)
    gate = picked / jnp.sum(picked, axis=0, keepdims=True) * ROUTE_SCALE

    running = jnp.zeros((1, tm), jnp.int32)
    slots = []
    for g in range(N_GROUPS):
        part = chosen_i[g * GROUP_SIZE:(g + 1) * GROUP_SIZE, :]
        rows_ = []
        for r in range(GROUP_SIZE):
            rows_.append(running)
            running = running + part[r:r + 1, :]
        slots.append(jnp.concatenate(rows_, axis=0))
    slot = jnp.concatenate(slots, axis=0)
    idx_rows = []
    gate_rows = []
    for k in range(TOP_K):
        hit = jnp.logical_and(chosen, slot == k)
        idx_rows.append(jnp.sum(jnp.where(hit, expert, 0), axis=0, keepdims=True))
        gate_rows.append(jnp.sum(jnp.where(hit, gate, 0.0), axis=0, keepdims=True))
    idx_ref[...] = jnp.concatenate(idx_rows, axis=0)
    gate_ref[...] = jnp.concatenate(gate_rows, axis=0)


def _router(x, w_router_t, bias, tm=512):
    n, d = x.shape
    return pl.pallas_call(
        _router_kernel,
        grid=(n // tm,),
        in_specs=[pl.BlockSpec((tm, d), lambda i: (i, 0)),
                  pl.BlockSpec((N_EXPERTS, d), lambda i: (0, 0)),
                  pl.BlockSpec((N_EXPERTS, 1), lambda i: (0, 0))],
        out_specs=[pl.BlockSpec((TOP_K, tm), lambda i: (0, i)),
                   pl.BlockSpec((TOP_K, tm), lambda i: (0, i)),
                   pl.BlockSpec((N_EXPERTS, tm), lambda i: (0, i)),
                   pl.BlockSpec((None, N_EXPERTS, LANES), lambda i: (i, 0, 0))],
        out_shape=[jax.ShapeDtypeStruct((TOP_K, n), jnp.int32),
                   jax.ShapeDtypeStruct((TOP_K, n), F32),
                   jax.ShapeDtypeStruct((N_EXPERTS, n), jnp.int32),
                   jax.ShapeDtypeStruct((n // tm, N_EXPERTS, LANES), jnp.int32)],
        compiler_params=_params(("arbitrary",)),
        name="router",
    )(x, w_router_t, bias.reshape(N_EXPERTS, 1))


def _plan_kernel(mask_ref, idx_ref, off_ref, pos_ref):
    tm = mask_ref.shape[1]
    before = (lax.broadcasted_iota(jnp.int32, (tm, tm), 0)
              < lax.broadcasted_iota(jnp.int32, (tm, tm), 1)).astype(BF16)
    rank = _dot(mask_ref[...].astype(BF16), before).astype(jnp.int32)
    row_of = off_ref[...] + rank
    expert = lax.broadcasted_iota(jnp.int32, (N_EXPERTS, tm), 0)
    rows_ = []
    for k in range(TOP_K):
        hit = expert == idx_ref[k:k + 1, :]
        rows_.append(jnp.sum(jnp.where(hit, row_of, 0), axis=0, keepdims=True))
    pos_ref[...] = jnp.concatenate(rows_, axis=0)


def _plan(mask_t, idx_t, tile_offsets, tm):
    n = mask_t.shape[1]
    return pl.pallas_call(
        _plan_kernel,
        grid=(n // tm,),
        in_specs=[pl.BlockSpec((N_EXPERTS, tm), lambda i: (0, i)),
                  pl.BlockSpec((TOP_K, tm), lambda i: (0, i)),
                  pl.BlockSpec((None, N_EXPERTS, 1), lambda i: (i, 0, 0))],
        out_specs=pl.BlockSpec((TOP_K, tm), lambda i: (0, i)),
        out_shape=jax.ShapeDtypeStruct((TOP_K, n), jnp.int32),
        compiler_params=_params(("arbitrary",)),
        name="plan",
    )(mask_t, idx_t, tile_offsets)


def _block_layout(counts_per_tile, n):
    rows = MOE_ROWS
    nb = (n * TOP_K) // rows + N_EXPERTS
    counts = jnp.sum(counts_per_tile, axis=0)
    padded = (counts + rows - 1) // rows * rows
    padded_ends = jnp.cumsum(padded)
    padded_starts = padded_ends - padded
    tile_offsets = padded_starts[None, :] + jnp.cumsum(counts_per_tile, axis=0) - counts_per_tile
    used = (padded_ends[-1] // rows).astype(jnp.int32)
    block_start = jnp.arange(nb, dtype=jnp.int32) * rows
    block_expert = jnp.minimum(
        jnp.sum((padded_ends[None, :] <= block_start[:, None]).astype(jnp.int32), axis=1),
        N_EXPERTS - 1)
    last = block_expert[jnp.maximum(used - 1, 0)]
    block_expert = jnp.where(jnp.arange(nb) < used, block_expert, last)
    pad_from = (padded_starts + counts).astype(jnp.int32)
    return (tile_offsets.astype(jnp.int32)[:, :, None], block_expert, used.reshape(1),
            pad_from, padded_ends.astype(jnp.int32))


DISPATCH_ROWS = 128


def _pack_rows(x):
    k = x.shape[1] // 2
    lo = pltpu.bitcast(x[:, :k].astype(BF16).astype(F32), jnp.uint32) >> 16
    hi = pltpu.bitcast(x[:, k:].astype(BF16).astype(F32), jnp.uint32) & jnp.uint32(0xFFFF0000)
    return lo | hi


def _unpack_rows(w):
    lo = pltpu.bitcast(w << 16, F32)
    hi = pltpu.bitcast(w & jnp.uint32(0xFFFF0000), F32)
    return lo, hi


def _dispatch_kernel(pad_from_ref, pad_to_ref, used_ref, pos_hbm, x_ref, xs_hbm, pos_smem,
                     xbuf, zeros, row_sem, pos_sem, pad_sem):
    i = pl.program_id(0)
    steps = pl.num_programs(0)
    tm = pos_smem.shape[2]
    slot = i % 3

    def wait_rows(s):
        for _ in range(TOP_K):
            pltpu.make_async_copy(xbuf.at[s], xs_hbm.at[pl.ds(0, tm)], row_sem.at[s]).wait()

    def pos_copy(blk):
        return pltpu.make_async_copy(pos_hbm.at[blk], pos_smem.at[blk % 2], pos_sem.at[blk % 2])

    @pl.when(i == 0)
    def _():
        pos_copy(0).start()
        zeros[...] = jnp.zeros_like(zeros)
        block = zeros.shape[0]

        def tail_copy(blk):
            start = pl.multiple_of(blk * block, block)
            return pltpu.make_async_copy(zeros, xs_hbm.at[pl.ds(start, block)], pad_sem)

        def tail_start(blk, c):
            tail_copy(blk).start()
            return c

        def tail_wait(blk, c):
            tail_copy(blk).wait()
            return c

        lax.fori_loop(used_ref[0], xs_hbm.shape[0] // block, tail_start, 0)
        lax.fori_loop(used_ref[0], xs_hbm.shape[0] // block, tail_wait, 0)

        def pad_walk(start_not_wait):
            def act(cp):
                if start_not_wait:
                    cp.start()
                else:
                    cp.wait()

            def per_expert(e, carry):
                lo = pad_from_ref[e]
                hi = pad_to_ref[e]
                lo8 = jnp.minimum((lo + SUBLANES - 1) // SUBLANES * SUBLANES, hi)

                def one_row(r, c):
                    act(pltpu.make_async_copy(zeros.at[pl.ds(0, 1)], xs_hbm.at[pl.ds(r, 1)],
                                              pad_sem))
                    return c

                lax.fori_loop(lo, lo8, one_row, 0)
                rem = hi - lo8
                off = lo8
                size = block // 2
                while size >= SUBLANES:
                    @pl.when((rem & size) != 0)
                    def _(off=off, size=size):
                        dst = xs_hbm.at[pl.ds(pl.multiple_of(off, SUBLANES), size)]
                        act(pltpu.make_async_copy(zeros.at[pl.ds(0, size)], dst, pad_sem))
                    off = off + (rem & size)
                    size //= 2
                return carry

            lax.fori_loop(0, N_EXPERTS, per_expert, 0)

        pad_walk(True)
        pad_walk(False)

    @pl.when(i >= 3)
    def _():
        wait_rows(slot)

    @pl.when(i + 1 < steps)
    def _():
        pos_copy(i + 1).start()

    xbuf[slot] = _pack_rows(x_ref[...])
    pos_copy(i).wait()
    pslot = i % 2

    for r in range(tm):
        for k in range(TOP_K):
            pltpu.make_async_copy(xbuf.at[slot, pl.ds(r, 1)],
                                  xs_hbm.at[pl.ds(pos_smem[pslot, k, r], 1)],
                                  row_sem.at[slot]).start()

    @pl.when(i == steps - 1)
    def _():
        wait_rows(slot)

        @pl.when(steps > 1)
        def _():
            wait_rows((i + 2) % 3)

        @pl.when(steps > 2)
        def _():
            wait_rows((i + 1) % 3)


def _dispatch(pos, x, pad_from, pad_to, used, total_rows):
    n, d = x.shape
    tm = DISPATCH_ROWS
    grid_spec = pltpu.PrefetchScalarGridSpec(
        num_scalar_prefetch=3,
        grid=(n // tm,),
        in_specs=[pl.BlockSpec(memory_space=pl.ANY),
                  pl.BlockSpec((tm, d), lambda i, *_: (i, 0))],
        out_specs=pl.BlockSpec(memory_space=pl.ANY),
        scratch_shapes=[pltpu.SMEM((2, TOP_K, tm), jnp.int32),
                        pltpu.VMEM((3, tm, d // 2), jnp.uint32),
                        pltpu.VMEM((MOE_ROWS, d // 2), jnp.uint32),
                        pltpu.SemaphoreType.DMA((3,)),
                        pltpu.SemaphoreType.DMA((2,)),
                        pltpu.SemaphoreType.DMA],
    )
    return pl.pallas_call(
        _dispatch_kernel,
        grid_spec=grid_spec,
        out_shape=jax.ShapeDtypeStruct((total_rows, d // 2), jnp.uint32),
        compiler_params=_params(("arbitrary",)),
        name="dispatch",
    )(pad_from, pad_to, used, pos.reshape(TOP_K, n // tm, tm).transpose(1, 0, 2), x)


def _experts_kernel(be_ref, used_ref, x_ref, wg_ref, wu_ref, wd_ref, y_ref,
                    wg_bf, wu_bf, wd_bf):
    i = pl.program_id(0)
    used = used_ref[0]

    @pl.when(i < used)
    def _():
        changed = jnp.logical_or(i == 0, be_ref[jnp.maximum(i - 1, 0)] != be_ref[i])

        @pl.when(changed)
        def _():
            wg_bf[...] = wg_ref[...].astype(BF16)
            wu_bf[...] = wu_ref[...].astype(BF16)
            wd_bf[...] = wd_ref[...].astype(BF16)

        half = x_ref.shape[1]
        x_lo, x_hi = (t.astype(BF16) for t in _unpack_rows(x_ref[...]))
        gate = _dot(x_lo, wg_bf[:half, :]) + _dot(x_hi, wg_bf[half:, :])
        up = _dot(x_lo, wu_bf[:half, :]) + _dot(x_hi, wu_bf[half:, :])
        hid = jax.nn.silu(gate) * up
        y_ref[...] = _pack_rows(_dot(hid.astype(BF16), wd_bf[...]))

    @pl.when(i >= used)
    def _():
        y_ref[...] = jnp.zeros_like(y_ref)


def _experts(block_expert, used, xs, w_gate, w_up, w_down, layer):
    nb = block_expert.shape[0]
    rows = MOE_ROWS
    d = w_gate.shape[2]
    de = w_gate.shape[3]
    grid_spec = pltpu.PrefetchScalarGridSpec(
        num_scalar_prefetch=2,
        grid=(nb,),
        in_specs=[
            pl.BlockSpec((rows, d // 2), lambda i, be, u: (jnp.minimum(i, u[0] - 1), 0)),
            pl.BlockSpec((None, None, d, de), lambda i, be, u: (layer, be[i], 0, 0)),
            pl.BlockSpec((None, None, d, de), lambda i, be, u: (layer, be[i], 0, 0)),
            pl.BlockSpec((None, None, de, d), lambda i, be, u: (layer, be[i], 0, 0)),
        ],
        out_specs=pl.BlockSpec((rows, d // 2), lambda i, be, u: (i, 0)),
        scratch_shapes=[
            pltpu.VMEM((d, de), BF16),
            pltpu.VMEM((d, de), BF16),
            pltpu.VMEM((de, d), BF16),
        ],
    )
    return pl.pallas_call(
        _experts_kernel,
        grid_spec=grid_spec,
        out_shape=jax.ShapeDtypeStruct((nb * rows, d // 2), jnp.uint32),
        compiler_params=_params(("arbitrary",)),
        name="experts",
    )(block_expert, used, xs, w_gate, w_up, w_down)


def _combine_kernel(pos_hbm, y_hbm, gate_ref, h_ref, wsg_ref, wsu_ref, wsd_ref, g_ref, b_ref,
                    o_ref, obf_ref, ybuf0, ybuf1, ybuf2, pos_smem, row_sem, pos_sem):
    i = pl.program_id(0)
    steps = pl.num_programs(0)
    tm = h_ref.shape[0]
    last = steps - 1
    ybufs = (ybuf0, ybuf1, ybuf2)

    def pos_copy(t):
        return pltpu.make_async_copy(pos_hbm.at[jnp.minimum(t, last)], pos_smem.at[t % 2],
                                     pos_sem.at[t % 2])

    def row_copy(t, buf, k, r):
        return pltpu.make_async_copy(y_hbm.at[pl.ds(pos_smem[t % 2, k, r], 1)],
                                     ybufs[buf].at[k, pl.ds(r, 1)], row_sem.at[buf])

    def wait_rows(buf):
        for k in range(TOP_K):
            pltpu.make_async_copy(y_hbm.at[pl.ds(0, tm)], ybufs[buf].at[k],
                                  row_sem.at[buf]).wait()

    @pl.when(i == 0)
    def _():
        for t in range(2):
            pos_copy(t).start()
            pos_copy(t).wait()

            def issue(r, carry, t=t):
                for k in range(TOP_K):
                    row_copy(t, t, k, r).start()
                return carry
            lax.fori_loop(0, tm, issue, 0)

    def step(cur, ahead):
        wait_rows(cur)

        @pl.when(i > 0)
        def _():
            pos_copy(i + 2).wait()

        @pl.when(i == 0)
        def _():
            pos_copy(2).start()
            pos_copy(2).wait()

        for r in range(tm):
            for k in range(TOP_K):
                row_copy(i + 2, ahead, k, r).start()
        pos_copy(i + 3).start()

        h = h_ref[...]
        x = h.astype(BF16)
        hid = jax.nn.silu(_dot(x, wsg_ref[...])) * _dot(x, wsu_ref[...])
        y = ALPHA * h + _dot(hid.astype(BF16), wsd_ref[...])

        gates = gate_ref[...]
        half = ybufs[cur].shape[2]
        routed_lo = jnp.zeros((tm, half), F32)
        routed_hi = jnp.zeros((tm, half), F32)
        for k in range(TOP_K):
            lo, hi = _unpack_rows(ybufs[cur][k])
            routed_lo = routed_lo + lo * gates[:, k:k + 1]
            routed_hi = routed_hi + hi * gates[:, k:k + 1]
        y = y + jnp.concatenate([routed_lo, routed_hi], axis=1)
        out = _layer_norm(y, g_ref[...], b_ref[...])
        o_ref[...] = out
        obf_ref[...] = out.astype(BF16)

        @pl.when(i == last)
        def _():
            wait_rows((cur + 1) % 3)
            wait_rows(ahead)
            pos_copy(i + 3).wait()

    for phase in range(3):
        @pl.when(i % 3 == phase)
        def _(phase=phase):
            step(phase, (phase + 2) % 3)


def _combine(pos, y_rows, gates, h, ws_gate, ws_up, ws_down, ln_g, ln_b, layer):
    n, d = h.shape
    tm = COMBINE_ROWS
    ds = ws_gate.shape[2]
    return pl.pallas_call(
        _combine_kernel,
        grid=(n // tm,),
        in_specs=[pl.BlockSpec(memory_space=pl.ANY),
                  pl.BlockSpec(memory_space=pl.ANY),
                  pl.BlockSpec((tm, TOP_K), lambda i: (i, 0)),
                  pl.BlockSpec((tm, d), lambda i: (i, 0)),
                  _layer_spec((d, ds), layer),
                  _layer_spec((d, ds), layer),
                  _layer_spec((ds, d), layer),
                  _layer_spec((1, d), layer),
                  _layer_spec((1, d), layer)],
        out_specs=[pl.BlockSpec((tm, d), lambda i: (i, 0)),
                   pl.BlockSpec((tm, d), lambda i: (i, 0))],
        out_shape=[jax.ShapeDtypeStruct((n, d), F32), jax.ShapeDtypeStruct((n, d), BF16)],
        scratch_shapes=[pltpu.VMEM((TOP_K, tm, d // 2), jnp.uint32)] * 3
        + [pltpu.SMEM((2, TOP_K, tm), jnp.int32),
                        pltpu.SemaphoreType.DMA((3,)),
                        pltpu.SemaphoreType.DMA((2,))],
        compiler_params=_params(("arbitrary",)),
        name="combine",
    )(pos.reshape(TOP_K, n // tm, tm).transpose(1, 0, 2), y_rows, gates, h,
      ws_gate, ws_up, ws_down, ln_g, ln_b)


def _alibi_slopes(n):
    return 2.0 ** (-8.0 * jnp.arange(1, n + 1, dtype=F32) / n)


def kernel(x, mem, w_in, a_sinks, g_a, g_b, w_out, ln1_g, ln1_b, wq_m, wk_m, wv_m, wo_m,
           ln2_g, ln2_b, w_router, router_bias, w_gate, w_up, w_down, ws_gate, ws_up, ws_down,
           ln3_g, ln3_b):
    b, length, d = x.shape
    n = b * length
    mlen = mem.shape[1]
    depth = w_in.shape[0]
    h = x.reshape(n, d)
    h_bf = h.astype(BF16)
    mem_bf = mem.reshape(b * mlen, d).astype(BF16)
    slopes_a = _alibi_slopes(A_Q_HEADS)
    slopes_b = _alibi_slopes(B_HEADS)

    w_in_bf = w_in.astype(BF16)
    w_in_a = w_in_bf[:, :, :A_PROJ_WIDTH]
    w_in_b = w_in_bf[:, :, A_PROJ_WIDTH:]
    w_out_bf = w_out.astype(BF16)
    wq_bf, wk_bf, wv_bf, wo_bf = (w.astype(BF16) for w in (wq_m, wk_m, wv_m, wo_m))
    wsg_bf, wsu_bf, wsd_bf = (w.astype(BF16) for w in (ws_gate, ws_up, ws_down))
    row = lambda p: p.reshape(depth, 1, p.shape[1])
    g_b3, ln1_g3, ln1_b3, ln2_g3, ln2_b3, ln3_g3, ln3_b3 = (
        row(p) for p in (g_b, ln1_g, ln1_b, ln2_g, ln2_b, ln3_g, ln3_b))

    for l in range(depth):
        proj_a = _matmul(h_bf, w_in_a, l, 512, A_PROJ_WIDTH, BF16)
        proj_b = _matmul(h_bf, w_in_b, l, 512, B_WIDTH, F32)
        mixed_a = _attn_a(proj_a.reshape(b, length, A_PROJ_WIDTH), slopes_a, a_sinks[l], g_a[l])
        out_b = _attn_b(proj_b.reshape(b, length, 3 * B_WIDTH), slopes_b)
        h, h_bf = _mix_out(mixed_a.reshape(n, A_WIDTH), out_b.reshape(n, B_WIDTH), g_b3,
                           w_out_bf, h, ln1_g3, ln1_b3, l)

        km = _matmul(mem_bf, wk_bf, l, b * mlen, d // 2, BF16)
        vm = _matmul(mem_bf, wv_bf, l, b * mlen, d // 2, BF16)
        o = _mem_attn(h_bf.reshape(b, length, d), wq_bf,
                      km.reshape(b, mlen, d), vm.reshape(b, mlen, d), l)
        h, h_bf = _proj_out(o.reshape(n, d), wo_bf, h, ln2_g3, ln2_b3, l)

        route_tm = 512
        idx_t, gate_t, mask_t, cnt = _router(h, w_router[l].T, router_bias[l], route_tm)
        tile_offsets, block_expert, used, pad_from, pad_to = _block_layout(cnt[:, :, 0], n)
        pos = _plan(mask_t, idx_t, tile_offsets, route_tm)
        xs = _dispatch(pos, h, pad_from, pad_to, used, block_expert.shape[0] * MOE_ROWS)
        y_rows = _experts(block_expert, used, xs, w_gate, w_up, w_down, l)
        h, h_bf = _combine(pos, y_rows, gate_t.T, h, wsg_bf, wsu_bf, wsd_bf, ln3_g3, ln3_b3, l)
    return h.reshape(b, length, d)
```

```python
import math

import jax
import jax.numpy as jnp
from jax import lax
from jax.experimental import pallas as pl
from jax.experimental.pallas import tpu as pltpu

F32 = jnp.float32
BF16 = jnp.bfloat16

HEAD_DIM = 64
A_Q_HEADS = 16
A_KV_HEADS = 2
A_WINDOW = 128
B_HEADS = 16
B_BRANCH_DILATIONS = (16, 4, 1)
A_WIDTH = A_Q_HEADS * HEAD_DIM
A_KV_WIDTH = A_KV_HEADS * HEAD_DIM
B_WIDTH = B_HEADS * HEAD_DIM
A_PROJ_WIDTH = A_WIDTH + 2 * A_KV_WIDTH
BLK = 128

MEM_HEADS = 4

N_EXPERTS = 64
TOP_K = 8
N_GROUPS = 8
GROUP_SIZE = N_EXPERTS // N_GROUPS
TOPK_GROUPS = 4
ROUTE_SCALE = 2.5

DEPTH = 2
ALPHA = (2.0 * DEPTH) ** 0.25
LN_EPS = 1e-5
RMS_EPS = 1e-6

LANES = 128
SUBLANES = 8
NEG = -1e30

MOE_ROWS = 512
COMBINE_ROWS = 128
VMEM_LIMIT = 56 * 1024 * 1024


def _params(sem, vmem=VMEM_LIMIT):
    return pltpu.CompilerParams(dimension_semantics=sem, vmem_limit_bytes=vmem)


def _dot(a, b):
    return jnp.dot(a, b, preferred_element_type=F32)


def _dot_nt(a, b):
    return lax.dot_general(a, b, (((1,), (1,)), ((), ())), preferred_element_type=F32)


def _layer_norm(y, g, b):
    mu = jnp.mean(y, axis=-1, keepdims=True)
    yc = y - mu
    var = jnp.mean(yc * yc, axis=-1, keepdims=True)
    return yc * lax.rsqrt(var + LN_EPS) * g + b


def _layer_spec(shape, layer, index=None, single=True):
    index = index or (0,) * len(shape)
    mode = dict(pipeline_mode=pl.Buffered(1)) if single else {}
    return pl.BlockSpec((None,) + tuple(shape), lambda *_: (layer,) + tuple(index), **mode)


def _mm_kernel(x_ref, w_ref, o_ref):
    o_ref[...] = _dot(x_ref[...], w_ref[...]).astype(o_ref.dtype)


def _matmul(x, w, layer, tm, tn, out_dtype):
    m, k = x.shape
    n = w.shape[2]
    return pl.pallas_call(
        _mm_kernel,
        grid=(n // tn, m // tm),
        in_specs=[pl.BlockSpec((tm, k), lambda j, i: (i, 0)),
                  pl.BlockSpec((None, k, tn), lambda j, i: (layer, 0, j))],
        out_specs=pl.BlockSpec((tm, tn), lambda j, i: (i, j)),
        out_shape=jax.ShapeDtypeStruct((m, n), out_dtype),
        compiler_params=_params(("arbitrary", "arbitrary")),
        name="matmul",
    )(x, w)


def _band_bias(stacked_heads, max_offset, slope_of_stack):
    rows = stacked_heads * BLK
    row = lax.broadcasted_iota(jnp.int32, (rows, 2 * BLK), 0)
    col = lax.broadcasted_iota(jnp.int32, (rows, 2 * BLK), 1)
    offset = (row & (BLK - 1)) + BLK - col
    slope = jnp.full((rows, 2 * BLK), slope_of_stack[stacked_heads - 1], F32)
    for a in range(stacked_heads - 2, -1, -1):
        slope = jnp.where(row < (a + 1) * BLK, slope_of_stack[a], slope)
    valid = jnp.logical_and(offset >= 0, offset <= max_offset)
    bias = jnp.where(valid, -slope * offset.astype(F32), NEG)
    return bias, jnp.where(col < BLK, NEG, bias)


def _roll_half(t):
    return pltpu.roll(t.astype(F32), LANES // 2, axis=1).astype(BF16)


A_STACK = 4


def _attn_a_kernel(slopes_ref, sinks_ref, q_ref, kc_ref, kp_ref, vc_ref, vp_ref, g_ref,
                   o_ref, o_scr, bias_scr):
    n = pl.program_id(1)
    tiles_per_kv = A_Q_HEADS // A_KV_HEADS // 2

    def head_of(kv, half, a):
        return 2 * (kv * tiles_per_kv + a) + half

    @pl.when(jnp.logical_and(pl.program_id(0) == 0, n == 0))
    def _():
        for kv in range(A_KV_HEADS):
            for half in range(2):
                slopes = [slopes_ref[head_of(kv, half, a)] for a in range(A_STACK)]
                bias, bias_first = _band_bias(A_STACK, A_WINDOW - 1, slopes)
                bias_scr[0, 2 * kv + half] = bias
                bias_scr[1, 2 * kv + half] = bias_first

    first = (n == 0).astype(jnp.int32)
    lane = lax.broadcasted_iota(jnp.int32, (BLK, LANES), 1)
    lo = lane < HEAD_DIM
    srow = lax.broadcasted_iota(jnp.int32, (A_STACK * BLK, 1), 0)

    k_ver = (jnp.concatenate([kp_ref[0], kc_ref[0]], axis=0),
             jnp.concatenate([_roll_half(kp_ref[0]), _roll_half(kc_ref[0])], axis=0))
    v_ver = (jnp.concatenate([vp_ref[0], vc_ref[0]], axis=0),
             jnp.concatenate([_roll_half(vp_ref[0]), _roll_half(vc_ref[0])], axis=0))

    ss = jnp.zeros((BLK, 1), F32)
    for kv in range(A_KV_HEADS):
        outs = []
        for half in range(2):
            keep = lo if half == 0 else jnp.logical_not(lo)
            parts = []
            for a in range(A_STACK):
                t = kv * tiles_per_kv + a
                qt = q_ref[0, :, t * LANES:(t + 1) * LANES] * jnp.asarray(0.125, BF16)
                parts.append(jnp.where(keep, qt, jnp.zeros_like(qt)))
            qs = jnp.concatenate(parts, axis=0)
            ver = kv ^ half
            logits = _dot_nt(qs, k_ver[ver]) + bias_scr[first, 2 * kv + half]
            sink = jnp.full((A_STACK * BLK, 1), sinks_ref[head_of(kv, half, A_STACK - 1)], F32)
            for a in range(A_STACK - 2, -1, -1):
                sink = jnp.where(srow < (a + 1) * BLK, sinks_ref[head_of(kv, half, a)], sink)
            m = jnp.maximum(jnp.max(logits, axis=1, keepdims=True), sink)
            e = jnp.exp(logits - m)
            s = jnp.sum(e, axis=1, keepdims=True) + jnp.exp(sink - m)
            acc = _dot(e.astype(BF16), v_ver[ver])
            outs.append(acc * (1.0 / s))
        for a in range(A_STACK):
            t = kv * tiles_per_kv + a
            o_tile = jnp.where(lo, outs[0][a * BLK:(a + 1) * BLK], outs[1][a * BLK:(a + 1) * BLK])
            ss = ss + jnp.sum(o_tile * o_tile, axis=1, keepdims=True)
            o_scr[:, t * LANES:(t + 1) * LANES] = o_tile
    scale = lax.rsqrt(ss * (1.0 / A_WIDTH) + RMS_EPS)
    o_ref[0] = (o_scr[...] * scale * g_ref[...]).astype(o_ref.dtype)


def _attn_a(proj_a, slopes, sinks, g_a):
    b, length, _ = proj_a.shape
    kcol = A_WIDTH // LANES
    vcol = kcol + 1
    grid_spec = pltpu.PrefetchScalarGridSpec(
        num_scalar_prefetch=2,
        grid=(b, length // BLK),
        in_specs=[
            pl.BlockSpec((1, BLK, A_WIDTH), lambda i, n, *_: (i, n, 0)),
            pl.BlockSpec((1, BLK, LANES), lambda i, n, *_: (i, n, kcol)),
            pl.BlockSpec((1, BLK, LANES), lambda i, n, *_: (i, jnp.maximum(n - 1, 0), kcol)),
            pl.BlockSpec((1, BLK, LANES), lambda i, n, *_: (i, n, vcol)),
            pl.BlockSpec((1, BLK, LANES), lambda i, n, *_: (i, jnp.maximum(n - 1, 0), vcol)),
            pl.BlockSpec((1, A_WIDTH), lambda i, n, *_: (0, 0)),
        ],
        out_specs=pl.BlockSpec((1, BLK, A_WIDTH), lambda i, n, *_: (i, n, 0)),
        scratch_shapes=[pltpu.VMEM((BLK, A_WIDTH), F32),
                        pltpu.VMEM((2, 2 * A_KV_HEADS, A_STACK * BLK, 2 * BLK), F32)],
    )
    return pl.pallas_call(
        _attn_a_kernel,
        grid_spec=grid_spec,
        out_shape=jax.ShapeDtypeStruct((b, length, A_WIDTH), BF16),
        compiler_params=_params(("arbitrary", "arbitrary")),
        name="attn_a",
    )(slopes, sinks, proj_a, proj_a, proj_a, proj_a, proj_a, g_a.reshape(1, A_WIDTH))


B_PIPE_DEPTH = 4


def _attn_b_kernel(slopes_ref, q_ref, k_ref, v_ref, o_ref, acc_scr, m_scr, s_scr, bias_scr,
                   *stage_scrs):
    l_scrs = stage_scrs[:B_PIPE_DEPTH]
    mx_scrs = stage_scrs[B_PIPE_DEPTH:]
    hp = pl.program_id(1)
    length = q_ref.shape[1]
    lane = lax.broadcasted_iota(jnp.int32, (BLK, LANES), 1)
    lo = lane < HEAD_DIM
    qrow = lax.broadcasted_iota(jnp.int32, (2 * BLK, LANES), 0)
    qlane = lax.broadcasted_iota(jnp.int32, (2 * BLK, LANES), 1)
    qmask = (qrow >= BLK) == (qlane >= HEAD_DIM)

    for branch, dil in enumerate(B_BRANCH_DILATIONS):
        slopes = [slopes_ref[2 * hp] * float(dil), slopes_ref[2 * hp + 1] * float(dil)]
        bias, bias_first = _band_bias(2, BLK, slopes)
        bias_scr[branch, 0] = bias
        bias_scr[branch, 1] = bias_first

    for branch, dil in enumerate(B_BRANCH_DILATIONS):
        blocks_per_class = length // (dil * BLK)

        def rows(start, dil=dil):
            if dil == 1:
                return pl.ds(start, BLK)
            return pl.ds(start, BLK, stride=dil)

        def block_rows(it, dil=dil, blocks_per_class=blocks_per_class, rows=rows):
            res = it // blocks_per_class
            blk = it % blocks_per_class
            cur = rows(res + dil * BLK * blk)
            prev = rows(res + dil * BLK * jnp.maximum(blk - 1, 0))
            return cur, prev, jnp.where(blk == 0, 1, 0)

        def logits_stage(it, l_ref, m_ref, branch=branch, block_rows=block_rows):
            cur, prev, first = block_rows(it)
            q2 = q_ref[0, cur, :] * 0.125
            qs = jnp.concatenate([q2, q2], axis=0)
            qs = jnp.where(qmask, qs, 0.0).astype(BF16)
            k2 = jnp.concatenate([k_ref[0, prev, :], k_ref[0, cur, :]], axis=0).astype(BF16)
            logits = _dot_nt(qs, k2) + bias_scr[branch, first]
            l_ref[...] = logits
            m_ref[...] = jnp.broadcast_to(jnp.max(logits, axis=1, keepdims=True), m_ref.shape)

        def value_stage(it, l_ref, m_ref, branch=branch, block_rows=block_rows):
            cur, prev, _ = block_rows(it)
            v2 = jnp.concatenate([v_ref[0, prev, :], v_ref[0, cur, :]], axis=0).astype(BF16)
            m = m_ref[...]
            e = jnp.exp(l_ref[...] - jnp.concatenate([m, m], axis=1))
            s = jnp.sum(e, axis=1, keepdims=True)
            acc = _dot(e.astype(BF16), v2)
            m_t = jnp.where(lo, m[:BLK], m[BLK:])
            s_t = jnp.where(lo, s[:BLK], s[BLK:])
            acc_t = jnp.where(lo, acc[:BLK], acc[BLK:])
            if branch == 0:
                m_scr[cur, :] = m_t
                s_scr[cur, :] = s_t
                acc_scr[cur, :] = acc_t
            else:
                m_old = m_scr[cur, :]
                s_old = s_scr[cur, :]
                acc_old = acc_scr[cur, :]
                m_new = jnp.maximum(m_old, m_t)
                w_old = jnp.exp(m_old - m_new)
                w_blk = jnp.exp(m_t - m_new)
                m_scr[cur, :] = m_new
                s_scr[cur, :] = s_old * w_old + s_t * w_blk
                acc_scr[cur, :] = acc_old * w_old + acc_t * w_blk

        n_blocks = length // BLK
        depth = len(l_scrs)
        for u in range(depth):
            logits_stage(u, l_scrs[u], mx_scrs[u])

        def body(j, carry, logits_stage=logits_stage, value_stage=value_stage, depth=depth):
            for u in range(depth):
                value_stage(depth * j + u, l_scrs[u], mx_scrs[u])
            for u in range(depth):
                logits_stage(depth * (j + 1) + u, l_scrs[u], mx_scrs[u])
            return carry

        lax.fori_loop(0, n_blocks // depth - 1, body, 0)
        for u in range(depth):
            value_stage(n_blocks - depth + u, l_scrs[u], mx_scrs[u])

    o_ref[0] = acc_scr[...] / s_scr[...]


def _attn_b(proj_b, slopes):
    b, length, _ = proj_b.shape
    pairs = B_WIDTH // LANES
    grid_spec = pltpu.PrefetchScalarGridSpec(
        num_scalar_prefetch=1,
        grid=(b, pairs),
        in_specs=[
            pl.BlockSpec((1, length, LANES), lambda i, p, *_: (i, 0, p)),
            pl.BlockSpec((1, length, LANES), lambda i, p, *_: (i, 0, pairs + p)),
            pl.BlockSpec((1, length, LANES), lambda i, p, *_: (i, 0, 2 * pairs + p)),
        ],
        out_specs=pl.BlockSpec((1, length, LANES), lambda i, p, *_: (i, 0, p)),
        scratch_shapes=[pltpu.VMEM((length, LANES), F32)] * 3
        + [pltpu.VMEM((len(B_BRANCH_DILATIONS), 2, 2 * BLK, 2 * BLK), F32)]
        + [pltpu.VMEM((2 * BLK, 2 * BLK), F32)] * B_PIPE_DEPTH
        + [pltpu.VMEM((2 * BLK, LANES), F32)] * B_PIPE_DEPTH,
    )
    return pl.pallas_call(
        _attn_b_kernel,
        grid_spec=grid_spec,
        out_shape=jax.ShapeDtypeStruct((b, length, B_WIDTH), F32),
        compiler_params=_params(("arbitrary", "arbitrary")),
        name="attn_b",
    )(slopes, proj_b, proj_b, proj_b)


def _mix_out_kernel(xa_ref, xb_ref, gb_ref, wa_ref, wb_ref, h_ref, g_ref, b_ref,
                    o_ref, obf_ref):
    xb = xb_ref[...]
    scale = lax.rsqrt(jnp.mean(xb * xb, axis=-1, keepdims=True) + RMS_EPS)
    xb = (xb * scale * gb_ref[...]).astype(BF16)
    y = _dot(xa_ref[...], wa_ref[...]) + _dot(xb, wb_ref[...])
    out = _layer_norm(ALPHA * h_ref[...] + y, g_ref[...], b_ref[...])
    o_ref[...] = out
    obf_ref[...] = out.astype(BF16)


def _proj_out_kernel(x_ref, w_ref, h_ref, g_ref, b_ref, o_ref, obf_ref):
    y = _dot(x_ref[...], w_ref[...])
    out = _layer_norm(ALPHA * h_ref[...] + y, g_ref[...], b_ref[...])
    o_ref[...] = out
    obf_ref[...] = out.astype(BF16)


def _mix_out(xa, xb, g_b, w_out, h, ln_g, ln_b, layer, tm=512):
    n, d = h.shape
    ka = xa.shape[1]
    kb = xb.shape[1]
    return pl.pallas_call(
        _mix_out_kernel,
        grid=(n // tm,),
        in_specs=[pl.BlockSpec((tm, ka), lambda i: (i, 0)),
                  pl.BlockSpec((tm, kb), lambda i: (i, 0)),
                  _layer_spec((1, kb), layer),
                  _layer_spec((ka, d), layer, (0, 0)),
                  _layer_spec((kb, d), layer, (ka // kb, 0)),
                  pl.BlockSpec((tm, d), lambda i: (i, 0)),
                  _layer_spec((1, d), layer),
                  _layer_spec((1, d), layer)],
        out_specs=[pl.BlockSpec((tm, d), lambda i: (i, 0)),
                   pl.BlockSpec((tm, d), lambda i: (i, 0))],
        out_shape=[jax.ShapeDtypeStruct((n, d), F32), jax.ShapeDtypeStruct((n, d), BF16)],
        compiler_params=_params(("arbitrary",)),
        name="mix_out",
    )(xa, xb, g_b, w_out, w_out, h, ln_g, ln_b)


def _proj_out(x, w, h, ln_g, ln_b, layer, tm=512):
    n, d = h.shape
    k = x.shape[1]
    return pl.pallas_call(
        _proj_out_kernel,
        grid=(n // tm,),
        in_specs=[pl.BlockSpec((tm, k), lambda i: (i, 0)),
                  _layer_spec((k, d), layer),
                  pl.BlockSpec((tm, d), lambda i: (i, 0)),
                  _layer_spec((1, d), layer),
                  _layer_spec((1, d), layer)],
        out_specs=[pl.BlockSpec((tm, d), lambda i: (i, 0)),
                   pl.BlockSpec((tm, d), lambda i: (i, 0))],
        out_shape=[jax.ShapeDtypeStruct((n, d), F32), jax.ShapeDtypeStruct((n, d), BF16)],
        compiler_params=_params(("arbitrary",)),
        name="proj_out",
    )(x, w, h, ln_g, ln_b)


def _mem_attn_kernel(x_ref, wq_ref, k_ref, v_ref, o_ref):
    d = x_ref.shape[2]
    hd = d // MEM_HEADS
    q = _dot(x_ref[0], wq_ref[...]) * (1.0 / math.sqrt(hd))
    for head in range(MEM_HEADS):
        cols = slice(head * hd, (head + 1) * hd)
        logits = _dot_nt(q[:, cols].astype(BF16), k_ref[0, :, cols])
        m = jnp.max(logits, axis=1, keepdims=True)
        e = jnp.exp(logits - m)
        s = jnp.sum(e, axis=1, keepdims=True)
        o = _dot(e.astype(BF16), v_ref[0, :, cols]) / s
        o_ref[0, :, cols] = o.astype(o_ref.dtype)


def _mem_attn(x, wq, km, vm, layer, tm=512):
    b, length, d = x.shape
    mlen = km.shape[1]
    return pl.pallas_call(
        _mem_attn_kernel,
        grid=(b, length // tm),
        in_specs=[pl.BlockSpec((1, tm, d), lambda i, j: (i, j, 0)),
                  _layer_spec((d, d), layer),
                  pl.BlockSpec((1, mlen, d), lambda i, j: (i, 0, 0)),
                  pl.BlockSpec((1, mlen, d), lambda i, j: (i, 0, 0))],
        out_specs=pl.BlockSpec((1, tm, d), lambda i, j: (i, j, 0)),
        out_shape=jax.ShapeDtypeStruct((b, length, d), BF16),
        compiler_params=_params(("arbitrary", "arbitrary")),
        name="mem_attn",
    )(x, wq, km, vm)


def _split_bf16(a):
    hi = a.astype(BF16)
    lo = (a - hi.astype(F32)).astype(BF16)
    return hi, lo


def _router_kernel(x_ref, wt_ref, bias_ref, idx_ref, gate_ref, mask_ref, cnt_ref):
    tm = x_ref.shape[0]
    xh, xl = _split_bf16(x_ref[...])
    wh, wl = _split_bf16(wt_ref[...])
    logits = _dot_nt(wh, xh) + (_dot_nt(wh, xl) + _dot_nt(wl, xh))
    scores = jax.nn.sigmoid(logits)
    sel = scores + bias_ref[...]

    member = lax.broadcasted_iota(jnp.int32, (GROUP_SIZE, tm), 0)
    group_scores = []
    for g in range(N_GROUPS):
        v = sel[g * GROUP_SIZE:(g + 1) * GROUP_SIZE, :]
        m1 = jnp.max(v, axis=0, keepdims=True)
        first = jnp.min(jnp.where(v == m1, member, GROUP_SIZE), axis=0, keepdims=True)
        m2 = jnp.max(jnp.where(member == first, -jnp.inf, v), axis=0, keepdims=True)
        group_scores.append(m1 + m2)
    masked = []
    for g in range(N_GROUPS):
        beaten_by = jnp.zeros((1, tm), jnp.int32)
        for o in range(N_GROUPS):
            if o == g:
                continue
            wins = group_scores[o] > group_scores[g]
            if o < g:
                wins = jnp.logical_or(wins, group_scores[o] == group_scores[g])
            beaten_by = beaten_by + wins.astype(jnp.int32)
        keep = beaten_by < TOPK_GROUPS
        masked.append(jnp.where(keep, sel[g * GROUP_SIZE:(g + 1) * GROUP_SIZE, :], -jnp.inf))
    cand = jnp.concatenate(masked, axis=0)

    expert = lax.broadcasted_iota(jnp.int32, (N_EXPERTS, tm), 0)
    beaten_by = jnp.zeros((N_EXPERTS, tm), jnp.int32)
    for o in range(N_EXPERTS):
        other = cand[o:o + 1, :]
        wins = jnp.logical_or(other > cand, jnp.logical_and(other == cand, expert > o))
        beaten_by = beaten_by + wins.astype(jnp.int32)
    chosen = beaten_by < TOP_K
    chosen_i = chosen.astype(jnp.int32)
    mask_ref[...] = chosen_i
    cnt_ref[...] = jnp.broadcast_to(jnp.sum(chosen_i, axis=1, keepdims=True), cnt_ref.shape)

    picked = jnp.where(chosen, scores, 0.0)
    gate = picked / jnp.sum(picked, axis=0, keepdims=True) * ROUTE_SCALE

    running = jnp.zeros((1, tm), jnp.int32)
    slots = []
    for g in range(N_GROUPS):
        part = chosen_i[g * GROUP_SIZE:(g + 1) * GROUP_SIZE, :]
        rows_ = []
        for r in range(GROUP_SIZE):
            rows_.append(running)
            running = running + part[r:r + 1, :]
        slots.append(jnp.concatenate(rows_, axis=0))
    slot = jnp.concatenate(slots, axis=0)
    idx_rows = []
    gate_rows = []
    for k in range(TOP_K):
        hit = jnp.logical_and(chosen, slot == k)
        idx_rows.append(jnp.sum(jnp.where(hit, expert, 0), axis=0, keepdims=True))
        gate_rows.append(jnp.sum(jnp.where(hit, gate, 0.0), axis=0, keepdims=True))
    idx_ref[...] = jnp.concatenate(idx_rows, axis=0)
    gate_ref[...] = jnp.concatenate(gate_rows, axis=0)


def _router(x, w_router_t, bias, tm=512):
    n, d = x.shape
    return pl.pallas_call(
        _router_kernel,
        grid=(n // tm,),
        in_specs=[pl.BlockSpec((tm, d), lambda i: (i, 0)),
                  pl.BlockSpec((N_EXPERTS, d), lambda i: (0, 0)),
                  pl.BlockSpec((N_EXPERTS, 1), lambda i: (0, 0))],
        out_specs=[pl.BlockSpec((TOP_K, tm), lambda i: (0, i)),
                   pl.BlockSpec((TOP_K, tm), lambda i: (0, i)),
                   pl.BlockSpec((N_EXPERTS, tm), lambda i: (0, i)),
                   pl.BlockSpec((None, N_EXPERTS, LANES), lambda i: (i, 0, 0))],
        out_shape=[jax.ShapeDtypeStruct((TOP_K, n), jnp.int32),
                   jax.ShapeDtypeStruct((TOP_K, n), F32),
                   jax.ShapeDtypeStruct((N_EXPERTS, n), jnp.int32),
                   jax.ShapeDtypeStruct((n // tm, N_EXPERTS, LANES), jnp.int32)],
        compiler_params=_params(("arbitrary",)),
        name="router",
    )(x, w_router_t, bias.reshape(N_EXPERTS, 1))


def _plan_kernel(mask_ref, idx_ref, off_ref, pos_ref):
    tm = mask_ref.shape[1]
    before = (lax.broadcasted_iota(jnp.int32, (tm, tm), 0)
              < lax.broadcasted_iota(jnp.int32, (tm, tm), 1)).astype(BF16)
    rank = _dot(mask_ref[...].astype(BF16), before).astype(jnp.int32)
    row_of = off_ref[...] + rank
    expert = lax.broadcasted_iota(jnp.int32, (N_EXPERTS, tm), 0)
    rows_ = []
    for k in range(TOP_K):
        hit = expert == idx_ref[k:k + 1, :]
        rows_.append(jnp.sum(jnp.where(hit, row_of, 0), axis=0, keepdims=True))
    pos_ref[...] = jnp.concatenate(rows_, axis=0)


def _plan(mask_t, idx_t, tile_offsets, tm):
    n = mask_t.shape[1]
    return pl.pallas_call(
        _plan_kernel,
        grid=(n // tm,),
        in_specs=[pl.BlockSpec((N_EXPERTS, tm), lambda i: (0, i)),
                  pl.BlockSpec((TOP_K, tm), lambda i: (0, i)),
                  pl.BlockSpec((None, N_EXPERTS, 1), lambda i: (i, 0, 0))],
        out_specs=pl.BlockSpec((TOP_K, tm), lambda i: (0, i)),
        out_shape=jax.ShapeDtypeStruct((TOP_K, n), jnp.int32),
        compiler_params=_params(("arbitrary",)),
        name="plan",
    )(mask_t, idx_t, tile_offsets)


def _block_layout(counts_per_tile, n):
    rows = MOE_ROWS
    nb = (n * TOP_K) // rows + N_EXPERTS
    counts = jnp.sum(counts_per_tile, axis=0)
    padded = (counts + rows - 1) // rows * rows
    padded_ends = jnp.cumsum(padded)
    padded_starts = padded_ends - padded
    tile_offsets = padded_starts[None, :] + jnp.cumsum(counts_per_tile, axis=0) - counts_per_tile
    used = (padded_ends[-1] // rows).astype(jnp.int32)
    block_start = jnp.arange(nb, dtype=jnp.int32) * rows
    block_expert = jnp.minimum(
        jnp.sum((padded_ends[None, :] <= block_start[:, None]).astype(jnp.int32), axis=1),
        N_EXPERTS - 1)
    last = block_expert[jnp.maximum(used - 1, 0)]
    block_expert = jnp.where(jnp.arange(nb) < used, block_expert, last)
    pad_from = (padded_starts + counts).astype(jnp.int32)
    return (tile_offsets.astype(jnp.int32)[:, :, None], block_expert, used.reshape(1),
            pad_from, padded_ends.astype(jnp.int32))


DISPATCH_ROWS = 128


def _pack_rows(x):
    k = x.shape[1] // 2
    lo = pltpu.bitcast(x[:, :k].astype(BF16).astype(F32), jnp.uint32) >> 16
    hi = pltpu.bitcast(x[:, k:].astype(BF16).astype(F32), jnp.uint32) & jnp.uint32(0xFFFF0000)
    return lo | hi


def _unpack_rows(w):
    lo = pltpu.bitcast(w << 16, F32)
    hi = pltpu.bitcast(w & jnp.uint32(0xFFFF0000), F32)
    return lo, hi


def _dispatch_kernel(pad_from_ref, pad_to_ref, used_ref, pos_hbm, x_ref, xs_hbm, pos_smem,
                     xbuf, zeros, row_sem, pos_sem, pad_sem):
    i = pl.program_id(0)
    steps = pl.num_programs(0)
    tm = pos_smem.shape[2]
    slot = i % 3

    def wait_rows(s):
        for _ in range(TOP_K):
            pltpu.make_async_copy(xbuf.at[s], xs_hbm.at[pl.ds(0, tm)], row_sem.at[s]).wait()

    def pos_copy(blk):
        return pltpu.make_async_copy(pos_hbm.at[blk], pos_smem.at[blk % 2], pos_sem.at[blk % 2])

    @pl.when(i == 0)
    def _():
        pos_copy(0).start()
        zeros[...] = jnp.zeros_like(zeros)
        block = zeros.shape[0]

        def tail_copy(blk):
            start = pl.multiple_of(blk * block, block)
            return pltpu.make_async_copy(zeros, xs_hbm.at[pl.ds(start, block)], pad_sem)

        def tail_start(blk, c):
            tail_copy(blk).start()
            return c

        def tail_wait(blk, c):
            tail_copy(blk).wait()
            return c

        lax.fori_loop(used_ref[0], xs_hbm.shape[0] // block, tail_start, 0)
        lax.fori_loop(used_ref[0], xs_hbm.shape[0] // block, tail_wait, 0)

        def pad_walk(start_not_wait):
            def act(cp):
                if start_not_wait:
                    cp.start()
                else:
                    cp.wait()

            def per_expert(e, carry):
                lo = pad_from_ref[e]
                hi = pad_to_ref[e]
                lo8 = jnp.minimum((lo + SUBLANES - 1) // SUBLANES * SUBLANES, hi)

                def one_row(r, c):
                    act(pltpu.make_async_copy(zeros.at[pl.ds(0, 1)], xs_hbm.at[pl.ds(r, 1)],
                                              pad_sem))
                    return c

                lax.fori_loop(lo, lo8, one_row, 0)
                rem = hi - lo8
                off = lo8
                size = block // 2
                while size >= SUBLANES:
                    @pl.when((rem & size) != 0)
                    def _(off=off, size=size):
                        dst = xs_hbm.at[pl.ds(pl.multiple_of(off, SUBLANES), size)]
                        act(pltpu.make_async_copy(zeros.at[pl.ds(0, size)], dst, pad_sem))
                    off = off + (rem & size)
                    size //= 2
                return carry

            lax.fori_loop(0, N_EXPERTS, per_expert, 0)

        pad_walk(True)
        pad_walk(False)

    @pl.when(i >= 3)
    def _():
        wait_rows(slot)

    @pl.when(i + 1 < steps)
    def _():
        pos_copy(i + 1).start()

    xbuf[slot] = _pack_rows(x_ref[...])
    pos_copy(i).wait()
    pslot = i % 2

    for r in range(tm):
        for k in range(TOP_K):
            pltpu.make_async_copy(xbuf.at[slot, pl.ds(r, 1)],
                                  xs_hbm.at[pl.ds(pos_smem[pslot, k, r], 1)],
                                  row_sem.at[slot]).start()

    @pl.when(i == steps - 1)
    def _():
        wait_rows(slot)

        @pl.when(steps > 1)
        def _():
            wait_rows((i + 2) % 3)

        @pl.when(steps > 2)
        def _():
            wait_rows((i + 1) % 3)


def _dispatch(pos, x, pad_from, pad_to, used, total_rows):
    n, d = x.shape
    tm = DISPATCH_ROWS
    grid_spec = pltpu.PrefetchScalarGridSpec(
        num_scalar_prefetch=3,
        grid=(n // tm,),
        in_specs=[pl.BlockSpec(memory_space=pl.ANY),
                  pl.BlockSpec((tm, d), lambda i, *_: (i, 0))],
        out_specs=pl.BlockSpec(memory_space=pl.ANY),
        scratch_shapes=[pltpu.SMEM((2, TOP_K, tm), jnp.int32),
                        pltpu.VMEM((3, tm, d // 2), jnp.uint32),
                        pltpu.VMEM((MOE_ROWS, d // 2), jnp.uint32),
                        pltpu.SemaphoreType.DMA((3,)),
                        pltpu.SemaphoreType.DMA((2,)),
                        pltpu.SemaphoreType.DMA],
    )
    return pl.pallas_call(
        _dispatch_kernel,
        grid_spec=grid_spec,
        out_shape=jax.ShapeDtypeStruct((total_rows, d // 2), jnp.uint32),
        compiler_params=_params(("arbitrary",)),
        name="dispatch",
    )(pad_from, pad_to, used, pos.reshape(TOP_K, n // tm, tm).transpose(1, 0, 2), x)


def _experts_kernel(be_ref, used_ref, x_ref, wg_ref, wu_ref, wd_ref, y_ref,
                    wg_bf, wu_bf, wd_bf):
    i = pl.program_id(0)
    used = used_ref[0]

    @pl.when(i < used)
    def _():
        changed = jnp.logical_or(i == 0, be_ref[jnp.maximum(i - 1, 0)] != be_ref[i])

        @pl.when(changed)
        def _():
            wg_bf[...] = wg_ref[...].astype(BF16)
            wu_bf[...] = wu_ref[...].astype(BF16)
            wd_bf[...] = wd_ref[...].astype(BF16)

        half = x_ref.shape[1]
        x_lo, x_hi = (t.astype(BF16) for t in _unpack_rows(x_ref[...]))
        gate = _dot(x_lo, wg_bf[:half, :]) + _dot(x_hi, wg_bf[half:, :])
        up = _dot(x_lo, wu_bf[:half, :]) + _dot(x_hi, wu_bf[half:, :])
        hid = jax.nn.silu(gate) * up
        y_ref[...] = _pack_rows(_dot(hid.astype(BF16), wd_bf[...]))

    @pl.when(i >= used)
    def _():
        y_ref[...] = jnp.zeros_like(y_ref)


def _experts(block_expert, used, xs, w_gate, w_up, w_down, layer):
    nb = block_expert.shape[0]
    rows = MOE_ROWS
    d = w_gate.shape[2]
    de = w_gate.shape[3]
    grid_spec = pltpu.PrefetchScalarGridSpec(
        num_scalar_prefetch=2,
        grid=(nb,),
        in_specs=[
            pl.BlockSpec((rows, d // 2), lambda i, be, u: (jnp.minimum(i, u[0] - 1), 0)),
            pl.BlockSpec((None, None, d, de), lambda i, be, u: (layer, be[i], 0, 0)),
            pl.BlockSpec((None, None, d, de), lambda i, be, u: (layer, be[i], 0, 0)),
            pl.BlockSpec((None, None, de, d), lambda i, be, u: (layer, be[i], 0, 0)),
        ],
        out_specs=pl.BlockSpec((rows, d // 2), lambda i, be, u: (i, 0)),
        scratch_shapes=[
            pltpu.VMEM((d, de), BF16),
            pltpu.VMEM((d, de), BF16),
            pltpu.VMEM((de, d), BF16),
        ],
    )
    return pl.pallas_call(
        _experts_kernel,
        grid_spec=grid_spec,
        out_shape=jax.ShapeDtypeStruct((nb * rows, d // 2), jnp.uint32),
        compiler_params=_params(("arbitrary",)),
        name="experts",
    )(block_expert, used, xs, w_gate, w_up, w_down)


def _combine_kernel(pos_hbm, y_hbm, gate_ref, h_ref, wsg_ref, wsu_ref, wsd_ref, g_ref, b_ref,
                    *rest):
    if len(rest) == 11:
        wa_ref, wb_ref, o_ref, pa_ref, pb_ref = rest[:5]
    else:
        wa_ref = wb_ref = pa_ref = pb_ref = None
        o_ref = rest[0]
    ybuf0, ybuf1, ybuf2, pos_smem, row_sem, pos_sem = rest[-6:]
    i = pl.program_id(0)
    steps = pl.num_programs(0)
    tm = h_ref.shape[0]
    last = steps - 1
    ybufs = (ybuf0, ybuf1, ybuf2)

    def pos_copy(t):
        return pltpu.make_async_copy(pos_hbm.at[jnp.minimum(t, last)], pos_smem.at[t % 2],
                                     pos_sem.at[t % 2])

    def row_copy(t, buf, k, r):
        return pltpu.make_async_copy(y_hbm.at[pl.ds(pos_smem[t % 2, k, r], 1)],
                                     ybufs[buf].at[k, pl.ds(r, 1)], row_sem.at[buf])

    def wait_rows(buf):
        for k in range(TOP_K):
            pltpu.make_async_copy(y_hbm.at[pl.ds(0, tm)], ybufs[buf].at[k],
                                  row_sem.at[buf]).wait()

    @pl.when(i == 0)
    def _():
        for t in range(2):
            pos_copy(t).start()
            pos_copy(t).wait()

            def issue(r, carry, t=t):
                for k in range(TOP_K):
                    row_copy(t, t, k, r).start()
                return carry
            lax.fori_loop(0, tm, issue, 0)

    def step(cur, ahead):
        wait_rows(cur)

        @pl.when(i > 0)
        def _():
            pos_copy(i + 2).wait()

        @pl.when(i == 0)
        def _():
            pos_copy(2).start()
            pos_copy(2).wait()

        for r in range(tm):
            for k in range(TOP_K):
                row_copy(i + 2, ahead, k, r).start()
        pos_copy(i + 3).start()

        h = h_ref[...]
        x = h.astype(BF16)
        hid = jax.nn.silu(_dot(x, wsg_ref[...])) * _dot(x, wsu_ref[...])
        y = ALPHA * h + _dot(hid.astype(BF16), wsd_ref[...])

        gates = gate_ref[...]
        half = ybufs[cur].shape[2]
        routed_lo = jnp.zeros((tm, half), F32)
        routed_hi = jnp.zeros((tm, half), F32)
        for k in range(TOP_K):
            lo, hi = _unpack_rows(ybufs[cur][k])
            routed_lo = routed_lo + lo * gates[:, k:k + 1]
            routed_hi = routed_hi + hi * gates[:, k:k + 1]
        y = y + jnp.concatenate([routed_lo, routed_hi], axis=1)
        out = _layer_norm(y, g_ref[...], b_ref[...])
        o_ref[...] = out
        if wa_ref is not None:
            out_bf = out.astype(BF16)
            pa_ref[...] = _dot(out_bf, wa_ref[...]).astype(pa_ref.dtype)
            pb_ref[...] = _dot(out_bf, wb_ref[...]).astype(pb_ref.dtype)

        @pl.when(i == last)
        def _():
            wait_rows((cur + 1) % 3)
            wait_rows(ahead)
            pos_copy(i + 3).wait()

    for phase in range(3):
        @pl.when(i % 3 == phase)
        def _(phase=phase):
            step(phase, (phase + 2) % 3)


def _combine(pos, y_rows, gates, h, ws_gate, ws_up, ws_down, ln_g, ln_b, layer, next_proj=None):
    n, d = h.shape
    tm = COMBINE_ROWS
    ds = ws_gate.shape[2]
    in_specs = [pl.BlockSpec(memory_space=pl.ANY),
                pl.BlockSpec(memory_space=pl.ANY),
                pl.BlockSpec((tm, TOP_K), lambda i: (i, 0)),
                pl.BlockSpec((tm, d), lambda i: (i, 0)),
                _layer_spec((d, ds), layer),
                _layer_spec((d, ds), layer),
                _layer_spec((ds, d), layer),
                _layer_spec((1, d), layer),
                _layer_spec((1, d), layer)]
    out_specs = [pl.BlockSpec((tm, d), lambda i: (i, 0))]
    out_shape = [jax.ShapeDtypeStruct((n, d), F32)]
    operands = [pos.reshape(TOP_K, n // tm, tm).transpose(1, 0, 2), y_rows, gates, h,
                ws_gate, ws_up, ws_down, ln_g, ln_b]
    if next_proj is not None:
        for w, dtype in zip(next_proj, (BF16, F32)):
            width = w.shape[2]
            in_specs.append(_layer_spec((d, width), layer + 1))
            out_specs.append(pl.BlockSpec((tm, width), lambda i: (i, 0)))
            out_shape.append(jax.ShapeDtypeStruct((n, width), dtype))
            operands.append(w)
    return pl.pallas_call(
        _combine_kernel,
        grid=(n // tm,),
        in_specs=in_specs,
        out_specs=out_specs,
        out_shape=out_shape,
        scratch_shapes=[pltpu.VMEM((TOP_K, tm, d // 2), jnp.uint32)] * 3
        + [pltpu.SMEM((2, TOP_K, tm), jnp.int32),
           pltpu.SemaphoreType.DMA((3,)),
           pltpu.SemaphoreType.DMA((2,))],
        compiler_params=_params(("arbitrary",)),
        name="combine",
    )(*operands)


def _alibi_slopes(n):
    return 2.0 ** (-8.0 * jnp.arange(1, n + 1, dtype=F32) / n)


def kernel(x, mem, w_in, a_sinks, g_a, g_b, w_out, ln1_g, ln1_b, wq_m, wk_m, wv_m, wo_m,
           ln2_g, ln2_b, w_router, router_bias, w_gate, w_up, w_down, ws_gate, ws_up, ws_down,
           ln3_g, ln3_b):
    b, length, d = x.shape
    n = b * length
    mlen = mem.shape[1]
    depth = w_in.shape[0]
    h = x.reshape(n, d)
    h_bf = h.astype(BF16)
    mem_bf = mem.reshape(b * mlen, d).astype(BF16)
    slopes_a = _alibi_slopes(A_Q_HEADS)
    slopes_b = _alibi_slopes(B_HEADS)

    w_in_bf = w_in.astype(BF16)
    w_in_a = w_in_bf[:, :, :A_PROJ_WIDTH]
    w_in_b = w_in_bf[:, :, A_PROJ_WIDTH:]
    w_out_bf = w_out.astype(BF16)
    wq_bf, wk_bf, wv_bf, wo_bf = (w.astype(BF16) for w in (wq_m, wk_m, wv_m, wo_m))
    wsg_bf, wsu_bf, wsd_bf = (w.astype(BF16) for w in (ws_gate, ws_up, ws_down))
    row = lambda p: p.reshape(depth, 1, p.shape[1])
    g_b3, ln1_g3, ln1_b3, ln2_g3, ln2_b3, ln3_g3, ln3_b3 = (
        row(p) for p in (g_b, ln1_g, ln1_b, ln2_g, ln2_b, ln3_g, ln3_b))

    proj_a = _matmul(h_bf, w_in_a, 0, 512, A_PROJ_WIDTH, BF16)
    proj_b = _matmul(h_bf, w_in_b, 0, 512, B_WIDTH, F32)
    for l in range(depth):
        mixed_a = _attn_a(proj_a.reshape(b, length, A_PROJ_WIDTH), slopes_a, a_sinks[l], g_a[l])
        out_b = _attn_b(proj_b.reshape(b, length, 3 * B_WIDTH), slopes_b)
        h, h_bf = _mix_out(mixed_a.reshape(n, A_WIDTH), out_b.reshape(n, B_WIDTH), g_b3,
                           w_out_bf, h, ln1_g3, ln1_b3, l)

        km = _matmul(mem_bf, wk_bf, l, b * mlen, d // 2, BF16)
        vm = _matmul(mem_bf, wv_bf, l, b * mlen, d // 2, BF16)
        o = _mem_attn(h_bf.reshape(b, length, d), wq_bf,
                      km.reshape(b, mlen, d), vm.reshape(b, mlen, d), l)
        h, h_bf = _proj_out(o.reshape(n, d), wo_bf, h, ln2_g3, ln2_b3, l)

        route_tm = 512
        idx_t, gate_t, mask_t, cnt = _router(h, w_router[l].T, router_bias[l], route_tm)
        tile_offsets, block_expert, used, pad_from, pad_to = _block_layout(cnt[:, :, 0], n)
        pos = _plan(mask_t, idx_t, tile_offsets, route_tm)
        xs = _dispatch(pos, h, pad_from, pad_to, used, block_expert.shape[0] * MOE_ROWS)
        y_rows = _experts(block_expert, used, xs, w_gate, w_up, w_down, l)
        next_proj = (w_in_a, w_in_b) if l + 1 < depth else None
        outs = _combine(pos, y_rows, gate_t.T, h, wsg_bf, wsu_bf, wsd_bf, ln3_g3, ln3_b3, l,
                        next_proj)
        h = outs[0]
        if next_proj is not None:
            proj_a, proj_b = outs[1:]
    return h.reshape(b, length, d)
```

```python
import math

import jax
import jax.numpy as jnp
from jax import lax
from jax.experimental import pallas as pl
from jax.experimental.pallas import tpu as pltpu

F32 = jnp.float32
BF16 = jnp.bfloat16

HEAD_DIM = 64
A_Q_HEADS = 16
A_KV_HEADS = 2
A_WINDOW = 128
B_HEADS = 16
B_BRANCH_DILATIONS = (16, 4, 1)
A_WIDTH = A_Q_HEADS * HEAD_DIM
A_KV_WIDTH = A_KV_HEADS * HEAD_DIM
B_WIDTH = B_HEADS * HEAD_DIM
A_PROJ_WIDTH = A_WIDTH + 2 * A_KV_WIDTH
BLK = 128

MEM_HEADS = 4

N_EXPERTS = 64
TOP_K = 8
N_GROUPS = 8
GROUP_SIZE = N_EXPERTS // N_GROUPS
TOPK_GROUPS = 4
ROUTE_SCALE = 2.5

DEPTH = 2
ALPHA = (2.0 * DEPTH) ** 0.25
LN_EPS = 1e-5
RMS_EPS = 1e-6

LANES = 128
SUBLANES = 8
NEG = -1e30

MOE_ROWS = 512
COMBINE_ROWS = 128
VMEM_LIMIT = 56 * 1024 * 1024


def _params(sem, vmem=VMEM_LIMIT):
    return pltpu.CompilerParams(dimension_semantics=sem, vmem_limit_bytes=vmem)


def _dot(a, b):
    return jnp.dot(a, b, preferred_element_type=F32)


def _dot_nt(a, b):
    return lax.dot_general(a, b, (((1,), (1,)), ((), ())), preferred_element_type=F32)


def _layer_norm(y, g, b):
    mu = jnp.mean(y, axis=-1, keepdims=True)
    yc = y - mu
    var = jnp.mean(yc * yc, axis=-1, keepdims=True)
    return yc * lax.rsqrt(var + LN_EPS) * g + b


def _layer_spec(shape, layer, index=None, single=True):
    index = index or (0,) * len(shape)
    mode = dict(pipeline_mode=pl.Buffered(1)) if single else {}
    return pl.BlockSpec((None,) + tuple(shape), lambda *_: (layer,) + tuple(index), **mode)


def _mm_kernel(x_ref, w_ref, o_ref):
    o_ref[...] = _dot(x_ref[...], w_ref[...]).astype(o_ref.dtype)


def _matmul(x, w, layer, tm, tn, out_dtype):
    m, k = x.shape
    n = w.shape[2]
    return pl.pallas_call(
        _mm_kernel,
        grid=(n // tn, m // tm),
        in_specs=[pl.BlockSpec((tm, k), lambda j, i: (i, 0)),
                  pl.BlockSpec((None, k, tn), lambda j, i: (layer, 0, j))],
        out_specs=pl.BlockSpec((tm, tn), lambda j, i: (i, j)),
        out_shape=jax.ShapeDtypeStruct((m, n), out_dtype),
        compiler_params=_params(("arbitrary", "arbitrary")),
        name="matmul",
    )(x, w)


def _band_bias(stacked_heads, max_offset, slope_of_stack):
    rows = stacked_heads * BLK
    row = lax.broadcasted_iota(jnp.int32, (rows, 2 * BLK), 0)
    col = lax.broadcasted_iota(jnp.int32, (rows, 2 * BLK), 1)
    offset = (row & (BLK - 1)) + BLK - col
    slope = jnp.full((rows, 2 * BLK), slope_of_stack[stacked_heads - 1], F32)
    for a in range(stacked_heads - 2, -1, -1):
        slope = jnp.where(row < (a + 1) * BLK, slope_of_stack[a], slope)
    valid = jnp.logical_and(offset >= 0, offset <= max_offset)
    bias = jnp.where(valid, -slope * offset.astype(F32), NEG)
    return bias, jnp.where(col < BLK, NEG, bias)


def _roll_half(t):
    return pltpu.roll(t.astype(F32), LANES // 2, axis=1).astype(BF16)


A_STACK = 4


def _attn_a_kernel(slopes_ref, sinks_ref, q_ref, kc_ref, kp_ref, vc_ref, vp_ref, g_ref,
                   o_ref, o_scr, bias_scr):
    n = pl.program_id(1)
    tiles_per_kv = A_Q_HEADS // A_KV_HEADS // 2

    def head_of(kv, half, a):
        return 2 * (kv * tiles_per_kv + a) + half

    @pl.when(jnp.logical_and(pl.program_id(0) == 0, n == 0))
    def _():
        for kv in range(A_KV_HEADS):
            for half in range(2):
                slopes = [slopes_ref[head_of(kv, half, a)] for a in range(A_STACK)]
                bias, bias_first = _band_bias(A_STACK, A_WINDOW - 1, slopes)
                bias_scr[0, 2 * kv + half] = bias
                bias_scr[1, 2 * kv + half] = bias_first

    first = (n == 0).astype(jnp.int32)
    lane = lax.broadcasted_iota(jnp.int32, (BLK, LANES), 1)
    lo = lane < HEAD_DIM
    srow = lax.broadcasted_iota(jnp.int32, (A_STACK * BLK, 1), 0)

    k_ver = (jnp.concatenate([kp_ref[0], kc_ref[0]], axis=0),
             jnp.concatenate([_roll_half(kp_ref[0]), _roll_half(kc_ref[0])], axis=0))
    v_ver = (jnp.concatenate([vp_ref[0], vc_ref[0]], axis=0),
             jnp.concatenate([_roll_half(vp_ref[0]), _roll_half(vc_ref[0])], axis=0))

    ss = jnp.zeros((BLK, 1), F32)
    for kv in range(A_KV_HEADS):
        outs = []
        for half in range(2):
            keep = lo if half == 0 else jnp.logical_not(lo)
            parts = []
            for a in range(A_STACK):
                t = kv * tiles_per_kv + a
                qt = q_ref[0, :, t * LANES:(t + 1) * LANES] * jnp.asarray(0.125, BF16)
                parts.append(jnp.where(keep, qt, jnp.zeros_like(qt)))
            qs = jnp.concatenate(parts, axis=0)
            ver = kv ^ half
            logits = _dot_nt(qs, k_ver[ver]) + bias_scr[first, 2 * kv + half]
            sink = jnp.full((A_STACK * BLK, 1), sinks_ref[head_of(kv, half, A_STACK - 1)], F32)
            for a in range(A_STACK - 2, -1, -1):
                sink = jnp.where(srow < (a + 1) * BLK, sinks_ref[head_of(kv, half, a)], sink)
            m = jnp.maximum(jnp.max(logits, axis=1, keepdims=True), sink)
            e = jnp.exp(logits - m)
            s = jnp.sum(e, axis=1, keepdims=True) + jnp.exp(sink - m)
            acc = _dot(e.astype(BF16), v_ver[ver])
            outs.append(acc * (1.0 / s))
        for a in range(A_STACK):
            t = kv * tiles_per_kv + a
            o_tile = jnp.where(lo, outs[0][a * BLK:(a + 1) * BLK], outs[1][a * BLK:(a + 1) * BLK])
            ss = ss + jnp.sum(o_tile * o_tile, axis=1, keepdims=True)
            o_scr[:, t * LANES:(t + 1) * LANES] = o_tile
    scale = lax.rsqrt(ss * (1.0 / A_WIDTH) + RMS_EPS)
    o_ref[0] = (o_scr[...] * scale * g_ref[...]).astype(o_ref.dtype)


def _attn_a(proj_a, slopes, sinks, g_a):
    b, length, _ = proj_a.shape
    kcol = A_WIDTH // LANES
    vcol = kcol + 1
    grid_spec = pltpu.PrefetchScalarGridSpec(
        num_scalar_prefetch=2,
        grid=(b, length // BLK),
        in_specs=[
            pl.BlockSpec((1, BLK, A_WIDTH), lambda i, n, *_: (i, n, 0)),
            pl.BlockSpec((1, BLK, LANES), lambda i, n, *_: (i, n, kcol)),
            pl.BlockSpec((1, BLK, LANES), lambda i, n, *_: (i, jnp.maximum(n - 1, 0), kcol)),
            pl.BlockSpec((1, BLK, LANES), lambda i, n, *_: (i, n, vcol)),
            pl.BlockSpec((1, BLK, LANES), lambda i, n, *_: (i, jnp.maximum(n - 1, 0), vcol)),
            pl.BlockSpec((1, A_WIDTH), lambda i, n, *_: (0, 0)),
        ],
        out_specs=pl.BlockSpec((1, BLK, A_WIDTH), lambda i, n, *_: (i, n, 0)),
        scratch_shapes=[pltpu.VMEM((BLK, A_WIDTH), F32),
                        pltpu.VMEM((2, 2 * A_KV_HEADS, A_STACK * BLK, 2 * BLK), F32)],
    )
    return pl.pallas_call(
        _attn_a_kernel,
        grid_spec=grid_spec,
        out_shape=jax.ShapeDtypeStruct((b, length, A_WIDTH), BF16),
        compiler_params=_params(("arbitrary", "arbitrary")),
        name="attn_a",
    )(slopes, sinks, proj_a, proj_a, proj_a, proj_a, proj_a, g_a.reshape(1, A_WIDTH))


B_PIPE_DEPTH = 4


def _attn_b_kernel(slopes_ref, q_ref, k_ref, v_ref, o_ref, acc_scr, m_scr, s_scr, bias_scr,
                   *stage_scrs):
    l_scrs = stage_scrs[:B_PIPE_DEPTH]
    mx_scrs = stage_scrs[B_PIPE_DEPTH:]
    hp = pl.program_id(1)
    length = q_ref.shape[1]
    lane = lax.broadcasted_iota(jnp.int32, (BLK, LANES), 1)
    lo = lane < HEAD_DIM
    qrow = lax.broadcasted_iota(jnp.int32, (2 * BLK, LANES), 0)
    qlane = lax.broadcasted_iota(jnp.int32, (2 * BLK, LANES), 1)
    qmask = (qrow >= BLK) == (qlane >= HEAD_DIM)

    for branch, dil in enumerate(B_BRANCH_DILATIONS):
        slopes = [slopes_ref[2 * hp] * float(dil), slopes_ref[2 * hp + 1] * float(dil)]
        bias, bias_first = _band_bias(2, BLK, slopes)
        bias_scr[branch, 0] = bias
        bias_scr[branch, 1] = bias_first

    for branch, dil in enumerate(B_BRANCH_DILATIONS):
        blocks_per_class = length // (dil * BLK)

        def rows(start, dil=dil):
            if dil == 1:
                return pl.ds(start, BLK)
            return pl.ds(start, BLK, stride=dil)

        def block_rows(it, dil=dil, blocks_per_class=blocks_per_class, rows=rows):
            res = it // blocks_per_class
            blk = it % blocks_per_class
            cur = rows(res + dil * BLK * blk)
            prev = rows(res + dil * BLK * jnp.maximum(blk - 1, 0))
            return cur, prev, jnp.where(blk == 0, 1, 0)

        def logits_stage(it, l_ref, m_ref, branch=branch, block_rows=block_rows):
            cur, prev, first = block_rows(it)
            q2 = q_ref[0, cur, :] * 0.125
            qs = jnp.concatenate([q2, q2], axis=0)
            qs = jnp.where(qmask, qs, 0.0).astype(BF16)
            k2 = jnp.concatenate([k_ref[0, prev, :], k_ref[0, cur, :]], axis=0).astype(BF16)
            logits = _dot_nt(qs, k2) + bias_scr[branch, first]
            l_ref[...] = logits
            m_ref[...] = jnp.broadcast_to(jnp.max(logits, axis=1, keepdims=True), m_ref.shape)

        def value_stage(it, l_ref, m_ref, branch=branch, block_rows=block_rows):
            cur, prev, _ = block_rows(it)
            v2 = jnp.concatenate([v_ref[0, prev, :], v_ref[0, cur, :]], axis=0).astype(BF16)
            m = m_ref[...]
            e = jnp.exp(l_ref[...] - jnp.concatenate([m, m], axis=1))
            s = jnp.sum(e, axis=1, keepdims=True)
            acc = _dot(e.astype(BF16), v2)
            m_t = jnp.where(lo, m[:BLK], m[BLK:])
            s_t = jnp.where(lo, s[:BLK], s[BLK:])
            acc_t = jnp.where(lo, acc[:BLK], acc[BLK:])
            if branch == 0:
                m_scr[cur, :] = m_t
                s_scr[cur, :] = s_t
                acc_scr[cur, :] = acc_t
            else:
                m_old = m_scr[cur, :]
                s_old = s_scr[cur, :]
                acc_old = acc_scr[cur, :]
                m_new = jnp.maximum(m_old, m_t)
                w_old = jnp.exp(m_old - m_new)
                w_blk = jnp.exp(m_t - m_new)
                m_scr[cur, :] = m_new
                s_scr[cur, :] = s_old * w_old + s_t * w_blk
                acc_scr[cur, :] = acc_old * w_old + acc_t * w_blk

        n_blocks = length // BLK
        depth = len(l_scrs)
        for u in range(depth):
            logits_stage(u, l_scrs[u], mx_scrs[u])

        def body(j, carry, logits_stage=logits_stage, value_stage=value_stage, depth=depth):
            for u in range(depth):
                value_stage(depth * j + u, l_scrs[u], mx_scrs[u])
            for u in range(depth):
                logits_stage(depth * (j + 1) + u, l_scrs[u], mx_scrs[u])
            return carry

        lax.fori_loop(0, n_blocks // depth - 1, body, 0)
        for u in range(depth):
            value_stage(n_blocks - depth + u, l_scrs[u], mx_scrs[u])

    o_ref[0] = acc_scr[...] / s_scr[...]


def _attn_b(proj_b, slopes):
    b, length, _ = proj_b.shape
    pairs = B_WIDTH // LANES
    grid_spec = pltpu.PrefetchScalarGridSpec(
        num_scalar_prefetch=1,
        grid=(b, pairs),
        in_specs=[
            pl.BlockSpec((1, length, LANES), lambda i, p, *_: (i, 0, p)),
            pl.BlockSpec((1, length, LANES), lambda i, p, *_: (i, 0, pairs + p)),
            pl.BlockSpec((1, length, LANES), lambda i, p, *_: (i, 0, 2 * pairs + p)),
        ],
        out_specs=pl.BlockSpec((1, length, LANES), lambda i, p, *_: (i, 0, p)),
        scratch_shapes=[pltpu.VMEM((length, LANES), F32)] * 3
        + [pltpu.VMEM((len(B_BRANCH_DILATIONS), 2, 2 * BLK, 2 * BLK), F32)]
        + [pltpu.VMEM((2 * BLK, 2 * BLK), F32)] * B_PIPE_DEPTH
        + [pltpu.VMEM((2 * BLK, LANES), F32)] * B_PIPE_DEPTH,
    )
    return pl.pallas_call(
        _attn_b_kernel,
        grid_spec=grid_spec,
        out_shape=jax.ShapeDtypeStruct((b, length, B_WIDTH), F32),
        compiler_params=_params(("arbitrary", "arbitrary")),
        name="attn_b",
    )(slopes, proj_b, proj_b, proj_b)


def _mix_out_kernel(xa_ref, xb_ref, gb_ref, wa_ref, wb_ref, h_ref, g_ref, b_ref,
                    o_ref, obf_ref):
    xb = xb_ref[...]
    scale = lax.rsqrt(jnp.mean(xb * xb, axis=-1, keepdims=True) + RMS_EPS)
    xb = (xb * scale * gb_ref[...]).astype(BF16)
    y = _dot(xa_ref[...], wa_ref[...]) + _dot(xb, wb_ref[...])
    out = _layer_norm(ALPHA * h_ref[...] + y, g_ref[...], b_ref[...])
    o_ref[...] = out
    obf_ref[...] = out.astype(BF16)


def _proj_out_kernel(x_ref, w_ref, h_ref, g_ref, b_ref, o_ref, obf_ref):
    y = _dot(x_ref[...], w_ref[...])
    out = _layer_norm(ALPHA * h_ref[...] + y, g_ref[...], b_ref[...])
    o_ref[...] = out
    obf_ref[...] = out.astype(BF16)


def _mix_out(xa, xb, g_b, w_out, h, ln_g, ln_b, layer, tm=512):
    n, d = h.shape
    ka = xa.shape[1]
    kb = xb.shape[1]
    return pl.pallas_call(
        _mix_out_kernel,
        grid=(n // tm,),
        in_specs=[pl.BlockSpec((tm, ka), lambda i: (i, 0)),
                  pl.BlockSpec((tm, kb), lambda i: (i, 0)),
                  _layer_spec((1, kb), layer),
                  _layer_spec((ka, d), layer, (0, 0)),
                  _layer_spec((kb, d), layer, (ka // kb, 0)),
                  pl.BlockSpec((tm, d), lambda i: (i, 0)),
                  _layer_spec((1, d), layer),
                  _layer_spec((1, d), layer)],
        out_specs=[pl.BlockSpec((tm, d), lambda i: (i, 0)),
                   pl.BlockSpec((tm, d), lambda i: (i, 0))],
        out_shape=[jax.ShapeDtypeStruct((n, d), F32), jax.ShapeDtypeStruct((n, d), BF16)],
        compiler_params=_params(("arbitrary",)),
        name="mix_out",
    )(xa, xb, g_b, w_out, w_out, h, ln_g, ln_b)


def _proj_out(x, w, h, ln_g, ln_b, layer, tm=512):
    n, d = h.shape
    k = x.shape[1]
    return pl.pallas_call(
        _proj_out_kernel,
        grid=(n // tm,),
        in_specs=[pl.BlockSpec((tm, k), lambda i: (i, 0)),
                  _layer_spec((k, d), layer),
                  pl.BlockSpec((tm, d), lambda i: (i, 0)),
                  _layer_spec((1, d), layer),
                  _layer_spec((1, d), layer)],
        out_specs=[pl.BlockSpec((tm, d), lambda i: (i, 0)),
                   pl.BlockSpec((tm, d), lambda i: (i, 0))],
        out_shape=[jax.ShapeDtypeStruct((n, d), F32), jax.ShapeDtypeStruct((n, d), BF16)],
        compiler_params=_params(("arbitrary",)),
        name="proj_out",
    )(x, w, h, ln_g, ln_b)


def _mem_attn_kernel(x_ref, wq_ref, k_ref, v_ref, o_ref):
    d = x_ref.shape[2]
    hd = d // MEM_HEADS
    q = _dot(x_ref[0], wq_ref[...]) * (1.0 / math.sqrt(hd))
    for head in range(MEM_HEADS):
        cols = slice(head * hd, (head + 1) * hd)
        logits = _dot_nt(q[:, cols].astype(BF16), k_ref[0, :, cols])
        m = jnp.max(logits, axis=1, keepdims=True)
        e = jnp.exp(logits - m)
        s = jnp.sum(e, axis=1, keepdims=True)
        o = _dot(e.astype(BF16), v_ref[0, :, cols]) / s
        o_ref[0, :, cols] = o.astype(o_ref.dtype)


def _mem_attn(x, wq, km, vm, layer, tm=512):
    b, length, d = x.shape
    mlen = km.shape[1]
    return pl.pallas_call(
        _mem_attn_kernel,
        grid=(b, length // tm),
        in_specs=[pl.BlockSpec((1, tm, d), lambda i, j: (i, j, 0)),
                  _layer_spec((d, d), layer),
                  pl.BlockSpec((1, mlen, d), lambda i, j: (i, 0, 0)),
                  pl.BlockSpec((1, mlen, d), lambda i, j: (i, 0, 0))],
        out_specs=pl.BlockSpec((1, tm, d), lambda i, j: (i, j, 0)),
        out_shape=jax.ShapeDtypeStruct((b, length, d), BF16),
        compiler_params=_params(("arbitrary", "arbitrary")),
        name="mem_attn",
    )(x, wq, km, vm)


def _split_bf16(a):
    hi = a.astype(BF16)
    lo = (a - hi.astype(F32)).astype(BF16)
    return hi, lo


def _router_kernel(x_ref, wt_ref, bias_ref, idx_ref, gate_ref, mask_ref, cnt_ref):
    tm = x_ref.shape[0]
    xh, xl = _split_bf16(x_ref[...])
    wh, wl = _split_bf16(wt_ref[...])
    logits = _dot_nt(wh, xh) + (_dot_nt(wh, xl) + _dot_nt(wl, xh))
    scores = jax.nn.sigmoid(logits)
    sel = scores + bias_ref[...]

    member = lax.broadcasted_iota(jnp.int32, (GROUP_SIZE, tm), 0)
    group_scores = []
    for g in range(N_GROUPS):
        v = sel[g * GROUP_SIZE:(g + 1) * GROUP_SIZE, :]
        m1 = jnp.max(v, axis=0, keepdims=True)
        first = jnp.min(jnp.where(v == m1, member, GROUP_SIZE), axis=0, keepdims=True)
        m2 = jnp.max(jnp.where(member == first, -jnp.inf, v), axis=0, keepdims=True)
        group_scores.append(m1 + m2)
    masked = []
    for g in range(N_GROUPS):
        beaten_by = jnp.zeros((1, tm), jnp.int32)
        for o in range(N_GROUPS):
            if o == g:
                continue
            wins = group_scores[o] > group_scores[g]
            if o < g:
                wins = jnp.logical_or(wins, group_scores[o] == group_scores[g])
            beaten_by = beaten_by + wins.astype(jnp.int32)
        keep = beaten_by < TOPK_GROUPS
        masked.append(jnp.where(keep, sel[g * GROUP_SIZE:(g + 1) * GROUP_SIZE, :], -jnp.inf))
    cand = jnp.concatenate(masked, axis=0)

    expert = lax.broadcasted_iota(jnp.int32, (N_EXPERTS, tm), 0)
    beaten_by = jnp.zeros((N_EXPERTS, tm), jnp.int32)
    for o in range(N_EXPERTS):
        other = cand[o:o + 1, :]
        wins = jnp.logical_or(other > cand, jnp.logical_and(other == cand, expert > o))
        beaten_by = beaten_by + wins.astype(jnp.int32)
    chosen = beaten_by < TOP_K
    chosen_i = chosen.astype(jnp.int32)
    mask_ref[...] = chosen_i
    cnt_ref[...] = jnp.broadcast_to(jnp.sum(chosen_i, axis=1, keepdims=True), cnt_ref.shape)

    picked = jnp.where(chosen, scores, 0.0)
    gate = picked / jnp.sum(picked, axis=0, keepdims=True) * ROUTE_SCALE

    running = jnp.zeros((1, tm), jnp.int32)
    slots = []
    for g in range(N_GROUPS):
        part = chosen_i[g * GROUP_SIZE:(g + 1) * GROUP_SIZE, :]
        rows_ = []
        for r in range(GROUP_SIZE):
            rows_.append(running)
            running = running + part[r:r + 1, :]
        slots.append(jnp.concatenate(rows_, axis=0))
    slot = jnp.concatenate(slots, axis=0)
    idx_rows = []
    gate_rows = []
    for k in range(TOP_K):
        hit = jnp.logical_and(chosen, slot == k)
        idx_rows.append(jnp.sum(jnp.where(hit, expert, 0), axis=0, keepdims=True))
        gate_rows.append(jnp.sum(jnp.where(hit, gate, 0.0), axis=0, keepdims=True))
    idx_ref[...] = jnp.concatenate(idx_rows, axis=0)
    gate_ref[...] = jnp.concatenate(gate_rows, axis=0)


def _router(x, w_router_t, bias, tm=512):
    n, d = x.shape
    return pl.pallas_call(
        _router_kernel,
        grid=(n // tm,),
        in_specs=[pl.BlockSpec((tm, d), lambda i: (i, 0)),
                  pl.BlockSpec((N_EXPERTS, d), lambda i: (0, 0)),
                  pl.BlockSpec((N_EXPERTS, 1), lambda i: (0, 0))],
        out_specs=[pl.BlockSpec((TOP_K, tm), lambda i: (0, i)),
                   pl.BlockSpec((TOP_K, tm), lambda i: (0, i)),
                   pl.BlockSpec((N_EXPERTS, tm), lambda i: (0, i)),
                   pl.BlockSpec((None, N_EXPERTS, LANES), lambda i: (i, 0, 0))],
        out_shape=[jax.ShapeDtypeStruct((TOP_K, n), jnp.int32),
                   jax.ShapeDtypeStruct((TOP_K, n), F32),
                   jax.ShapeDtypeStruct((N_EXPERTS, n), jnp.int32),
                   jax.ShapeDtypeStruct((n // tm, N_EXPERTS, LANES), jnp.int32)],
        compiler_params=_params(("arbitrary",)),
        name="router",
    )(x, w_router_t, bias.reshape(N_EXPERTS, 1))


def _plan_kernel(mask_ref, idx_ref, off_ref, pos_ref):
    tm = mask_ref.shape[1]
    before = (lax.broadcasted_iota(jnp.int32, (tm, tm), 0)
              < lax.broadcasted_iota(jnp.int32, (tm, tm), 1)).astype(BF16)
    rank = _dot(mask_ref[...].astype(BF16), before).astype(jnp.int32)
    row_of = off_ref[...] + rank
    expert = lax.broadcasted_iota(jnp.int32, (N_EXPERTS, tm), 0)
    rows_ = []
    for k in range(TOP_K):
        hit = expert == idx_ref[k:k + 1, :]
        rows_.append(jnp.sum(jnp.where(hit, row_of, 0), axis=0, keepdims=True))
    pos_ref[...] = jnp.concatenate(rows_, axis=0)


def _plan(mask_t, idx_t, tile_offsets, tm):
    n = mask_t.shape[1]
    return pl.pallas_call(
        _plan_kernel,
        grid=(n // tm,),
        in_specs=[pl.BlockSpec((N_EXPERTS, tm), lambda i: (0, i)),
                  pl.BlockSpec((TOP_K, tm), lambda i: (0, i)),
                  pl.BlockSpec((None, N_EXPERTS, 1), lambda i: (i, 0, 0))],
        out_specs=pl.BlockSpec((TOP_K, tm), lambda i: (0, i)),
        out_shape=jax.ShapeDtypeStruct((TOP_K, n), jnp.int32),
        compiler_params=_params(("arbitrary",)),
        name="plan",
    )(mask_t, idx_t, tile_offsets)


def _block_layout(counts_per_tile, n):
    rows = MOE_ROWS
    nb = (n * TOP_K) // rows + N_EXPERTS
    counts = jnp.sum(counts_per_tile, axis=0)
    padded = (counts + rows - 1) // rows * rows
    padded_ends = jnp.cumsum(padded)
    padded_starts = padded_ends - padded
    tile_offsets = padded_starts[None, :] + jnp.cumsum(counts_per_tile, axis=0) - counts_per_tile
    used = (padded_ends[-1] // rows).astype(jnp.int32)
    block_start = jnp.arange(nb, dtype=jnp.int32) * rows
    block_expert = jnp.minimum(
        jnp.sum((padded_ends[None, :] <= block_start[:, None]).astype(jnp.int32), axis=1),
        N_EXPERTS - 1)
    last = block_expert[jnp.maximum(used - 1, 0)]
    block_expert = jnp.where(jnp.arange(nb) < used, block_expert, last)
    pad_from = (padded_starts + counts).astype(jnp.int32)
    return (tile_offsets.astype(jnp.int32)[:, :, None], block_expert, used.reshape(1),
            pad_from, padded_ends.astype(jnp.int32))


DISPATCH_ROWS = 128
DMA_QUEUES = 2


def _pack_rows(x):
    k = x.shape[1] // 2
    lo = pltpu.bitcast(x[:, :k].astype(BF16).astype(F32), jnp.uint32) >> 16
    hi = pltpu.bitcast(x[:, k:].astype(BF16).astype(F32), jnp.uint32) & jnp.uint32(0xFFFF0000)
    return lo | hi


def _unpack_rows(w):
    lo = pltpu.bitcast(w << 16, F32)
    hi = pltpu.bitcast(w & jnp.uint32(0xFFFF0000), F32)
    return lo, hi


def _dispatch_kernel(pad_from_ref, pad_to_ref, used_ref, pos_hbm, x_ref, xs_hbm, pos_smem,
                     xbuf, zeros, row_sem, pos_sem, pad_sem):
    i = pl.program_id(0)
    steps = pl.num_programs(0)
    tm = pos_smem.shape[2]
    slot = i % 3

    def wait_rows(s):
        for _ in range(TOP_K):
            pltpu.make_async_copy(xbuf.at[s], xs_hbm.at[pl.ds(0, tm)], row_sem.at[s]).wait()

    def pos_copy(blk):
        return pltpu.make_async_copy(pos_hbm.at[blk], pos_smem.at[blk % 2], pos_sem.at[blk % 2])

    @pl.when(i == 0)
    def _():
        pos_copy(0).start()
        zeros[...] = jnp.zeros_like(zeros)
        block = zeros.shape[0]

        def tail_copy(blk):
            start = pl.multiple_of(blk * block, block)
            return pltpu.make_async_copy(zeros, xs_hbm.at[pl.ds(start, block)], pad_sem)

        def tail_start(blk, c):
            tail_copy(blk).start()
            return c

        def tail_wait(blk, c):
            tail_copy(blk).wait()
            return c

        lax.fori_loop(used_ref[0], xs_hbm.shape[0] // block, tail_start, 0)
        lax.fori_loop(used_ref[0], xs_hbm.shape[0] // block, tail_wait, 0)

        def pad_walk(start_not_wait):
            def act(cp):
                if start_not_wait:
                    cp.start()
                else:
                    cp.wait()

            def per_expert(e, carry):
                lo = pad_from_ref[e]
                hi = pad_to_ref[e]
                lo8 = jnp.minimum((lo + SUBLANES - 1) // SUBLANES * SUBLANES, hi)

                def one_row(r, c):
                    act(pltpu.make_async_copy(zeros.at[pl.ds(0, 1)], xs_hbm.at[pl.ds(r, 1)],
                                              pad_sem))
                    return c

                lax.fori_loop(lo, lo8, one_row, 0)
                rem = hi - lo8
                off = lo8
                size = block // 2
                while size >= SUBLANES:
                    @pl.when((rem & size) != 0)
                    def _(off=off, size=size):
                        dst = xs_hbm.at[pl.ds(pl.multiple_of(off, SUBLANES), size)]
                        act(pltpu.make_async_copy(zeros.at[pl.ds(0, size)], dst, pad_sem))
                    off = off + (rem & size)
                    size //= 2
                return carry

            lax.fori_loop(0, N_EXPERTS, per_expert, 0)

        pad_walk(True)
        pad_walk(False)

    @pl.when(i >= 3)
    def _():
        wait_rows(slot)

    @pl.when(i + 1 < steps)
    def _():
        pos_copy(i + 1).start()

    xbuf[slot] = _pack_rows(x_ref[...])
    pos_copy(i).wait()
    pslot = i % 2

    for r in range(tm):
        for k in range(TOP_K):
            pltpu.make_async_copy(xbuf.at[slot, pl.ds(r, 1)],
                                  xs_hbm.at[pl.ds(pos_smem[pslot, k, r], 1)],
                                  row_sem.at[slot]).start(priority=k % DMA_QUEUES)

    @pl.when(i == steps - 1)
    def _():
        wait_rows(slot)

        @pl.when(steps > 1)
        def _():
            wait_rows((i + 2) % 3)

        @pl.when(steps > 2)
        def _():
            wait_rows((i + 1) % 3)


def _dispatch(pos, x, pad_from, pad_to, used, total_rows):
    n, d = x.shape
    tm = DISPATCH_ROWS
    grid_spec = pltpu.PrefetchScalarGridSpec(
        num_scalar_prefetch=3,
        grid=(n // tm,),
        in_specs=[pl.BlockSpec(memory_space=pl.ANY),
                  pl.BlockSpec((tm, d), lambda i, *_: (i, 0))],
        out_specs=pl.BlockSpec(memory_space=pl.ANY),
        scratch_shapes=[pltpu.SMEM((2, TOP_K, tm), jnp.int32),
                        pltpu.VMEM((3, tm, d // 2), jnp.uint32),
                        pltpu.VMEM((MOE_ROWS, d // 2), jnp.uint32),
                        pltpu.SemaphoreType.DMA((3,)),
                        pltpu.SemaphoreType.DMA((2,)),
                        pltpu.SemaphoreType.DMA],
    )
    return pl.pallas_call(
        _dispatch_kernel,
        grid_spec=grid_spec,
        out_shape=jax.ShapeDtypeStruct((total_rows, d // 2), jnp.uint32),
        compiler_params=_params(("arbitrary",)),
        name="dispatch",
    )(pad_from, pad_to, used, pos.reshape(TOP_K, n // tm, tm).transpose(1, 0, 2), x)


def _experts_kernel(be_ref, used_ref, x_ref, wg_ref, wu_ref, wd_ref, y_ref,
                    wg_bf, wu_bf, wd_bf):
    i = pl.program_id(0)
    used = used_ref[0]

    @pl.when(i < used)
    def _():
        changed = jnp.logical_or(i == 0, be_ref[jnp.maximum(i - 1, 0)] != be_ref[i])

        @pl.when(changed)
        def _():
            wg_bf[...] = wg_ref[...].astype(BF16)
            wu_bf[...] = wu_ref[...].astype(BF16)
            wd_bf[...] = wd_ref[...].astype(BF16)

        half = x_ref.shape[1]
        x_lo, x_hi = (t.astype(BF16) for t in _unpack_rows(x_ref[...]))
        gate = _dot(x_lo, wg_bf[:half, :]) + _dot(x_hi, wg_bf[half:, :])
        up = _dot(x_lo, wu_bf[:half, :]) + _dot(x_hi, wu_bf[half:, :])
        hid = jax.nn.silu(gate) * up
        y_ref[...] = _pack_rows(_dot(hid.astype(BF16), wd_bf[...]))

    @pl.when(i >= used)
    def _():
        y_ref[...] = jnp.zeros_like(y_ref)


def _experts(block_expert, used, xs, w_gate, w_up, w_down, layer):
    nb = block_expert.shape[0]
    rows = MOE_ROWS
    d = w_gate.shape[2]
    de = w_gate.shape[3]
    grid_spec = pltpu.PrefetchScalarGridSpec(
        num_scalar_prefetch=2,
        grid=(nb,),
        in_specs=[
            pl.BlockSpec((rows, d // 2), lambda i, be, u: (jnp.minimum(i, u[0] - 1), 0)),
            pl.BlockSpec((None, None, d, de), lambda i, be, u: (layer, be[i], 0, 0)),
            pl.BlockSpec((None, None, d, de), lambda i, be, u: (layer, be[i], 0, 0)),
            pl.BlockSpec((None, None, de, d), lambda i, be, u: (layer, be[i], 0, 0)),
        ],
        out_specs=pl.BlockSpec((rows, d // 2), lambda i, be, u: (i, 0)),
        scratch_shapes=[
            pltpu.VMEM((d, de), BF16),
            pltpu.VMEM((d, de), BF16),
            pltpu.VMEM((de, d), BF16),
        ],
    )
    return pl.pallas_call(
        _experts_kernel,
        grid_spec=grid_spec,
        out_shape=jax.ShapeDtypeStruct((nb * rows, d // 2), jnp.uint32),
        compiler_params=_params(("arbitrary",)),
        name="experts",
    )(block_expert, used, xs, w_gate, w_up, w_down)


def _combine_kernel(pos_hbm, y_hbm, gate_ref, h_ref, wsg_ref, wsu_ref, wsd_ref, g_ref, b_ref,
                    *rest):
    if len(rest) == 11:
        wa_ref, wb_ref, o_ref, pa_ref, pb_ref = rest[:5]
    else:
        wa_ref = wb_ref = pa_ref = pb_ref = None
        o_ref = rest[0]
    ybuf0, ybuf1, ybuf2, pos_smem, row_sem, pos_sem = rest[-6:]
    i = pl.program_id(0)
    steps = pl.num_programs(0)
    tm = h_ref.shape[0]
    last = steps - 1
    ybufs = (ybuf0, ybuf1, ybuf2)

    def pos_copy(t):
        return pltpu.make_async_copy(pos_hbm.at[jnp.minimum(t, last)], pos_smem.at[t % 2],
                                     pos_sem.at[t % 2])

    def row_copy(t, buf, k, r):
        return pltpu.make_async_copy(y_hbm.at[pl.ds(pos_smem[t % 2, k, r], 1)],
                                     ybufs[buf].at[k, pl.ds(r, 1)], row_sem.at[buf])

    def wait_rows(buf):
        for k in range(TOP_K):
            pltpu.make_async_copy(y_hbm.at[pl.ds(0, tm)], ybufs[buf].at[k],
                                  row_sem.at[buf]).wait()

    @pl.when(i == 0)
    def _():
        for t in range(2):
            pos_copy(t).start()
            pos_copy(t).wait()

            def issue(r, carry, t=t):
                for k in range(TOP_K):
                    row_copy(t, t, k, r).start()
                return carry
            lax.fori_loop(0, tm, issue, 0)

    def step(cur, ahead):
        wait_rows(cur)

        @pl.when(i > 0)
        def _():
            pos_copy(i + 2).wait()

        @pl.when(i == 0)
        def _():
            pos_copy(2).start()
            pos_copy(2).wait()

        for r in range(tm):
            for k in range(TOP_K):
                row_copy(i + 2, ahead, k, r).start(priority=k % DMA_QUEUES)
        pos_copy(i + 3).start()

        h = h_ref[...]
        x = h.astype(BF16)
        hid = jax.nn.silu(_dot(x, wsg_ref[...])) * _dot(x, wsu_ref[...])
        y = ALPHA * h + _dot(hid.astype(BF16), wsd_ref[...])

        gates = gate_ref[...]
        half = ybufs[cur].shape[2]
        routed_lo = jnp.zeros((tm, half), F32)
        routed_hi = jnp.zeros((tm, half), F32)
        for k in range(TOP_K):
            lo, hi = _unpack_rows(ybufs[cur][k])
            routed_lo = routed_lo + lo * gates[:, k:k + 1]
            routed_hi = routed_hi + hi * gates[:, k:k + 1]
        y = y + jnp.concatenate([routed_lo, routed_hi], axis=1)
        out = _layer_norm(y, g_ref[...], b_ref[...])
        o_ref[...] = out
        if wa_ref is not None:
            out_bf = out.astype(BF16)
            pa_ref[...] = _dot(out_bf, wa_ref[...]).astype(pa_ref.dtype)
            pb_ref[...] = _dot(out_bf, wb_ref[...]).astype(pb_ref.dtype)

        @pl.when(i == last)
        def _():
            wait_rows((cur + 1) % 3)
            wait_rows(ahead)
            pos_copy(i + 3).wait()

    for phase in range(3):
        @pl.when(i % 3 == phase)
        def _(phase=phase):
            step(phase, (phase + 2) % 3)


def _combine(pos, y_rows, gates, h, ws_gate, ws_up, ws_down, ln_g, ln_b, layer, next_proj=None):
    n, d = h.shape
    tm = COMBINE_ROWS
    ds = ws_gate.shape[2]
    in_specs = [pl.BlockSpec(memory_space=pl.ANY),
                pl.BlockSpec(memory_space=pl.ANY),
                pl.BlockSpec((tm, TOP_K), lambda i: (i, 0)),
                pl.BlockSpec((tm, d), lambda i: (i, 0)),
                _layer_spec((d, ds), layer),
                _layer_spec((d, ds), layer),
                _layer_spec((ds, d), layer),
                _layer_spec((1, d), layer),
                _layer_spec((1, d), layer)]
    out_specs = [pl.BlockSpec((tm, d), lambda i: (i, 0))]
    out_shape = [jax.ShapeDtypeStruct((n, d), F32)]
    operands = [pos.reshape(TOP_K, n // tm, tm).transpose(1, 0, 2), y_rows, gates, h,
                ws_gate, ws_up, ws_down, ln_g, ln_b]
    if next_proj is not None:
        for w, dtype in zip(next_proj, (BF16, F32)):
            width = w.shape[2]
            in_specs.append(_layer_spec((d, width), layer + 1))
            out_specs.append(pl.BlockSpec((tm, width), lambda i: (i, 0)))
            out_shape.append(jax.ShapeDtypeStruct((n, width), dtype))
            operands.append(w)
    return pl.pallas_call(
        _combine_kernel,
        grid=(n // tm,),
        in_specs=in_specs,
        out_specs=out_specs,
        out_shape=out_shape,
        scratch_shapes=[pltpu.VMEM((TOP_K, tm, d // 2), jnp.uint32)] * 3
        + [pltpu.SMEM((2, TOP_K, tm), jnp.int32),
           pltpu.SemaphoreType.DMA((3,)),
           pltpu.SemaphoreType.DMA((2,))],
        compiler_params=_params(("arbitrary",)),
        name="combine",
    )(*operands)


def _alibi_slopes(n):
    return 2.0 ** (-8.0 * jnp.arange(1, n + 1, dtype=F32) / n)


def kernel(x, mem, w_in, a_sinks, g_a, g_b, w_out, ln1_g, ln1_b, wq_m, wk_m, wv_m, wo_m,
           ln2_g, ln2_b, w_router, router_bias, w_gate, w_up, w_down, ws_gate, ws_up, ws_down,
           ln3_g, ln3_b):
    b, length, d = x.shape
    n = b * length
    mlen = mem.shape[1]
    depth = w_in.shape[0]
    h = x.reshape(n, d)
    h_bf = h.astype(BF16)
    mem_bf = mem.reshape(b * mlen, d).astype(BF16)
    slopes_a = _alibi_slopes(A_Q_HEADS)
    slopes_b = _alibi_slopes(B_HEADS)

    w_in_bf = w_in.astype(BF16)
    w_in_a = w_in_bf[:, :, :A_PROJ_WIDTH]
    w_in_b = w_in_bf[:, :, A_PROJ_WIDTH:]
    w_out_bf = w_out.astype(BF16)
    wq_bf, wk_bf, wv_bf, wo_bf = (w.astype(BF16) for w in (wq_m, wk_m, wv_m, wo_m))
    wsg_bf, wsu_bf, wsd_bf = (w.astype(BF16) for w in (ws_gate, ws_up, ws_down))
    row = lambda p: p.reshape(depth, 1, p.shape[1])
    g_b3, ln1_g3, ln1_b3, ln2_g3, ln2_b3, ln3_g3, ln3_b3 = (
        row(p) for p in (g_b, ln1_g, ln1_b, ln2_g, ln2_b, ln3_g, ln3_b))

    proj_a = _matmul(h_bf, w_in_a, 0, 512, A_PROJ_WIDTH, BF16)
    proj_b = _matmul(h_bf, w_in_b, 0, 512, B_WIDTH, F32)
    for l in range(depth):
        mixed_a = _attn_a(proj_a.reshape(b, length, A_PROJ_WIDTH), slopes_a, a_sinks[l], g_a[l])
        out_b = _attn_b(proj_b.reshape(b, length, 3 * B_WIDTH), slopes_b)
        h, h_bf = _mix_out(mixed_a.reshape(n, A_WIDTH), out_b.reshape(n, B_WIDTH), g_b3,
                           w_out_bf, h, ln1_g3, ln1_b3, l)

        km = _matmul(mem_bf, wk_bf, l, b * mlen, d // 2, BF16)
        vm = _matmul(mem_bf, wv_bf, l, b * mlen, d // 2, BF16)
        o = _mem_attn(h_bf.reshape(b, length, d), wq_bf,
                      km.reshape(b, mlen, d), vm.reshape(b, mlen, d), l)
        h, h_bf = _proj_out(o.reshape(n, d), wo_bf, h, ln2_g3, ln2_b3, l)

        route_tm = 512
        idx_t, gate_t, mask_t, cnt = _router(h, w_router[l].T, router_bias[l], route_tm)
        tile_offsets, block_expert, used, pad_from, pad_to = _block_layout(cnt[:, :, 0], n)
        pos = _plan(mask_t, idx_t, tile_offsets, route_tm)
        xs = _dispatch(pos, h, pad_from, pad_to, used, block_expert.shape[0] * MOE_ROWS)
        y_rows = _experts(block_expert, used, xs, w_gate, w_up, w_down, l)
        next_proj = (w_in_a, w_in_b) if l + 1 < depth else None
        outs = _combine(pos, y_rows, gate_t.T, h, wsg_bf, wsu_bf, wsd_bf, ln3_g3, ln3_b3, l,
                        next_proj)
        h = outs[0]
        if next_proj is not None:
            proj_a, proj_b = outs[1:]
    return h.reshape(b, length, d)
```

```python
import math

import jax
import jax.numpy as jnp
from jax import lax
from jax.experimental import pallas as pl
from jax.experimental.pallas import tpu as pltpu

F32 = jnp.float32
BF16 = jnp.bfloat16

HEAD_DIM = 64
A_Q_HEADS = 16
A_KV_HEADS = 2
A_WINDOW = 128
B_HEADS = 16
B_BRANCH_DILATIONS = (16, 4, 1)
A_WIDTH = A_Q_HEADS * HEAD_DIM
A_KV_WIDTH = A_KV_HEADS * HEAD_DIM
B_WIDTH = B_HEADS * HEAD_DIM
A_PROJ_WIDTH = A_WIDTH + 2 * A_KV_WIDTH
BLK = 128

MEM_HEADS = 4

N_EXPERTS = 64
TOP_K = 8
N_GROUPS = 8
GROUP_SIZE = N_EXPERTS // N_GROUPS
TOPK_GROUPS = 4
ROUTE_SCALE = 2.5

DEPTH = 2
ALPHA = (2.0 * DEPTH) ** 0.25
LN_EPS = 1e-5
RMS_EPS = 1e-6

LANES = 128
SUBLANES = 8
NEG = -1e30

MOE_ROWS = 512
COMBINE_ROWS = 128
VMEM_LIMIT = 56 * 1024 * 1024


def _params(sem, vmem=VMEM_LIMIT):
    return pltpu.CompilerParams(dimension_semantics=sem, vmem_limit_bytes=vmem)


def _dot(a, b):
    return jnp.dot(a, b, preferred_element_type=F32)


def _dot_nt(a, b):
    return lax.dot_general(a, b, (((1,), (1,)), ((), ())), preferred_element_type=F32)


def _layer_norm(y, g, b):
    mu = jnp.mean(y, axis=-1, keepdims=True)
    yc = y - mu
    var = jnp.mean(yc * yc, axis=-1, keepdims=True)
    return yc * lax.rsqrt(var + LN_EPS) * g + b


def _layer_spec(shape, layer, index=None, single=True):
    index = index or (0,) * len(shape)
    mode = dict(pipeline_mode=pl.Buffered(1)) if single else {}
    return pl.BlockSpec((None,) + tuple(shape), lambda *_: (layer,) + tuple(index), **mode)


def _mm_kernel(x_ref, w_ref, o_ref):
    o_ref[...] = _dot(x_ref[...], w_ref[...]).astype(o_ref.dtype)


def _matmul(x, w, layer, tm, tn, out_dtype):
    m, k = x.shape
    n = w.shape[2]
    return pl.pallas_call(
        _mm_kernel,
        grid=(n // tn, m // tm),
        in_specs=[pl.BlockSpec((tm, k), lambda j, i: (i, 0)),
                  pl.BlockSpec((None, k, tn), lambda j, i: (layer, 0, j))],
        out_specs=pl.BlockSpec((tm, tn), lambda j, i: (i, j)),
        out_shape=jax.ShapeDtypeStruct((m, n), out_dtype),
        compiler_params=_params(("arbitrary", "arbitrary")),
        name="matmul",
    )(x, w)


def _band_bias(stacked_heads, max_offset, slope_of_stack):
    rows = stacked_heads * BLK
    row = lax.broadcasted_iota(jnp.int32, (rows, 2 * BLK), 0)
    col = lax.broadcasted_iota(jnp.int32, (rows, 2 * BLK), 1)
    offset = (row & (BLK - 1)) + BLK - col
    slope = jnp.full((rows, 2 * BLK), slope_of_stack[stacked_heads - 1], F32)
    for a in range(stacked_heads - 2, -1, -1):
        slope = jnp.where(row < (a + 1) * BLK, slope_of_stack[a], slope)
    valid = jnp.logical_and(offset >= 0, offset <= max_offset)
    bias = jnp.where(valid, -slope * offset.astype(F32), NEG)
    return bias, jnp.where(col < BLK, NEG, bias)


def _roll_half(t):
    return pltpu.roll(t.astype(F32), LANES // 2, axis=1).astype(BF16)


A_STACK = 4


def _attn_a_kernel(slopes_ref, sinks_ref, q_ref, kc_ref, kp_ref, vc_ref, vp_ref, g_ref,
                   o_ref, o_scr, bias_scr):
    n = pl.program_id(1)
    tiles_per_kv = A_Q_HEADS // A_KV_HEADS // 2

    def head_of(kv, half, a):
        return 2 * (kv * tiles_per_kv + a) + half

    @pl.when(jnp.logical_and(pl.program_id(0) == 0, n == 0))
    def _():
        for kv in range(A_KV_HEADS):
            for half in range(2):
                slopes = [slopes_ref[head_of(kv, half, a)] for a in range(A_STACK)]
                bias, bias_first = _band_bias(A_STACK, A_WINDOW - 1, slopes)
                bias_scr[0, 2 * kv + half] = bias
                bias_scr[1, 2 * kv + half] = bias_first

    first = (n == 0).astype(jnp.int32)
    lane = lax.broadcasted_iota(jnp.int32, (BLK, LANES), 1)
    lo = lane < HEAD_DIM
    srow = lax.broadcasted_iota(jnp.int32, (A_STACK * BLK, 1), 0)

    k_ver = (jnp.concatenate([kp_ref[0], kc_ref[0]], axis=0),
             jnp.concatenate([_roll_half(kp_ref[0]), _roll_half(kc_ref[0])], axis=0))
    v_ver = (jnp.concatenate([vp_ref[0], vc_ref[0]], axis=0),
             jnp.concatenate([_roll_half(vp_ref[0]), _roll_half(vc_ref[0])], axis=0))

    ss = jnp.zeros((BLK, 1), F32)
    for kv in range(A_KV_HEADS):
        outs = []
        for half in range(2):
            keep = lo if half == 0 else jnp.logical_not(lo)
            parts = []
            for a in range(A_STACK):
                t = kv * tiles_per_kv + a
                qt = q_ref[0, :, t * LANES:(t + 1) * LANES] * jnp.asarray(0.125, BF16)
                parts.append(jnp.where(keep, qt, jnp.zeros_like(qt)))
            qs = jnp.concatenate(parts, axis=0)
            ver = kv ^ half
            logits = _dot_nt(qs, k_ver[ver]) + bias_scr[first, 2 * kv + half]
            sink = jnp.full((A_STACK * BLK, 1), sinks_ref[head_of(kv, half, A_STACK - 1)], F32)
            for a in range(A_STACK - 2, -1, -1):
                sink = jnp.where(srow < (a + 1) * BLK, sinks_ref[head_of(kv, half, a)], sink)
            m = jnp.maximum(jnp.max(logits, axis=1, keepdims=True), sink)
            e = jnp.exp(logits - m)
            s = jnp.sum(e, axis=1, keepdims=True) + jnp.exp(sink - m)
            acc = _dot(e.astype(BF16), v_ver[ver])
            outs.append(acc * (1.0 / s))
        for a in range(A_STACK):
            t = kv * tiles_per_kv + a
            o_tile = jnp.where(lo, outs[0][a * BLK:(a + 1) * BLK], outs[1][a * BLK:(a + 1) * BLK])
            ss = ss + jnp.sum(o_tile * o_tile, axis=1, keepdims=True)
            o_scr[:, t * LANES:(t + 1) * LANES] = o_tile
    scale = lax.rsqrt(ss * (1.0 / A_WIDTH) + RMS_EPS)
    o_ref[0] = (o_scr[...] * scale * g_ref[...]).astype(o_ref.dtype)


def _attn_a(proj_a, slopes, sinks, g_a):
    b, length, _ = proj_a.shape
    kcol = A_WIDTH // LANES
    vcol = kcol + 1
    grid_spec = pltpu.PrefetchScalarGridSpec(
        num_scalar_prefetch=2,
        grid=(b, length // BLK),
        in_specs=[
            pl.BlockSpec((1, BLK, A_WIDTH), lambda i, n, *_: (i, n, 0)),
            pl.BlockSpec((1, BLK, LANES), lambda i, n, *_: (i, n, kcol)),
            pl.BlockSpec((1, BLK, LANES), lambda i, n, *_: (i, jnp.maximum(n - 1, 0), kcol)),
            pl.BlockSpec((1, BLK, LANES), lambda i, n, *_: (i, n, vcol)),
            pl.BlockSpec((1, BLK, LANES), lambda i, n, *_: (i, jnp.maximum(n - 1, 0), vcol)),
            pl.BlockSpec((1, A_WIDTH), lambda i, n, *_: (0, 0)),
        ],
        out_specs=pl.BlockSpec((1, BLK, A_WIDTH), lambda i, n, *_: (i, n, 0)),
        scratch_shapes=[pltpu.VMEM((BLK, A_WIDTH), F32),
                        pltpu.VMEM((2, 2 * A_KV_HEADS, A_STACK * BLK, 2 * BLK), F32)],
    )
    return pl.pallas_call(
        _attn_a_kernel,
        grid_spec=grid_spec,
        out_shape=jax.ShapeDtypeStruct((b, length, A_WIDTH), BF16),
        compiler_params=_params(("arbitrary", "arbitrary")),
        name="attn_a",
    )(slopes, sinks, proj_a, proj_a, proj_a, proj_a, proj_a, g_a.reshape(1, A_WIDTH))


B_PIPE_DEPTH = 4


def _attn_b_kernel(slopes_ref, q_ref, k_ref, v_ref, o_ref, acc_scr, m_scr, s_scr, bias_scr,
                   *stage_scrs):
    l_scrs = stage_scrs[:B_PIPE_DEPTH]
    mx_scrs = stage_scrs[B_PIPE_DEPTH:]
    hp = pl.program_id(1)
    length = q_ref.shape[1]
    lane = lax.broadcasted_iota(jnp.int32, (BLK, LANES), 1)
    lo = lane < HEAD_DIM
    qrow = lax.broadcasted_iota(jnp.int32, (2 * BLK, LANES), 0)
    qlane = lax.broadcasted_iota(jnp.int32, (2 * BLK, LANES), 1)
    qmask = (qrow >= BLK) == (qlane >= HEAD_DIM)

    for branch, dil in enumerate(B_BRANCH_DILATIONS):
        slopes = [slopes_ref[2 * hp] * float(dil), slopes_ref[2 * hp + 1] * float(dil)]
        bias, bias_first = _band_bias(2, BLK, slopes)
        bias_scr[branch, 0] = bias
        bias_scr[branch, 1] = bias_first

    for branch, dil in enumerate(B_BRANCH_DILATIONS):
        blocks_per_class = length // (dil * BLK)

        def rows(start, dil=dil):
            if dil == 1:
                return pl.ds(start, BLK)
            return pl.ds(start, BLK, stride=dil)

        def block_rows(it, dil=dil, blocks_per_class=blocks_per_class, rows=rows):
            res = it // blocks_per_class
            blk = it % blocks_per_class
            cur = rows(res + dil * BLK * blk)
            prev = rows(res + dil * BLK * jnp.maximum(blk - 1, 0))
            return cur, prev, jnp.where(blk == 0, 1, 0)

        def logits_stage(it, l_ref, m_ref, branch=branch, block_rows=block_rows):
            cur, prev, first = block_rows(it)
            q2 = q_ref[0, cur, :] * 0.125
            qs = jnp.concatenate([q2, q2], axis=0)
            qs = jnp.where(qmask, qs, 0.0).astype(BF16)
            k2 = jnp.concatenate([k_ref[0, prev, :], k_ref[0, cur, :]], axis=0).astype(BF16)
            logits = _dot_nt(qs, k2) + bias_scr[branch, first]
            l_ref[...] = logits
            m_ref[...] = jnp.broadcast_to(jnp.max(logits, axis=1, keepdims=True), m_ref.shape)

        def value_stage(it, l_ref, m_ref, branch=branch, block_rows=block_rows):
            cur, prev, _ = block_rows(it)
            v2 = jnp.concatenate([v_ref[0, prev, :], v_ref[0, cur, :]], axis=0).astype(BF16)
            m = m_ref[...]
            e = jnp.exp(l_ref[...] - jnp.concatenate([m, m], axis=1))
            s = jnp.sum(e, axis=1, keepdims=True)
            acc = _dot(e.astype(BF16), v2)
            m_t = jnp.where(lo, m[:BLK], m[BLK:])
            s_t = jnp.where(lo, s[:BLK], s[BLK:])
            acc_t = jnp.where(lo, acc[:BLK], acc[BLK:])
            if branch == 0:
                m_scr[cur, :] = m_t
                s_scr[cur, :] = s_t
                acc_scr[cur, :] = acc_t
            else:
                m_old = m_scr[cur, :]
                s_old = s_scr[cur, :]
                acc_old = acc_scr[cur, :]
                m_new = jnp.maximum(m_old, m_t)
                w_old = jnp.exp(m_old - m_new)
                w_blk = jnp.exp(m_t - m_new)
                m_scr[cur, :] = m_new
                s_scr[cur, :] = s_old * w_old + s_t * w_blk
                acc_scr[cur, :] = acc_old * w_old + acc_t * w_blk

        n_blocks = length // BLK
        depth = len(l_scrs)
        for u in range(depth):
            logits_stage(u, l_scrs[u], mx_scrs[u])

        def body(j, carry, logits_stage=logits_stage, value_stage=value_stage, depth=depth):
            for u in range(depth):
                value_stage(depth * j + u, l_scrs[u], mx_scrs[u])
            for u in range(depth):
                logits_stage(depth * (j + 1) + u, l_scrs[u], mx_scrs[u])
            return carry

        lax.fori_loop(0, n_blocks // depth - 1, body, 0)
        for u in range(depth):
            value_stage(n_blocks - depth + u, l_scrs[u], mx_scrs[u])

    o_ref[0] = acc_scr[...] / s_scr[...]


def _attn_b(proj_b, slopes):
    b, length, _ = proj_b.shape
    pairs = B_WIDTH // LANES
    grid_spec = pltpu.PrefetchScalarGridSpec(
        num_scalar_prefetch=1,
        grid=(b, pairs),
        in_specs=[
            pl.BlockSpec((1, length, LANES), lambda i, p, *_: (i, 0, p)),
            pl.BlockSpec((1, length, LANES), lambda i, p, *_: (i, 0, pairs + p)),
            pl.BlockSpec((1, length, LANES), lambda i, p, *_: (i, 0, 2 * pairs + p)),
        ],
        out_specs=pl.BlockSpec((1, length, LANES), lambda i, p, *_: (i, 0, p)),
        scratch_shapes=[pltpu.VMEM((length, LANES), F32)] * 3
        + [pltpu.VMEM((len(B_BRANCH_DILATIONS), 2, 2 * BLK, 2 * BLK), F32)]
        + [pltpu.VMEM((2 * BLK, 2 * BLK), F32)] * B_PIPE_DEPTH
        + [pltpu.VMEM((2 * BLK, LANES), F32)] * B_PIPE_DEPTH,
    )
    return pl.pallas_call(
        _attn_b_kernel,
        grid_spec=grid_spec,
        out_shape=jax.ShapeDtypeStruct((b, length, B_WIDTH), F32),
        compiler_params=_params(("arbitrary", "arbitrary")),
        name="attn_b",
    )(slopes, proj_b, proj_b, proj_b)


def _mix_out_kernel(xa_ref, xb_ref, gb_ref, wa_ref, wb_ref, h_ref, g_ref, b_ref,
                    o_ref, obf_ref):
    xb = xb_ref[...]
    scale = lax.rsqrt(jnp.mean(xb * xb, axis=-1, keepdims=True) + RMS_EPS)
    xb = (xb * scale * gb_ref[...]).astype(BF16)
    y = _dot(xa_ref[...], wa_ref[...]) + _dot(xb, wb_ref[...])
    out = _layer_norm(ALPHA * h_ref[...] + y, g_ref[...], b_ref[...])
    o_ref[...] = out
    obf_ref[...] = out.astype(BF16)


def _proj_out_kernel(x_ref, w_ref, h_ref, g_ref, b_ref, o_ref):
    y = _dot(x_ref[...], w_ref[...])
    o_ref[...] = _layer_norm(ALPHA * h_ref[...] + y, g_ref[...], b_ref[...])


def _mix_out(xa, xb, g_b, w_out, h, ln_g, ln_b, layer, tm=512):
    n, d = h.shape
    ka = xa.shape[1]
    kb = xb.shape[1]
    return pl.pallas_call(
        _mix_out_kernel,
        grid=(n // tm,),
        in_specs=[pl.BlockSpec((tm, ka), lambda i: (i, 0)),
                  pl.BlockSpec((tm, kb), lambda i: (i, 0)),
                  _layer_spec((1, kb), layer),
                  _layer_spec((ka, d), layer, (0, 0)),
                  _layer_spec((kb, d), layer, (ka // kb, 0)),
                  pl.BlockSpec((tm, d), lambda i: (i, 0)),
                  _layer_spec((1, d), layer),
                  _layer_spec((1, d), layer)],
        out_specs=[pl.BlockSpec((tm, d), lambda i: (i, 0)),
                   pl.BlockSpec((tm, d), lambda i: (i, 0))],
        out_shape=[jax.ShapeDtypeStruct((n, d), F32), jax.ShapeDtypeStruct((n, d), BF16)],
        compiler_params=_params(("arbitrary",)),
        name="mix_out",
    )(xa, xb, g_b, w_out, w_out, h, ln_g, ln_b)


def _proj_out(x, w, h, ln_g, ln_b, layer, tm=512):
    n, d = h.shape
    k = x.shape[1]
    return pl.pallas_call(
        _proj_out_kernel,
        grid=(n // tm,),
        in_specs=[pl.BlockSpec((tm, k), lambda i: (i, 0)),
                  _layer_spec((k, d), layer),
                  pl.BlockSpec((tm, d), lambda i: (i, 0)),
                  _layer_spec((1, d), layer),
                  _layer_spec((1, d), layer)],
        out_specs=pl.BlockSpec((tm, d), lambda i: (i, 0)),
        out_shape=jax.ShapeDtypeStruct((n, d), F32),
        compiler_params=_params(("arbitrary",)),
        name="proj_out",
    )(x, w, h, ln_g, ln_b)


def _mem_attn_kernel(x_ref, wq_ref, k_ref, v_ref, o_ref):
    d = x_ref.shape[2]
    hd = d // MEM_HEADS
    q = _dot(x_ref[0], wq_ref[...]) * (1.0 / math.sqrt(hd))
    for head in range(MEM_HEADS):
        cols = slice(head * hd, (head + 1) * hd)
        logits = _dot_nt(q[:, cols].astype(BF16), k_ref[0, :, cols])
        m = jnp.max(logits, axis=1, keepdims=True)
        e = jnp.exp(logits - m)
        s = jnp.sum(e, axis=1, keepdims=True)
        o = _dot(e.astype(BF16), v_ref[0, :, cols]) / s
        o_ref[0, :, cols] = o.astype(o_ref.dtype)


def _mem_attn(x, wq, km, vm, layer, tm=512):
    b, length, d = x.shape
    mlen = km.shape[1]
    return pl.pallas_call(
        _mem_attn_kernel,
        grid=(b, length // tm),
        in_specs=[pl.BlockSpec((1, tm, d), lambda i, j: (i, j, 0)),
                  _layer_spec((d, d), layer),
                  pl.BlockSpec((1, mlen, d), lambda i, j: (i, 0, 0)),
                  pl.BlockSpec((1, mlen, d), lambda i, j: (i, 0, 0))],
        out_specs=pl.BlockSpec((1, tm, d), lambda i, j: (i, j, 0)),
        out_shape=jax.ShapeDtypeStruct((b, length, d), BF16),
        compiler_params=_params(("arbitrary", "arbitrary")),
        name="mem_attn",
    )(x, wq, km, vm)


def _split_bf16(a):
    hi = a.astype(BF16)
    lo = (a - hi.astype(F32)).astype(BF16)
    return hi, lo


def _router_kernel(x_ref, wt_ref, bias_ref, idx_ref, gate_ref, mask_ref, cnt_ref):
    tm = x_ref.shape[0]
    xh, xl = _split_bf16(x_ref[...])
    wh, wl = _split_bf16(wt_ref[...])
    logits = _dot_nt(wh, xh) + (_dot_nt(wh, xl) + _dot_nt(wl, xh))
    scores = jax.nn.sigmoid(logits)
    sel = scores + bias_ref[...]

    member = lax.broadcasted_iota(jnp.int32, (GROUP_SIZE, tm), 0)
    group_scores = []
    for g in range(N_GROUPS):
        v = sel[g * GROUP_SIZE:(g + 1) * GROUP_SIZE, :]
        m1 = jnp.max(v, axis=0, keepdims=True)
        first = jnp.min(jnp.where(v == m1, member, GROUP_SIZE), axis=0, keepdims=True)
        m2 = jnp.max(jnp.where(member == first, -jnp.inf, v), axis=0, keepdims=True)
        group_scores.append(m1 + m2)
    masked = []
    for g in range(N_GROUPS):
        beaten_by = jnp.zeros((1, tm), jnp.int32)
        for o in range(N_GROUPS):
            if o == g:
                continue
            wins = group_scores[o] > group_scores[g]
            if o < g:
                wins = jnp.logical_or(wins, group_scores[o] == group_scores[g])
            beaten_by = beaten_by + wins.astype(jnp.int32)
        keep = beaten_by < TOPK_GROUPS
        masked.append(jnp.where(keep, sel[g * GROUP_SIZE:(g + 1) * GROUP_SIZE, :], -jnp.inf))
    cand = jnp.concatenate(masked, axis=0)

    expert = lax.broadcasted_iota(jnp.int32, (N_EXPERTS, tm), 0)
    beaten_by = jnp.zeros((N_EXPERTS, tm), jnp.int32)
    for o in range(N_EXPERTS):
        other = cand[o:o + 1, :]
        wins = jnp.logical_or(other > cand, jnp.logical_and(other == cand, expert > o))
        beaten_by = beaten_by + wins.astype(jnp.int32)
    chosen = beaten_by < TOP_K
    chosen_i = chosen.astype(jnp.int32)
    mask_ref[...] = chosen_i
    cnt_ref[...] = jnp.broadcast_to(jnp.sum(chosen_i, axis=1, keepdims=True), cnt_ref.shape)

    picked = jnp.where(chosen, scores, 0.0)
    gate = picked / jnp.sum(picked, axis=0, keepdims=True) * ROUTE_SCALE

    running = jnp.zeros((1, tm), jnp.int32)
    slots = []
    for g in range(N_GROUPS):
        part = chosen_i[g * GROUP_SIZE:(g + 1) * GROUP_SIZE, :]
        rows_ = []
        for r in range(GROUP_SIZE):
            rows_.append(running)
            running = running + part[r:r + 1, :]
        slots.append(jnp.concatenate(rows_, axis=0))
    slot = jnp.concatenate(slots, axis=0)
    idx_rows = []
    gate_rows = []
    for k in range(TOP_K):
        hit = jnp.logical_and(chosen, slot == k)
        idx_rows.append(jnp.sum(jnp.where(hit, expert, 0), axis=0, keepdims=True))
        gate_rows.append(jnp.sum(jnp.where(hit, gate, 0.0), axis=0, keepdims=True))
    idx_ref[...] = jnp.concatenate(idx_rows, axis=0)
    gate_ref[...] = jnp.concatenate(gate_rows, axis=0)


def _router(x, w_router_t, bias, tm=512):
    n, d = x.shape
    return pl.pallas_call(
        _router_kernel,
        grid=(n // tm,),
        in_specs=[pl.BlockSpec((tm, d), lambda i: (i, 0)),
                  pl.BlockSpec((N_EXPERTS, d), lambda i: (0, 0)),
                  pl.BlockSpec((N_EXPERTS, 1), lambda i: (0, 0))],
        out_specs=[pl.BlockSpec((TOP_K, tm), lambda i: (0, i)),
                   pl.BlockSpec((TOP_K, tm), lambda i: (0, i)),
                   pl.BlockSpec((N_EXPERTS, tm), lambda i: (0, i)),
                   pl.BlockSpec((None, N_EXPERTS, LANES), lambda i: (i, 0, 0))],
        out_shape=[jax.ShapeDtypeStruct((TOP_K, n), jnp.int32),
                   jax.ShapeDtypeStruct((TOP_K, n), F32),
                   jax.ShapeDtypeStruct((N_EXPERTS, n), jnp.int32),
                   jax.ShapeDtypeStruct((n // tm, N_EXPERTS, LANES), jnp.int32)],
        compiler_params=_params(("arbitrary",)),
        name="router",
    )(x, w_router_t, bias.reshape(N_EXPERTS, 1))


def _plan_kernel(mask_ref, idx_ref, off_ref, pos_ref):
    tm = mask_ref.shape[1]
    before = (lax.broadcasted_iota(jnp.int32, (tm, tm), 0)
              < lax.broadcasted_iota(jnp.int32, (tm, tm), 1)).astype(BF16)
    rank = _dot(mask_ref[...].astype(BF16), before).astype(jnp.int32)
    row_of = off_ref[...] + rank
    expert = lax.broadcasted_iota(jnp.int32, (N_EXPERTS, tm), 0)
    rows_ = []
    for k in range(TOP_K):
        hit = expert == idx_ref[k:k + 1, :]
        rows_.append(jnp.sum(jnp.where(hit, row_of, 0), axis=0, keepdims=True))
    pos_ref[...] = jnp.concatenate(rows_, axis=0)


def _plan(mask_t, idx_t, tile_offsets, tm):
    n = mask_t.shape[1]
    return pl.pallas_call(
        _plan_kernel,
        grid=(n // tm,),
        in_specs=[pl.BlockSpec((N_EXPERTS, tm), lambda i: (0, i)),
                  pl.BlockSpec((TOP_K, tm), lambda i: (0, i)),
                  pl.BlockSpec((None, N_EXPERTS, 1), lambda i: (i, 0, 0))],
        out_specs=pl.BlockSpec((TOP_K, tm), lambda i: (0, i)),
        out_shape=jax.ShapeDtypeStruct((TOP_K, n), jnp.int32),
        compiler_params=_params(("arbitrary",)),
        name="plan",
    )(mask_t, idx_t, tile_offsets)


def _block_layout(counts_per_tile, n):
    rows = MOE_ROWS
    nb = (n * TOP_K) // rows + N_EXPERTS
    counts = jnp.sum(counts_per_tile, axis=0)
    padded = (counts + rows - 1) // rows * rows
    padded_ends = jnp.cumsum(padded)
    padded_starts = padded_ends - padded
    tile_offsets = padded_starts[None, :] + jnp.cumsum(counts_per_tile, axis=0) - counts_per_tile
    used = (padded_ends[-1] // rows).astype(jnp.int32)
    block_start = jnp.arange(nb, dtype=jnp.int32) * rows
    block_expert = jnp.minimum(
        jnp.sum((padded_ends[None, :] <= block_start[:, None]).astype(jnp.int32), axis=1),
        N_EXPERTS - 1)
    last = block_expert[jnp.maximum(used - 1, 0)]
    block_expert = jnp.where(jnp.arange(nb) < used, block_expert, last)
    pad_from = (padded_starts + counts).astype(jnp.int32)
    return (tile_offsets.astype(jnp.int32)[:, :, None], block_expert, used.reshape(1),
            pad_from, padded_ends.astype(jnp.int32))


DISPATCH_ROWS = 128
DMA_QUEUES = 2


def _pack_rows(x):
    k = x.shape[1] // 2
    lo = pltpu.bitcast(x[:, :k].astype(BF16).astype(F32), jnp.uint32) >> 16
    hi = pltpu.bitcast(x[:, k:].astype(BF16).astype(F32), jnp.uint32) & jnp.uint32(0xFFFF0000)
    return lo | hi


def _unpack_rows(w):
    lo = pltpu.bitcast(w << 16, F32)
    hi = pltpu.bitcast(w & jnp.uint32(0xFFFF0000), F32)
    return lo, hi


def _dispatch_kernel(pad_from_ref, pad_to_ref, used_ref, pos_hbm, x_ref, xs_hbm, pos_smem,
                     xbuf, zeros, row_sem, pos_sem, pad_sem):
    i = pl.program_id(0)
    steps = pl.num_programs(0)
    tm = pos_smem.shape[2]
    slot = i % 3

    def wait_rows(s):
        for _ in range(TOP_K):
            pltpu.make_async_copy(xbuf.at[s], xs_hbm.at[pl.ds(0, tm)], row_sem.at[s]).wait()

    def pos_copy(blk):
        return pltpu.make_async_copy(pos_hbm.at[blk], pos_smem.at[blk % 2], pos_sem.at[blk % 2])

    @pl.when(i == 0)
    def _():
        pos_copy(0).start()
        zeros[...] = jnp.zeros_like(zeros)
        block = zeros.shape[0]

        def tail_copy(blk):
            start = pl.multiple_of(blk * block, block)
            return pltpu.make_async_copy(zeros, xs_hbm.at[pl.ds(start, block)], pad_sem)

        def tail_start(blk, c):
            tail_copy(blk).start()
            return c

        def tail_wait(blk, c):
            tail_copy(blk).wait()
            return c

        lax.fori_loop(used_ref[0], xs_hbm.shape[0] // block, tail_start, 0)
        lax.fori_loop(used_ref[0], xs_hbm.shape[0] // block, tail_wait, 0)

        def pad_walk(start_not_wait):
            def act(cp):
                if start_not_wait:
                    cp.start()
                else:
                    cp.wait()

            def per_expert(e, carry):
                lo = pad_from_ref[e]
                hi = pad_to_ref[e]
                lo8 = jnp.minimum((lo + SUBLANES - 1) // SUBLANES * SUBLANES, hi)

                def one_row(r, c):
                    act(pltpu.make_async_copy(zeros.at[pl.ds(0, 1)], xs_hbm.at[pl.ds(r, 1)],
                                              pad_sem))
                    return c

                lax.fori_loop(lo, lo8, one_row, 0)
                rem = hi - lo8
                off = lo8
                size = block // 2
                while size >= SUBLANES:
                    @pl.when((rem & size) != 0)
                    def _(off=off, size=size):
                        dst = xs_hbm.at[pl.ds(pl.multiple_of(off, SUBLANES), size)]
                        act(pltpu.make_async_copy(zeros.at[pl.ds(0, size)], dst, pad_sem))
                    off = off + (rem & size)
                    size //= 2
                return carry

            lax.fori_loop(0, N_EXPERTS, per_expert, 0)

        pad_walk(True)
        pad_walk(False)

    @pl.when(i >= 3)
    def _():
        wait_rows(slot)

    @pl.when(i + 1 < steps)
    def _():
        pos_copy(i + 1).start()

    xbuf[slot] = _pack_rows(x_ref[...])
    pos_copy(i).wait()
    pslot = i % 2

    for r in range(tm):
        for k in range(TOP_K):
            pltpu.make_async_copy(xbuf.at[slot, pl.ds(r, 1)],
                                  xs_hbm.at[pl.ds(pos_smem[pslot, k, r], 1)],
                                  row_sem.at[slot]).start(priority=k % DMA_QUEUES)

    @pl.when(i == steps - 1)
    def _():
        wait_rows(slot)

        @pl.when(steps > 1)
        def _():
            wait_rows((i + 2) % 3)

        @pl.when(steps > 2)
        def _():
            wait_rows((i + 1) % 3)


def _dispatch(pos, x, pad_from, pad_to, used, total_rows):
    n, d = x.shape
    tm = DISPATCH_ROWS
    grid_spec = pltpu.PrefetchScalarGridSpec(
        num_scalar_prefetch=3,
        grid=(n // tm,),
        in_specs=[pl.BlockSpec(memory_space=pl.ANY),
                  pl.BlockSpec((tm, d), lambda i, *_: (i, 0))],
        out_specs=pl.BlockSpec(memory_space=pl.ANY),
        scratch_shapes=[pltpu.SMEM((2, TOP_K, tm), jnp.int32),
                        pltpu.VMEM((3, tm, d // 2), jnp.uint32),
                        pltpu.VMEM((MOE_ROWS, d // 2), jnp.uint32),
                        pltpu.SemaphoreType.DMA((3,)),
                        pltpu.SemaphoreType.DMA((2,)),
                        pltpu.SemaphoreType.DMA],
    )
    return pl.pallas_call(
        _dispatch_kernel,
        grid_spec=grid_spec,
        out_shape=jax.ShapeDtypeStruct((total_rows, d // 2), jnp.uint32),
        compiler_params=_params(("arbitrary",)),
        name="dispatch",
    )(pad_from, pad_to, used, pos.reshape(TOP_K, n // tm, tm).transpose(1, 0, 2), x)


def _experts_kernel(be_ref, used_ref, x_ref, wg_ref, wu_ref, wd_ref, y_ref,
                    wg_bf, wu_bf, wd_bf):
    i = pl.program_id(0)
    used = used_ref[0]

    @pl.when(i < used)
    def _():
        changed = jnp.logical_or(i == 0, be_ref[jnp.maximum(i - 1, 0)] != be_ref[i])

        @pl.when(changed)
        def _():
            wg_bf[...] = wg_ref[...].astype(BF16)
            wu_bf[...] = wu_ref[...].astype(BF16)
            wd_bf[...] = wd_ref[...].astype(BF16)

        half = x_ref.shape[1]
        x_lo, x_hi = (t.astype(BF16) for t in _unpack_rows(x_ref[...]))
        gate = _dot(x_lo, wg_bf[:half, :]) + _dot(x_hi, wg_bf[half:, :])
        up = _dot(x_lo, wu_bf[:half, :]) + _dot(x_hi, wu_bf[half:, :])
        hid = jax.nn.silu(gate) * up
        y_ref[...] = _pack_rows(_dot(hid.astype(BF16), wd_bf[...]))

    @pl.when(i >= used)
    def _():
        y_ref[...] = jnp.zeros_like(y_ref)


def _experts(block_expert, used, xs, w_gate, w_up, w_down, layer):
    nb = block_expert.shape[0]
    rows = MOE_ROWS
    d = w_gate.shape[2]
    de = w_gate.shape[3]
    grid_spec = pltpu.PrefetchScalarGridSpec(
        num_scalar_prefetch=2,
        grid=(nb,),
        in_specs=[
            pl.BlockSpec((rows, d // 2), lambda i, be, u: (jnp.minimum(i, u[0] - 1), 0)),
            pl.BlockSpec((None, None, d, de), lambda i, be, u: (layer, be[i], 0, 0)),
            pl.BlockSpec((None, None, d, de), lambda i, be, u: (layer, be[i], 0, 0)),
            pl.BlockSpec((None, None, de, d), lambda i, be, u: (layer, be[i], 0, 0)),
        ],
        out_specs=pl.BlockSpec((rows, d // 2), lambda i, be, u: (i, 0)),
        scratch_shapes=[
            pltpu.VMEM((d, de), BF16),
            pltpu.VMEM((d, de), BF16),
            pltpu.VMEM((de, d), BF16),
        ],
    )
    return pl.pallas_call(
        _experts_kernel,
        grid_spec=grid_spec,
        out_shape=jax.ShapeDtypeStruct((nb * rows, d // 2), jnp.uint32),
        compiler_params=_params(("arbitrary",)),
        name="experts",
    )(block_expert, used, xs, w_gate, w_up, w_down)


def _combine_kernel(pos_hbm, y_hbm, gate_ref, h_ref, wsg_ref, wsu_ref, wsd_ref, g_ref, b_ref,
                    *rest):
    if len(rest) == 11:
        wa_ref, wb_ref, o_ref, pa_ref, pb_ref = rest[:5]
    else:
        wa_ref = wb_ref = pa_ref = pb_ref = None
        o_ref = rest[0]
    ybuf0, ybuf1, ybuf2, pos_smem, row_sem, pos_sem = rest[-6:]
    i = pl.program_id(0)
    steps = pl.num_programs(0)
    tm = h_ref.shape[0]
    last = steps - 1
    ybufs = (ybuf0, ybuf1, ybuf2)

    def pos_copy(t):
        return pltpu.make_async_copy(pos_hbm.at[jnp.minimum(t, last)], pos_smem.at[t % 2],
                                     pos_sem.at[t % 2])

    def row_copy(t, buf, k, r):
        return pltpu.make_async_copy(y_hbm.at[pl.ds(pos_smem[t % 2, k, r], 1)],
                                     ybufs[buf].at[k, pl.ds(r, 1)], row_sem.at[buf])

    def wait_rows(buf):
        for k in range(TOP_K):
            pltpu.make_async_copy(y_hbm.at[pl.ds(0, tm)], ybufs[buf].at[k],
                                  row_sem.at[buf]).wait()

    @pl.when(i == 0)
    def _():
        for t in range(2):
            pos_copy(t).start()
            pos_copy(t).wait()

            def issue(r, carry, t=t):
                for k in range(TOP_K):
                    row_copy(t, t, k, r).start()
                return carry
            lax.fori_loop(0, tm, issue, 0)

    def step(cur, ahead):
        wait_rows(cur)

        @pl.when(i > 0)
        def _():
            pos_copy(i + 2).wait()

        @pl.when(i == 0)
        def _():
            pos_copy(2).start()
            pos_copy(2).wait()

        for r in range(tm):
            for k in range(TOP_K):
                row_copy(i + 2, ahead, k, r).start(priority=k % DMA_QUEUES)
        pos_copy(i + 3).start()

        h = h_ref[...]
        x = h.astype(BF16)
        hid = jax.nn.silu(_dot(x, wsg_ref[...])) * _dot(x, wsu_ref[...])
        y = ALPHA * h + _dot(hid.astype(BF16), wsd_ref[...])

        gates = gate_ref[...]
        half = ybufs[cur].shape[2]
        routed_lo = jnp.zeros((tm, half), F32)
        routed_hi = jnp.zeros((tm, half), F32)
        for k in range(TOP_K):
            lo, hi = _unpack_rows(ybufs[cur][k])
            routed_lo = routed_lo + lo * gates[:, k:k + 1]
            routed_hi = routed_hi + hi * gates[:, k:k + 1]
        y = y + jnp.concatenate([routed_lo, routed_hi], axis=1)
        out = _layer_norm(y, g_ref[...], b_ref[...])
        o_ref[...] = out
        if wa_ref is not None:
            out_bf = out.astype(BF16)
            pa_ref[...] = _dot(out_bf, wa_ref[...]).astype(pa_ref.dtype)
            pb_ref[...] = _dot(out_bf, wb_ref[...]).astype(pb_ref.dtype)

        @pl.when(i == last)
        def _():
            wait_rows((cur + 1) % 3)
            wait_rows(ahead)
            pos_copy(i + 3).wait()

    for phase in range(3):
        @pl.when(i % 3 == phase)
        def _(phase=phase):
            step(phase, (phase + 2) % 3)


def _combine(pos, y_rows, gates, h, ws_gate, ws_up, ws_down, ln_g, ln_b, layer, next_proj=None):
    n, d = h.shape
    tm = COMBINE_ROWS
    ds = ws_gate.shape[2]
    in_specs = [pl.BlockSpec(memory_space=pl.ANY),
                pl.BlockSpec(memory_space=pl.ANY),
                pl.BlockSpec((tm, TOP_K), lambda i: (i, 0)),
                pl.BlockSpec((tm, d), lambda i: (i, 0)),
                _layer_spec((d, ds), layer),
                _layer_spec((d, ds), layer),
                _layer_spec((ds, d), layer),
                _layer_spec((1, d), layer),
                _layer_spec((1, d), layer)]
    out_specs = [pl.BlockSpec((tm, d), lambda i: (i, 0))]
    out_shape = [jax.ShapeDtypeStruct((n, d), F32)]
    operands = [pos.reshape(TOP_K, n // tm, tm).transpose(1, 0, 2), y_rows, gates, h,
                ws_gate, ws_up, ws_down, ln_g, ln_b]
    if next_proj is not None:
        for w, dtype in zip(next_proj, (BF16, F32)):
            width = w.shape[2]
            in_specs.append(_layer_spec((d, width), layer + 1))
            out_specs.append(pl.BlockSpec((tm, width), lambda i: (i, 0)))
            out_shape.append(jax.ShapeDtypeStruct((n, width), dtype))
            operands.append(w)
    return pl.pallas_call(
        _combine_kernel,
        grid=(n // tm,),
        in_specs=in_specs,
        out_specs=out_specs,
        out_shape=out_shape,
        scratch_shapes=[pltpu.VMEM((TOP_K, tm, d // 2), jnp.uint32)] * 3
        + [pltpu.SMEM((2, TOP_K, tm), jnp.int32),
           pltpu.SemaphoreType.DMA((3,)),
           pltpu.SemaphoreType.DMA((2,))],
        compiler_params=_params(("arbitrary",)),
        name="combine",
    )(*operands)


def _alibi_slopes(n):
    return 2.0 ** (-8.0 * jnp.arange(1, n + 1, dtype=F32) / n)


def kernel(x, mem, w_in, a_sinks, g_a, g_b, w_out, ln1_g, ln1_b, wq_m, wk_m, wv_m, wo_m,
           ln2_g, ln2_b, w_router, router_bias, w_gate, w_up, w_down, ws_gate, ws_up, ws_down,
           ln3_g, ln3_b):
    b, length, d = x.shape
    n = b * length
    mlen = mem.shape[1]
    depth = w_in.shape[0]
    h = x.reshape(n, d)
    h_bf = h.astype(BF16)
    mem_bf = mem.reshape(b * mlen, d).astype(BF16)
    slopes_a = _alibi_slopes(A_Q_HEADS)
    slopes_b = _alibi_slopes(B_HEADS)

    w_in_bf = w_in.astype(BF16)
    w_in_a = w_in_bf[:, :, :A_PROJ_WIDTH]
    w_in_b = w_in_bf[:, :, A_PROJ_WIDTH:]
    w_out_bf = w_out.astype(BF16)
    wq_bf, wk_bf, wv_bf, wo_bf = (w.astype(BF16) for w in (wq_m, wk_m, wv_m, wo_m))
    wsg_bf, wsu_bf, wsd_bf = (w.astype(BF16) for w in (ws_gate, ws_up, ws_down))
    row = lambda p: p.reshape(depth, 1, p.shape[1])
    g_b3, ln1_g3, ln1_b3, ln2_g3, ln2_b3, ln3_g3, ln3_b3 = (
        row(p) for p in (g_b, ln1_g, ln1_b, ln2_g, ln2_b, ln3_g, ln3_b))

    proj_a = _matmul(h_bf, w_in_a, 0, 512, A_PROJ_WIDTH, BF16)
    proj_b = _matmul(h_bf, w_in_b, 0, 512, B_WIDTH, F32)
    for l in range(depth):
        mixed_a = _attn_a(proj_a.reshape(b, length, A_PROJ_WIDTH), slopes_a, a_sinks[l], g_a[l])
        out_b = _attn_b(proj_b.reshape(b, length, 3 * B_WIDTH), slopes_b)
        h, h_bf = _mix_out(mixed_a.reshape(n, A_WIDTH), out_b.reshape(n, B_WIDTH), g_b3,
                           w_out_bf, h, ln1_g3, ln1_b3, l)

        km = _matmul(mem_bf, wk_bf, l, b * mlen, d // 2, BF16)
        vm = _matmul(mem_bf, wv_bf, l, b * mlen, d // 2, BF16)
        o = _mem_attn(h_bf.reshape(b, length, d), wq_bf,
                      km.reshape(b, mlen, d), vm.reshape(b, mlen, d), l)
        h = _proj_out(o.reshape(n, d), wo_bf, h, ln2_g3, ln2_b3, l)

        route_tm = 512
        idx_t, gate_t, mask_t, cnt = _router(h, w_router[l].T, router_bias[l], route_tm)
        tile_offsets, block_expert, used, pad_from, pad_to = _block_layout(cnt[:, :, 0], n)
        pos = _plan(mask_t, idx_t, tile_offsets, route_tm)
        xs = _dispatch(pos, h, pad_from, pad_to, used, block_expert.shape[0] * MOE_ROWS)
        y_rows = _experts(block_expert, used, xs, w_gate, w_up, w_down, l)
        next_proj = (w_in_a, w_in_b) if l + 1 < depth else None
        outs = _combine(pos, y_rows, gate_t.T, h, wsg_bf, wsu_bf, wsd_bf, ln3_g3, ln3_b3, l,
                        next_proj)
        h = outs[0]
        if next_proj is not None:
            proj_a, proj_b = outs[1:]
    return h.reshape(b, length, d)
```
